```python
import math
import jax, jax.numpy as jnp
from jax import lax
import numpy as np

D_MODEL = 2048
BATCH = 8
SEQ = 2048
DEPTH = 2
DEC_BATCH = 128
DEC_SEQ = 4
PAST_LEN = 8192
PAGE_SIZE = 128

D_MIX = 2 * D_MODEL
SSD_WIDTH = D_MIX // 2
SSD_HEAD_DIM = 64
SSD_HEADS = SSD_WIDTH // SSD_HEAD_DIM
SSD_GROUPS = 4
SSD_STATE = 128
SSD_CONV = 4
SSD_CHUNK = 128
SSD_CONV_CH = SSD_WIDTH + 2 * SSD_GROUPS * SSD_STATE
ATT_WIDTH = D_MIX // 4
HEAD_DIM = 64
N_Q_HEADS = ATT_WIDTH // HEAD_DIM
N_KV_HEADS = 4
Q_PER_KV = N_Q_HEADS // N_KV_HEADS
WINDOW = 128
ATT_BLOCK = 128
ROPE_THETA = 10000.0
W_BUF = min(WINDOW, PAST_LEN)
SC_WIDTH = D_MIX // 4
SC_CONV = 3
PLE_DIM = 256
EPS = 1e-6
IN_SIZES = (SSD_WIDTH, SSD_CONV_CH, SSD_HEADS,
            ATT_WIDTH, N_KV_HEADS * HEAD_DIM, N_KV_HEADS * HEAD_DIM, ATT_WIDTH,
            SC_WIDTH, SC_WIDTH, SC_WIDTH, SC_WIDTH)
N_IN = sum(IN_SIZES)

kernel_name = "hybrid_ssd_swa_shortconv_step"


def _split_points():
    pts, acc = [], 0
    for s in IN_SIZES[:-1]:
        acc += s
        pts.append(acc)
    return pts


def _rmsnorm(x, w):
    xf = x.astype(jnp.float32)
    y = xf * lax.rsqrt(jnp.mean(xf * xf, axis=-1, keepdims=True) + EPS)
    return (y * w.astype(jnp.float32)).astype(x.dtype)


def _rope(x, pos):
    d = x.shape[-1]
    half = d // 2
    inv_freq = jnp.exp(-math.log(ROPE_THETA) * jnp.arange(half, dtype=jnp.float32) * (2.0 / d))
    ang = pos.astype(jnp.float32)[:, None] * inv_freq[None, :]
    cos = jnp.cos(ang)[None, :, None, :]
    sin = jnp.sin(ang)[None, :, None, :]
    xf = x.astype(jnp.float32)
    x1, x2 = xf[..., :half], xf[..., half:]
    return jnp.concatenate([x1 * cos - x2 * sin, x2 * cos + x1 * sin], axis=-1).astype(x.dtype)


def _causal_dwconv(u, buf, w):
    k = w.shape[0]
    L = u.shape[1]
    full = jnp.concatenate([buf.astype(u.dtype), u], axis=1)
    out = full[:, 0:L] * w[0]
    for j in range(1, k):
        out = out + full[:, j:j + L] * w[j]
    return out, full[:, L:]


def _ssd_scan(x, dt, a, bm, cm, s0):
    b, L, H, P = x.shape
    G, N = bm.shape[2], bm.shape[3]
    R = H // G
    Q = min(SSD_CHUNK, L)
    pad = (-L) % Q
    if pad:
        x = jnp.pad(x, ((0, 0), (0, pad), (0, 0), (0, 0)))
        dt = jnp.pad(dt, ((0, 0), (0, pad), (0, 0)))
        bm = jnp.pad(bm, ((0, 0), (0, pad), (0, 0), (0, 0)))
        cm = jnp.pad(cm, ((0, 0), (0, pad), (0, 0), (0, 0)))
    nc = (L + pad) // Q
    xc = x.astype(jnp.float32).reshape(b, nc, Q, G, R, P)
    dtc = dt.reshape(b, nc, Q, G, R)
    bc = bm.astype(jnp.float32).reshape(b, nc, Q, G, N)
    cc = cm.astype(jnp.float32).reshape(b, nc, Q, G, N)
    a_cum = jnp.cumsum(dtc * a.reshape(G, R), axis=2)
    seg = a_cum[:, :, :, None] - a_cum[:, :, None, :]
    causal = jnp.tril(jnp.ones((Q, Q), dtype=bool))[:, :, None, None]
    decay = jnp.exp(jnp.where(causal, seg, -jnp.inf))
    cb = jnp.einsum('bclgn,bcsgn->bclsg', cc, bc)
    m = cb[..., None] * decay * dtc[:, :, None]
    y_diag = jnp.einsum('bclsgr,bcsgrp->bclgrp', m, xc)
    w_state = jnp.exp(a_cum[:, :, -1:] - a_cum) * dtc
    st = jnp.einsum('bcsgn,bcsgr,bcsgrp->bcgrpn', bc, w_state, xc)
    chunk_decay = jnp.exp(a_cum[:, :, -1])

    def step(carry, inp):
        dec, s_c = inp
        return dec[..., None, None] * carry + s_c, carry

    s_init = s0.astype(jnp.float32).reshape(b, G, R, P, N)
    s_fin, s_in = lax.scan(step, s_init, (jnp.moveaxis(chunk_decay, 1, 0), jnp.moveaxis(st, 1, 0)))
    s_in = jnp.moveaxis(s_in, 0, 1)
    y_off = jnp.einsum('bclgn,bcgrpn,bclgr->bclgrp', cc, s_in, jnp.exp(a_cum))
    y = (y_diag + y_off).reshape(b, nc * Q, H, P)[:, :L]
    return y, s_fin.reshape(b, H, P, N)


def _ssd_branch(z, xbc, dt_raw, conv_buf, s0, conv_w, conv_b, a_log, dt_bias, d_skip, norm_w):
    b, L = z.shape[0], z.shape[1]
    xbc_c, conv_buf_new = _causal_dwconv(xbc, conv_buf, conv_w)
    xbc_c = jax.nn.silu(xbc_c + conv_b)
    xs = xbc_c[..., :SSD_WIDTH].reshape(b, L, SSD_HEADS, SSD_HEAD_DIM)
    bm = xbc_c[..., SSD_WIDTH:SSD_WIDTH + SSD_GROUPS * SSD_STATE].reshape(b, L, SSD_GROUPS, SSD_STATE)
    cm = xbc_c[..., SSD_WIDTH + SSD_GROUPS * SSD_STATE:].reshape(b, L, SSD_GROUPS, SSD_STATE)
    dt = jax.nn.softplus(dt_raw.astype(jnp.float32) + dt_bias.astype(jnp.float32))
    a = -jnp.exp(a_log.astype(jnp.float32))
    y, s_fin = _ssd_scan(xs, dt, a, bm, cm, s0)
    y = y + d_skip.astype(jnp.float32)[:, None] * xs.astype(jnp.float32)
    y = y.reshape(b, L, SSD_WIDTH) * jax.nn.silu(z.astype(jnp.float32))
    yg = y.reshape(b, L, SSD_GROUPS, SSD_WIDTH // SSD_GROUPS)
    yg = yg * lax.rsqrt(jnp.mean(yg * yg, axis=-1, keepdims=True) + EPS)
    y = yg.reshape(b, L, SSD_WIDTH) * norm_w.astype(jnp.float32)
    return y.astype(z.dtype), conv_buf_new, s_fin.astype(s0.dtype)


def _swa_branch(q, k, v, g, k_buf, v_buf, pos0, sinks):
    b, L = q.shape[0], q.shape[1]
    wb = k_buf.shape[1]
    q = q.reshape(b, L, N_Q_HEADS, HEAD_DIM)
    k = k.reshape(b, L, N_KV_HEADS, HEAD_DIM)
    v = v.reshape(b, L, N_KV_HEADS, HEAD_DIM)
    q_pos = pos0 + jnp.arange(L, dtype=jnp.int32)
    q = _rope(q, q_pos)
    k = _rope(k, q_pos)
    k_full = jnp.concatenate([k_buf, k.astype(k_buf.dtype)], axis=1)
    v_full = jnp.concatenate([v_buf, v.astype(v_buf.dtype)], axis=1)
    k_pos = pos0 - wb + jnp.arange(wb + L, dtype=jnp.int32)
    k_new, v_new = k_full[:, -W_BUF:], v_full[:, -W_BUF:]
    if L <= ATT_BLOCK:
        tq, nb = L, 1
        kb, vb, kp = k_full[:, None], v_full[:, None], k_pos[None]
    else:
        tq = ATT_BLOCK
        nb = L // tq
        k5 = k_full.reshape(b, nb + 1, tq, N_KV_HEADS, HEAD_DIM)
        v5 = v_full.reshape(b, nb + 1, tq, N_KV_HEADS, HEAD_DIM)
        kb = jnp.concatenate([k5[:, :-1], k5[:, 1:]], axis=2)
        vb = jnp.concatenate([v5[:, :-1], v5[:, 1:]], axis=2)
        kp5 = k_pos.reshape(nb + 1, tq)
        kp = jnp.concatenate([kp5[:-1], kp5[1:]], axis=1)
    qb = q.reshape(b, nb, tq, N_KV_HEADS, Q_PER_KV, HEAD_DIM)
    qp = q_pos.reshape(nb, tq)
    s = jnp.einsum('bntkgd,bnskd->bnkgts', qb, kb.astype(qb.dtype)).astype(jnp.float32) * (HEAD_DIM ** -0.5)
    kpe, qpe = kp[:, None, :], qp[:, :, None]
    mask = (kpe <= qpe) & (kpe > qpe - WINDOW) & (kpe >= 0)
    s = jnp.where(mask[None, :, None, None], s, -jnp.inf)
    sink = jnp.broadcast_to(sinks.astype(jnp.float32).reshape(N_KV_HEADS, Q_PER_KV)[None, None, :, :, None, None],
                            s.shape[:-1] + (1,))
    pr = jax.nn.softmax(jnp.concatenate([s, sink], axis=-1), axis=-1)[..., :-1]
    o = jnp.einsum('bnkgts,bnskd->bntkgd', pr.astype(q.dtype), vb.astype(q.dtype))
    o = o.reshape(b, L, ATT_WIDTH) * jax.nn.silu(g)
    return o, k_new, v_new


def _sc_branch(bg, cg, hh, g, buf, w):
    u = cg * hh
    conv, buf_new = _causal_dwconv(u, buf, w)
    return bg * conv * jax.nn.silu(g), buf_new


def _layer(x, p, pos0, ssd_buf, ssm0, k_buf, v_buf, sc_buf,
           w_in, w_out, norm_pre, norm_post, ssd_conv_w, ssd_conv_b, ssd_a_log, ssd_dt_bias,
           ssd_d, ssd_norm, attn_sinks, sc_conv_w, ple_proj, ple_gate):
    h = _rmsnorm(x, norm_pre)
    proj = jnp.einsum('bld,dn->bln', h, w_in)
    (z, xbc, dt_raw, q, k, v, g_att, b_sc, c_sc, h_sc, g_sc) = jnp.split(proj, _split_points(), axis=-1)
    y_ssd, ssd_buf_new, ssm_new = _ssd_branch(z, xbc, dt_raw, ssd_buf, ssm0, ssd_conv_w, ssd_conv_b,
                                              ssd_a_log, ssd_dt_bias, ssd_d, ssd_norm)
    y_att, k_new, v_new = _swa_branch(q, k, v, g_att, k_buf, v_buf, pos0, attn_sinks)
    y_sc, sc_buf_new = _sc_branch(b_sc, c_sc, h_sc, g_sc, sc_buf, sc_conv_w)
    mix = jnp.einsum('bln,nd->bld', jnp.concatenate([y_ssd, y_att, y_sc], axis=-1), w_out)
    x = x + _rmsnorm(mix, norm_post)
    gate = jax.nn.sigmoid(jnp.einsum('bld,de->ble', x, ple_gate).astype(jnp.float32))
    e = jnp.einsum('blp,pd->bld', p, ple_proj).astype(jnp.float32)
    x = x + (gate * e).astype(x.dtype)
    return x, ssd_buf_new, ssm_new, k_new, v_new, sc_buf_new


def setup_inputs(seed: int = 0) -> dict:
    key = jax.random.key(seed)
    ks = jax.random.split(key, 24)
    f32 = jnp.float32
    nrm = lambda k, shape, scale: jax.random.normal(k, shape, f32) * scale
    dt0 = jnp.exp(jax.random.uniform(ks[16], (DEPTH, SSD_HEADS), f32, math.log(1e-3), math.log(1e-1)))
    return {
        "x_prompt": nrm(ks[0], (BATCH, SEQ, D_MODEL), 1.0),
        "x_sample": nrm(ks[1], (DEC_BATCH, DEC_SEQ, D_MODEL), 1.0),
        "p_prompt": nrm(ks[2], (DEPTH, BATCH, SEQ, PLE_DIM), 1.0),
        "p_sample": nrm(ks[3], (DEPTH, DEC_BATCH, DEC_SEQ, PLE_DIM), 1.0),
        "state_ssd_conv": nrm(ks[4], (DEPTH, DEC_BATCH, SSD_CONV - 1, SSD_CONV_CH), 1.0),
        "state_ssm": nrm(ks[5], (DEPTH, DEC_BATCH, SSD_HEADS, SSD_HEAD_DIM, SSD_STATE), 0.1),
        "cache_k": nrm(ks[6], (DEPTH, DEC_BATCH, W_BUF, N_KV_HEADS, HEAD_DIM), 1.0),
        "cache_v": nrm(ks[7], (DEPTH, DEC_BATCH, W_BUF, N_KV_HEADS, HEAD_DIM), 1.0),
        "state_sc_conv": nrm(ks[8], (DEPTH, DEC_BATCH, SC_CONV - 1, SC_WIDTH), 1.0),
        "w_in": nrm(ks[9], (DEPTH, D_MODEL, N_IN), D_MODEL ** -0.5),
        "w_out": nrm(ks[10], (DEPTH, D_MIX, D_MODEL), D_MIX ** -0.5),
        "norm_pre": 1.0 + nrm(ks[11], (DEPTH, D_MODEL), 0.02),
        "norm_post": 1.0 + nrm(ks[12], (DEPTH, D_MODEL), 0.02),
        "ssd_conv_w": nrm(ks[13], (DEPTH, SSD_CONV, SSD_CONV_CH), SSD_CONV ** -0.5),
        "ssd_conv_b": nrm(ks[14], (DEPTH, SSD_CONV_CH), 0.02),
        "ssd_a_log": jnp.log(jax.random.uniform(ks[15], (DEPTH, SSD_HEADS), f32, 1.0, 16.0)),
        "ssd_dt_bias": dt0 + jnp.log(-jnp.expm1(-dt0)),
        "ssd_d": 1.0 + nrm(ks[17], (DEPTH, SSD_HEADS), 0.02),
        "ssd_norm": 1.0 + nrm(ks[18], (DEPTH, SSD_WIDTH), 0.02),
        "attn_sinks": nrm(ks[19], (DEPTH, N_Q_HEADS), 0.5),
        "sc_conv_w": nrm(ks[20], (DEPTH, SC_CONV, SC_WIDTH), SC_CONV ** -0.5),
        "ple_proj": nrm(ks[21], (DEPTH, PLE_DIM, D_MODEL), PLE_DIM ** -0.5),
        "ple_gate": nrm(ks[22], (DEPTH, D_MODEL, D_MODEL), D_MODEL ** -0.5),
    }


def reference(x_prompt, x_sample, p_prompt, p_sample, state_ssd_conv, state_ssm, cache_k, cache_v,
              state_sc_conv, w_in, w_out, norm_pre, norm_post, ssd_conv_w, ssd_conv_b, ssd_a_log,
              ssd_dt_bias, ssd_d, ssd_norm, attn_sinks, sc_conv_w, ple_proj, ple_gate):
    bp = x_prompt.shape[0]
    dtp = x_prompt.dtype
    z_ssd_buf = jnp.zeros((bp, SSD_CONV - 1, SSD_CONV_CH), dtp)
    z_ssm = jnp.zeros((bp, SSD_HEADS, SSD_HEAD_DIM, SSD_STATE), jnp.float32)
    z_kv = jnp.zeros((bp, WINDOW, N_KV_HEADS, HEAD_DIM), dtp)
    z_sc_buf = jnp.zeros((bp, SC_CONV - 1, SC_WIDTH), dtp)
    yp, ys = x_prompt, x_sample
    pr_conv, pr_ssm, pr_k, pr_v, pr_sc = [], [], [], [], []
    sa_conv, sa_ssm, sa_k, sa_v, sa_sc = [], [], [], [], []
    for i in range(DEPTH):
        lw = (w_in[i], w_out[i], norm_pre[i], norm_post[i], ssd_conv_w[i], ssd_conv_b[i], ssd_a_log[i],
              ssd_dt_bias[i], ssd_d[i], ssd_norm[i], attn_sinks[i], sc_conv_w[i], ple_proj[i], ple_gate[i])
        yp, c1, s1, k1, v1, sc1 = _layer(yp, p_prompt[i], 0, z_ssd_buf, z_ssm, z_kv, z_kv, z_sc_buf, *lw)
        ys, c2, s2, k2, v2, sc2 = _layer(ys, p_sample[i], PAST_LEN, state_ssd_conv[i], state_ssm[i],
                                         cache_k[i], cache_v[i], state_sc_conv[i], *lw)
        pr_conv.append(c1); pr_ssm.append(s1); pr_k.append(k1); pr_v.append(v1); pr_sc.append(sc1)
        sa_conv.append(c2); sa_ssm.append(s2); sa_k.append(k2); sa_v.append(v2); sa_sc.append(sc2)
    return (yp, ys,
            jnp.stack(pr_conv), jnp.stack(pr_ssm), jnp.stack(pr_k), jnp.stack(pr_v), jnp.stack(pr_sc),
            jnp.stack(sa_conv), jnp.stack(sa_ssm), jnp.stack(sa_k), jnp.stack(sa_v), jnp.stack(sa_sc))
```

```python
import functools
import math

import jax
import jax.numpy as jnp
from jax import lax
from jax.experimental import pallas as pl
from jax.experimental.pallas import tpu as pltpu

F32 = jnp.float32
BF16 = jnp.bfloat16

D_MODEL = 2048
D_MIX = 2 * D_MODEL
SSD_WIDTH = D_MIX // 2
SSD_HEAD_DIM = 64
SSD_HEADS = SSD_WIDTH // SSD_HEAD_DIM
SSD_GROUPS = 4
SSD_STATE = 128
SSD_CONV = 4
SSD_CONV_CH = SSD_WIDTH + 2 * SSD_GROUPS * SSD_STATE
ATT_WIDTH = D_MIX // 4
HEAD_DIM = 64
N_Q_HEADS = ATT_WIDTH // HEAD_DIM
N_KV_HEADS = 4
KV_WIDTH = N_KV_HEADS * HEAD_DIM
WINDOW = 128
ROPE_THETA = 10000.0
SC_WIDTH = D_MIX // 4
SC_CONV = 3
PLE_DIM = 256
EPS = 1e-6
PAST_LEN = 8192

LANES = 128
SUBLANES = 8
CHUNK = 128
GROUP_W = SSD_WIDTH // SSD_GROUPS
HEADS_PER_GROUP = SSD_HEADS // SSD_GROUPS

COL_Z = 0
COL_X = 2048
COL_B = 4096
COL_C = 4608
COL_Q = 5120
COL_GATT = 6144
COL_BSC = 7168
COL_CSC = 8192
COL_HSC = 9216
COL_GSC = 10240
COL_K = 11264
COL_V = 11520
COL_DT = 11776
N_PROJ = 12288

VMEM_LIMIT = 56 * 1024 * 1024


def _cparams(ndims):
    return pltpu.CompilerParams(dimension_semantics=("arbitrary",) * ndims, vmem_limit_bytes=VMEM_LIMIT)


def _silu(x):
    return x * jax.nn.sigmoid(x)


def _inproj_kernel(x_ref, nw_ref, w_ref, o_ref, h_ref, *, tm, row_chunk):
    @pl.when(pl.program_id(1) == 0)
    def _():
        def body(r, carry):
            rows = pl.ds(pl.multiple_of(r * row_chunk, row_chunk), row_chunk)
            x = x_ref[rows, :]
            ms = jnp.mean(x * x, axis=-1, keepdims=True)
            h_ref[rows, :] = ((x * lax.rsqrt(ms + EPS)) * nw_ref[...]).astype(BF16)
            return carry

        lax.fori_loop(0, tm // row_chunk, body, 0)

    o_ref[...] = jnp.dot(h_ref[...], w_ref[...], preferred_element_type=F32)


def _in_proj(x2d, nw, w_bf16):
    t = x2d.shape[0]
    tm = min(1024, t)
    tn = 1024
    return pl.pallas_call(
        functools.partial(_inproj_kernel, tm=tm, row_chunk=128),
        grid=(t // tm, N_PROJ // tn),
        in_specs=[
            pl.BlockSpec((tm, D_MODEL), lambda i, j: (i, 0)),
            pl.BlockSpec((1, D_MODEL), lambda i, j: (0, 0)),
            pl.BlockSpec((D_MODEL, tn), lambda i, j: (0, j)),
        ],
        out_specs=pl.BlockSpec((tm, tn), lambda i, j: (i, j)),
        out_shape=jax.ShapeDtypeStruct((t, N_PROJ), F32),
        scratch_shapes=[pltpu.VMEM((tm, D_MODEL), BF16)],
        compiler_params=_cparams(2),
        name="in_proj",
    )(x2d, nw, w_bf16)


def _ssd_kernel(*refs, L, nc, has_state):
    Q = CHUNK
    if has_state:
        (z_ref, x_ref, b_ref, c_ref, dt_ref, cs_ref, ss_ref, cw_ref, cbias_ref, alog_ref, dtb_ref, dsk_ref,
         nw_ref, e_ref, y_ref, cso_ref, sso_ref,
         xpad, xs_scr, bm_scr, cm_scr, y_scr, ea_scr, we_scr, zp_scr, dtp_scr) = refs
    else:
        (z_ref, x_ref, b_ref, c_ref, dt_ref, cw_ref, cbias_ref, alog_ref, dtb_ref, dsk_ref,
         nw_ref, e_ref, y_ref, cso_ref, sso_ref,
         xpad, xs_scr, bm_scr, cm_scr, y_scr, ea_scr, we_scr) = refs
    c = pl.program_id(1)
    padded = L != Q

    @pl.when(c == 0)
    def _init():
        if has_state:
            sso_ref[...] = ss_ref[...]
            xpad[0:SUBLANES, :] = cs_ref[...]
        else:
            sso_ref[...] = jnp.zeros(sso_ref.shape, F32)
            xpad[0:SUBLANES, :] = jnp.zeros((SUBLANES, SSD_CONV_CH), F32)

    if padded:
        xpad[SUBLANES:SUBLANES + Q, :] = jnp.zeros((Q, SSD_CONV_CH), F32)
    xpad[SUBLANES:SUBLANES + L, 0:SSD_WIDTH] = x_ref[...]
    xpad[SUBLANES:SUBLANES + L, SSD_WIDTH:SSD_WIDTH + GROUP_W] = b_ref[...]
    xpad[SUBLANES:SUBLANES + L, SSD_WIDTH + GROUP_W:SSD_CONV_CH] = c_ref[...]

    cwid = 256
    for cb in range(SSD_CONV_CH // cwid):
        cols = slice(cb * cwid, (cb + 1) * cwid)
        base = SUBLANES - (SSD_CONV - 1)
        acc = xpad[base:base + Q, cols] * cw_ref[0:1, cols]
        for j in range(1, SSD_CONV):
            acc = acc + xpad[base + j:base + j + Q, cols] * cw_ref[j:j + 1, cols]
        act = _silu(acc + cbias_ref[:, cols])
        lo = cb * cwid
        if lo < SSD_WIDTH:
            xs_scr[:, lo:lo + cwid] = act
        elif lo < SSD_WIDTH + GROUP_W:
            bm_scr[:, lo - SSD_WIDTH:lo - SSD_WIDTH + cwid] = act.astype(BF16)
        else:
            o = lo - SSD_WIDTH - GROUP_W
            cm_scr[:, o:o + cwid] = act.astype(BF16)

    tail = (SUBLANES - (SSD_CONV - 1) + L) // SUBLANES * SUBLANES
    tail_tile = xpad[tail:tail + SUBLANES, :]

    @pl.when(c == nc - 1)
    def _():
        cso_ref[...] = tail_tile

    if nc > 1:
        xpad[0:SUBLANES, :] = tail_tile

    row = lax.broadcasted_iota(jnp.int32, (Q, LANES), 0)
    if padded:
        dtp_scr[...] = jnp.zeros((Q, LANES), F32)
        dtp_scr[0:L, :] = dt_ref[...]
        dt_raw = dtp_scr[...]
    else:
        dt_raw = dt_ref[...]
    xdt = dt_raw + dtb_ref[...]
    dt = jnp.maximum(xdt, 0.0) + jnp.log1p(jnp.exp(-jnp.abs(xdt)))
    if padded:
        dt = jnp.where(row < L, dt, 0.0)
    a = -jnp.exp(alog_ref[...])
    a_cum = dt * a
    k = 1
    while k < Q:
        a_cum = a_cum + jnp.where(row >= k, pltpu.roll(a_cum, k, 0), 0.0)
        k *= 2
    a_cum_t = a_cum.T
    dt_t = dt.T
    a_last = a_cum[Q - 1:Q, :]

    e_mat = e_ref[...]

    def expand(v):
        hi = v.astype(BF16)
        lo_ = (v - hi.astype(F32)).astype(BF16)
        return (jnp.dot(hi, e_mat, preferred_element_type=F32)
                + jnp.dot(lo_, e_mat, preferred_element_type=F32))

    ea_scr[...] = expand(jnp.exp(a_cum))
    we_scr[...] = expand(jnp.exp(a_last - a_cum) * dt)
    cd_rows = jnp.broadcast_to(jnp.exp(a_cum_t[:, Q - 1:Q]), (LANES, LANES))

    lrow = lax.broadcasted_iota(jnp.int32, (Q, Q), 0)
    scol = lax.broadcasted_iota(jnp.int32, (Q, Q), 1)
    causal = lrow >= scol
    lane = lax.broadcasted_iota(jnp.int32, (Q, LANES), 1)
    low_half = lane < SSD_HEAD_DIM

    for g in range(SSD_GROUPS):
        gcols = slice(g * SSD_STATE, (g + 1) * SSD_STATE)
        cm_g = cm_scr[:, gcols]
        bm_g = bm_scr[:, gcols]
        cbm = lax.dot_general(cm_g, bm_g, (((1,), (1,)), ((), ())), preferred_element_type=F32)
        for p in range(HEADS_PER_GROUP // 2):
            pair = g * (HEADS_PER_GROUP // 2) + p
            ms = []
            for h in (2 * pair, 2 * pair + 1):
                seg = jnp.broadcast_to(a_cum[:, h:h + 1], (Q, Q)) - a_cum_t[h:h + 1, :]
                dec = jnp.exp(jnp.where(causal, seg, -jnp.inf))
                ms.append((cbm * dec * dt_t[h:h + 1, :]).astype(BF16))
            lhs = jnp.concatenate(ms, axis=1)
            xp = xs_scr[:, pair * LANES:(pair + 1) * LANES]
            rhs = jnp.concatenate([jnp.where(low_half, xp, 0.0).astype(BF16),
                                   jnp.where(low_half, 0.0, xp).astype(BF16)], axis=0)
            y_scr[:, pair * LANES:(pair + 1) * LANES] = jnp.dot(lhs, rhs, preferred_element_type=F32)
        wcols = slice(g * GROUP_W, (g + 1) * GROUP_W)
        s_g = sso_ref[wcols, :].astype(BF16)
        y_off = lax.dot_general(cm_g, s_g, (((1,), (1,)), ((), ())), preferred_element_type=F32)
        xs_g = xs_scr[:, wcols]
        y_scr[:, wcols] = y_scr[:, wcols] + y_off * ea_scr[:, wcols] + dsk_ref[:, wcols] * xs_g

    rows_out = Q if not padded else SUBLANES
    if padded:
        zp_scr[...] = jnp.zeros((SUBLANES, SSD_WIDTH), F32)
        zp_scr[0:L, :] = z_ref[...]
    for g in range(SSD_GROUPS):
        wcols = slice(g * GROUP_W, (g + 1) * GROUP_W)
        zg = zp_scr[:, wcols] if padded else z_ref[:, wcols]
        yg = y_scr[0:rows_out, wcols] * _silu(zg)
        msq = jnp.mean(yg * yg, axis=-1, keepdims=True)
        out = (yg * lax.rsqrt(msq + EPS)) * nw_ref[:, wcols]
        y_ref[:, wcols] = out[0:L].astype(y_ref.dtype)

    for g in range(SSD_GROUPS):
        wcols = slice(g * GROUP_W, (g + 1) * GROUP_W)
        gcols = slice(g * SSD_STATE, (g + 1) * SSD_STATE)
        xw_t = (xs_scr[:, wcols] * we_scr[:, wcols]).T.astype(BF16)
        st = jnp.dot(xw_t, bm_scr[:, gcols], preferred_element_type=F32)
        for hh in range(HEADS_PER_GROUP):
            h = g * HEADS_PER_GROUP + hh
            r0 = h * SSD_HEAD_DIM
            sso_ref[r0:r0 + SSD_HEAD_DIM, :] = (sso_ref[r0:r0 + SSD_HEAD_DIM, :] * cd_rows[h:h + 1, :]
                                                + st[hh * SSD_HEAD_DIM:(hh + 1) * SSD_HEAD_DIM, :])


def _ssd_mixer(proj, conv_state_pad, ssm_state, prm, *, batch, seqlen, y_dtype):
    has_state = conv_state_pad is not None
    if seqlen >= CHUNK:
        L, nc = CHUNK, seqlen // CHUNK
        src = proj

        def blk(width, col):
            return pl.BlockSpec((L, width), lambda b, c: (b * nc + c, col // width))

        y_shape = jax.ShapeDtypeStruct((batch * seqlen, SSD_WIDTH), y_dtype)
        y_spec = pl.BlockSpec((L, SSD_WIDTH), lambda b, c: (b * nc + c, 0))
    else:
        L, nc = seqlen, 1
        src = proj.reshape(batch, seqlen, N_PROJ)

        def blk(width, col):
            return pl.BlockSpec((None, L, width), lambda b, c: (b, 0, col // width))

        y_shape = jax.ShapeDtypeStruct((batch, seqlen, SSD_WIDTH), y_dtype)
        y_spec = pl.BlockSpec((None, L, SSD_WIDTH), lambda b, c: (b, 0, 0))

    def const(shape):
        return pl.BlockSpec(shape, lambda b, c: (0,) * len(shape))

    in_specs = [blk(SSD_WIDTH, COL_Z), blk(SSD_WIDTH, COL_X), blk(GROUP_W, COL_B), blk(GROUP_W, COL_C),
                blk(LANES, COL_DT)]
    args = [src, src, src, src, src]
    if has_state:
        in_specs += [pl.BlockSpec((None, SUBLANES, SSD_CONV_CH), lambda b, c: (b, 0, 0)),
                     pl.BlockSpec((None, SSD_WIDTH, SSD_STATE), lambda b, c: (b, 0, 0))]
        args += [conv_state_pad, ssm_state]
    in_specs += [const((SSD_CONV, SSD_CONV_CH)), const((1, SSD_CONV_CH)), const((1, LANES)), const((1, LANES)),
                 const((1, SSD_WIDTH)), const((1, SSD_WIDTH)), const((LANES, SSD_WIDTH))]
    args += [prm["ssd_conv_w"], prm["ssd_conv_b"], prm["a_log"], prm["dt_bias"], prm["d_skip"], prm["ssd_norm"],
             prm["expand"]]
    scratch = [pltpu.VMEM((SUBLANES + CHUNK, SSD_CONV_CH), F32),
               pltpu.VMEM((CHUNK, SSD_WIDTH), F32),
               pltpu.VMEM((CHUNK, GROUP_W), BF16),
               pltpu.VMEM((CHUNK, GROUP_W), BF16),
               pltpu.VMEM((CHUNK, SSD_WIDTH), F32),
               pltpu.VMEM((CHUNK, SSD_WIDTH), F32),
               pltpu.VMEM((CHUNK, SSD_WIDTH), F32)]
    if has_state:
        scratch += [pltpu.VMEM((SUBLANES, SSD_WIDTH), F32), pltpu.VMEM((CHUNK, LANES), F32)]
    y, cso, sso = pl.pallas_call(
        functools.partial(_ssd_kernel, L=L, nc=nc, has_state=has_state),
        grid=(batch, nc),
        in_specs=in_specs,
        out_specs=[y_spec,
                   pl.BlockSpec((None, SUBLANES, SSD_CONV_CH), lambda b, c: (b, 0, 0)),
                   pl.BlockSpec((None, SSD_WIDTH, SSD_STATE), lambda b, c: (b, 0, 0))],
        out_shape=[y_shape,
                   jax.ShapeDtypeStruct((batch, SUBLANES, SSD_CONV_CH), F32),
                   jax.ShapeDtypeStruct((batch, SSD_WIDTH, SSD_STATE), F32)],
        scratch_shapes=scratch,
        compiler_params=_cparams(2),
        name="ssd_prompt" if not has_state else "ssd_decode",
    )(*args)
    off = SUBLANES - (SSD_CONV - 1) + L - (SUBLANES - (SSD_CONV - 1) + L) // SUBLANES * SUBLANES
    return (y.reshape(batch * seqlen, SSD_WIDTH), cso[:, off:off + SSD_CONV - 1, :],
            sso.reshape(batch, SSD_HEADS, SSD_HEAD_DIM, SSD_STATE))


def _att_kernel(*refs, L, nb, carry):
    TQ = CHUNK if L == CHUNK else SUBLANES
    W = WINDOW
    if carry:
        (q_ref, k_ref, v_ref, g_ref, sb_ref, sc_ref, sh_ref, sg_ref, cos_ref, sin_ref, sinks_ref, scw_ref,
         ya_ref, yc_ref, ko_ref, vo_ref, sco_ref,
         kprev, vprev, xpad2) = refs
    else:
        (q_ref, k_ref, v_ref, g_ref, sb_ref, sc_ref, sh_ref, sg_ref, cos_ref, sin_ref, sinks_ref, scw_ref,
         sch_ref, kp_ref, vp_ref,
         ya_ref, yc_ref, ko_ref, vo_ref, sco_ref,
         xpad2, pad_w, pad_kv, kc_scr, vc_scr) = refs
    i = pl.program_id(1)
    padded = L != TQ

    def full(ref, scr):
        if not padded:
            return ref[...]
        scr[...] = jnp.zeros(scr.shape, F32)
        scr[0:L, :] = ref[...]
        return scr[...]

    if padded:
        vals = {name: full(ref, pad_w) for name, ref in
                (("q", q_ref), ("g", g_ref), ("sb", sb_ref), ("sc", sc_ref), ("sh", sh_ref), ("sg", sg_ref))}

        def get(name, cols=slice(None)):
            return vals[name][:, cols]
    else:
        srcs = {"q": q_ref, "g": g_ref, "sb": sb_ref, "sc": sc_ref, "sh": sh_ref, "sg": sg_ref}

        def get(name, cols=slice(None)):
            return srcs[name][:, cols]

    lane = lax.broadcasted_iota(jnp.int32, (TQ, LANES), 1)
    first_half = (lane % HEAD_DIM) < (HEAD_DIM // 2)
    cos = cos_ref[...]
    sin = sin_ref[...]

    def rope(x):
        sw = jnp.where(first_half, pltpu.roll(x, LANES - HEAD_DIM // 2, 1), pltpu.roll(x, HEAD_DIM // 2, 1))
        return x * cos + sw * sin

    k_new = full(k_ref, pad_kv if padded else None)
    k_rot = jnp.concatenate([rope(k_new[:, 0:LANES]), rope(k_new[:, LANES:2 * LANES])], axis=1)
    v_new = full(v_ref, pad_kv if padded else None)
    if carry:
        @pl.when(i == 0)
        def _():
            kprev[...] = jnp.zeros((W, KV_WIDTH), F32)
            vprev[...] = jnp.zeros((W, KV_WIDTH), F32)

        k_prev = kprev[...]
        v_prev = vprev[...]
        k_cur, v_cur = k_rot, v_new
    else:
        k_prev = kp_ref[...]
        v_prev = vp_ref[...]
        kc_scr[...] = jnp.zeros((W, KV_WIDTH), F32)
        kc_scr[0:TQ, :] = k_rot
        vc_scr[...] = jnp.zeros((W, KV_WIDTH), F32)
        vc_scr[0:TQ, :] = v_new
        k_cur = kc_scr[...]
        v_cur = vc_scr[...]

    rowk = lax.broadcasted_iota(jnp.int32, (W, KV_WIDTH), 0)

    def shifted(prev, cur):
        merged = jnp.where(rowk < L, cur, prev)
        return merged if L == W else pltpu.roll(merged, W - L, 0)

    @pl.when(i == nb - 1)
    def _():
        ko_ref[...] = shifted(k_prev, k_cur)
        vo_ref[...] = shifted(v_prev, v_cur)

    if carry:
        kprev[...] = k_cur
        vprev[...] = v_cur

    k_all = jnp.concatenate([k_prev, k_cur], axis=0)
    v_all = jnp.concatenate([v_prev, v_cur], axis=0)
    lane2 = lax.broadcasted_iota(jnp.int32, (2 * W, LANES), 1)
    low2 = lane2 < HEAD_DIM
    low = lane < HEAD_DIM

    t_idx = lax.broadcasted_iota(jnp.int32, (TQ, 2 * W), 0)
    s_idx = lax.broadcasted_iota(jnp.int32, (TQ, 2 * W), 1)
    prev_ok = (s_idx < W) & (s_idx > t_idx)
    if carry:
        prev_ok = prev_ok & (i > 0)
    mask = prev_ok | ((s_idx >= W) & ((s_idx - W) <= t_idx))

    for kvp in range(N_KV_HEADS // 2):
        kpair = k_all[:, kvp * LANES:(kvp + 1) * LANES]
        vpair = v_all[:, kvp * LANES:(kvp + 1) * LANES]
        kroll = pltpu.roll(kpair, HEAD_DIM, 1)
        vroll = pltpu.roll(vpair, HEAD_DIM, 1)
        for sub in range(2):
            kv = 2 * kvp + sub
            if sub == 0:
                k2 = jnp.where(low2, kpair, kroll).astype(BF16)
                va = jnp.where(low2, vpair, 0.0)
                vb = jnp.where(low2, 0.0, vroll)
            else:
                k2 = jnp.where(low2, kroll, kpair).astype(BF16)
                va = jnp.where(low2, vroll, 0.0)
                vb = jnp.where(low2, 0.0, vpair)
            v_rhs = jnp.concatenate([va, vb], axis=0).astype(BF16)
            for qq in range(2):
                qp = 2 * kv + qq
                qcols = slice(qp * LANES, (qp + 1) * LANES)
                q_rot = rope(get("q", qcols)) * (HEAD_DIM ** -0.5)
                ps, dens = [], []
                for half in range(2):
                    h = 2 * qp + half
                    qm = jnp.where(low, q_rot, 0.0) if half == 0 else jnp.where(low, 0.0, q_rot)
                    s = lax.dot_general(qm.astype(BF16), k2, (((1,), (1,)), ((), ())),
                                        preferred_element_type=F32)
                    s = jnp.where(mask, s, -jnp.inf)
                    sink = sinks_ref[h:h + 1, 0:1]
                    m = jnp.maximum(jnp.max(s, axis=-1, keepdims=True), sink)
                    p_un = jnp.exp(s - m)
                    dens.append(jnp.sum(p_un, axis=-1, keepdims=True) + jnp.exp(sink - m))
                    ps.append(p_un.astype(BF16))
                pv = jnp.dot(jnp.concatenate(ps, axis=1), v_rhs, preferred_element_type=F32)
                o = pv / jnp.where(low, dens[0], dens[1])
                ya_ref[:, qcols] = (o * _silu(get("g", qcols)))[0:L].astype(ya_ref.dtype)

    @pl.when(i == 0)
    def _():
        if carry:
            xpad2[0:SUBLANES, :] = jnp.zeros((SUBLANES, SC_WIDTH), F32)
        else:
            xpad2[0:SUBLANES, :] = sch_ref[...]

    u = get("sc") * get("sh")
    xpad2[SUBLANES:SUBLANES + TQ, :] = u
    base = SUBLANES - (SC_CONV - 1)
    conv = xpad2[base:base + TQ, :] * scw_ref[0:1, :]
    for j in range(1, SC_CONV):
        conv = conv + xpad2[base + j:base + j + TQ, :] * scw_ref[j:j + 1, :]
    ysc = (get("sb") * conv) * _silu(get("sg"))
    yc_ref[...] = ysc[0:L].astype(yc_ref.dtype)

    tail = (base + L) // SUBLANES * SUBLANES
    tail_tile = xpad2[tail:tail + SUBLANES, :]

    @pl.when(i == nb - 1)
    def _():
        sco_ref[...] = tail_tile

    if carry:
        xpad2[0:SUBLANES, :] = tail_tile


def _att_mixer(proj, sc_state_pad, k_prev, v_prev, prm, rope_tab, *, batch, seqlen, y_dtype):
    carry = sc_state_pad is None
    if seqlen >= CHUNK:
        L, nb = CHUNK, seqlen // CHUNK
        src = proj

        def blk(width, col):
            return pl.BlockSpec((L, width), lambda b, i: (b * nb + i, col // width))

        def yspec(width):
            return pl.BlockSpec((L, width), lambda b, i: (b * nb + i, 0))

        def yshape(width):
            return jax.ShapeDtypeStruct((batch * seqlen, width), y_dtype)

        tq = CHUNK
    else:
        L, nb = seqlen, 1
        src = proj.reshape(batch, seqlen, N_PROJ)

        def blk(width, col):
            return pl.BlockSpec((None, L, width), lambda b, i: (b, 0, col // width))

        def yspec(width):
            return pl.BlockSpec((None, L, width), lambda b, i: (b, 0, 0))

        def yshape(width):
            return jax.ShapeDtypeStruct((batch, seqlen, width), y_dtype)

        tq = SUBLANES

    def const(shape):
        return pl.BlockSpec(shape, lambda b, i: (0,) * len(shape))

    cos_t, sin_t = rope_tab
    in_specs = [blk(ATT_WIDTH, COL_Q), blk(KV_WIDTH, COL_K), blk(KV_WIDTH, COL_V), blk(ATT_WIDTH, COL_GATT),
                blk(SC_WIDTH, COL_BSC), blk(SC_WIDTH, COL_CSC), blk(SC_WIDTH, COL_HSC), blk(SC_WIDTH, COL_GSC),
                pl.BlockSpec((tq, LANES), lambda b, i: (i, 0)), pl.BlockSpec((tq, LANES), lambda b, i: (i, 0)),
                const((N_Q_HEADS, LANES)), const((SC_CONV, SC_WIDTH))]
    args = [src] * 8 + [cos_t, sin_t, prm["sinks"], prm["sc_conv_w"]]
    scratch = []
    if carry:
        scratch += [pltpu.VMEM((WINDOW, KV_WIDTH), F32), pltpu.VMEM((WINDOW, KV_WIDTH), F32)]
    else:
        in_specs += [pl.BlockSpec((None, SUBLANES, SC_WIDTH), lambda b, i: (b, 0, 0)),
                     pl.BlockSpec((None, WINDOW, KV_WIDTH), lambda b, i: (b, 0, 0)),
                     pl.BlockSpec((None, WINDOW, KV_WIDTH), lambda b, i: (b, 0, 0))]
        args += [sc_state_pad, k_prev, v_prev]
    scratch += [pltpu.VMEM((SUBLANES + tq, SC_WIDTH), F32)]
    if not carry:
        scratch += [pltpu.VMEM((tq, ATT_WIDTH), F32), pltpu.VMEM((tq, KV_WIDTH), F32),
                    pltpu.VMEM((WINDOW, KV_WIDTH), F32), pltpu.VMEM((WINDOW, KV_WIDTH), F32)]
    ya, yc, ko, vo, sco = pl.pallas_call(
        functools.partial(_att_kernel, L=L, nb=nb, carry=carry),
        grid=(batch, nb),
        in_specs=in_specs,
        out_specs=[yspec(ATT_WIDTH), yspec(SC_WIDTH),
                   pl.BlockSpec((None, WINDOW, KV_WIDTH), lambda b, i: (b, 0, 0)),
                   pl.BlockSpec((None, WINDOW, KV_WIDTH), lambda b, i: (b, 0, 0)),
                   pl.BlockSpec((None, SUBLANES, SC_WIDTH), lambda b, i: (b, 0, 0))],
        out_shape=[yshape(ATT_WIDTH), yshape(SC_WIDTH),
                   jax.ShapeDtypeStruct((batch, WINDOW, KV_WIDTH), F32),
                   jax.ShapeDtypeStruct((batch, WINDOW, KV_WIDTH), F32),
                   jax.ShapeDtypeStruct((batch, SUBLANES, SC_WIDTH), F32)],
        scratch_shapes=scratch,
        compiler_params=_cparams(2),
        name="att_prompt" if carry else "att_decode",
    )(*args)
    base = SUBLANES - (SC_CONV - 1)
    off = base + L - (base + L) // SUBLANES * SUBLANES
    return (ya.reshape(batch * seqlen, ATT_WIDTH), yc.reshape(batch * seqlen, SC_WIDTH),
            ko.reshape(batch, WINDOW, N_KV_HEADS, HEAD_DIM), vo.reshape(batch, WINDOW, N_KV_HEADS, HEAD_DIM),
            sco[:, off:off + SC_CONV - 1, :])


OUT_KT = 1024
OUT_NT = 512
N_KSTEPS = D_MIX // OUT_KT
N_NSTEPS = D_MODEL // OUT_NT


def _outproj_kernel(ys_ref, ya_ref, yc_ref, wo_ref, x_ref, npost_ref, p_ref, pp_ref, gw_ref, o_ref,
                    acc_ref, xn_ref, xnb_ref):
    k = pl.program_id(1)

    def mm(y_ref):
        return jnp.dot(y_ref[...].astype(BF16), wo_ref[...], preferred_element_type=F32)

    @pl.when(k == 0)
    def _():
        acc_ref[...] = mm(ys_ref)

    @pl.when(k == 1)
    def _():
        acc_ref[...] += mm(ys_ref)

    @pl.when(k == 2)
    def _():
        acc_ref[...] += mm(ya_ref)

    @pl.when(k == 3)
    def _():
        mix = acc_ref[...] + mm(yc_ref)
        ms = jnp.mean(mix * mix, axis=-1, keepdims=True)
        xn = x_ref[...] + (mix * lax.rsqrt(ms + EPS)) * npost_ref[...]
        xnb_ref[...] = xn.astype(BF16)
        for j in range(N_NSTEPS):
            xn_ref[j] = xn[:, j * OUT_NT:(j + 1) * OUT_NT]

    @pl.when(k >= N_KSTEPS)
    def _():
        j = k - N_KSTEPS
        gate = jax.nn.sigmoid(jnp.dot(xnb_ref[...], gw_ref[...], preferred_element_type=F32))
        e = jnp.dot(p_ref[...].astype(BF16), pp_ref[...], preferred_element_type=F32)
        o_ref[...] = xn_ref[j] + gate * e


def _out_proj(ys, ya, yc, x2d, p2d, prm):
    t = x2d.shape[0]
    tm = min(512, t)

    def nstep(k):
        return jnp.maximum(k - N_KSTEPS, 0)

    return pl.pallas_call(
        _outproj_kernel,
        grid=(t // tm, N_KSTEPS + N_NSTEPS),
        in_specs=[
            pl.BlockSpec((tm, OUT_KT), lambda i, k: (i, jnp.minimum(k, 1))),
            pl.BlockSpec((tm, ATT_WIDTH), lambda i, k: (i, 0)),
            pl.BlockSpec((tm, SC_WIDTH), lambda i, k: (i, 0)),
            pl.BlockSpec((OUT_KT, D_MODEL), lambda i, k: (jnp.minimum(k, N_KSTEPS - 1), 0)),
            pl.BlockSpec((tm, D_MODEL), lambda i, k: (i, 0)),
            pl.BlockSpec((1, D_MODEL), lambda i, k: (0, 0)),
            pl.BlockSpec((tm, PLE_DIM), lambda i, k: (i, 0)),
            pl.BlockSpec((PLE_DIM, OUT_NT), lambda i, k: (0, nstep(k))),
            pl.BlockSpec((D_MODEL, OUT_NT), lambda i, k: (0, nstep(k))),
        ],
        out_specs=pl.BlockSpec((tm, OUT_NT), lambda i, k: (i, nstep(k))),
        out_shape=jax.ShapeDtypeStruct((t, D_MODEL), F32),
        scratch_shapes=[pltpu.VMEM((tm, D_MODEL), F32),
                        pltpu.VMEM((N_NSTEPS, tm, OUT_NT), F32),
                        pltpu.VMEM((tm, D_MODEL), BF16)],
        compiler_params=_cparams(2),
        name="out_proj",
    )(ys, ya, yc, prm["w_out"], x2d, prm["norm_post"], p2d, prm["ple_proj"], prm["ple_gate"])


def _regroup_w_in(w):
    o = 0
    seg = {}
    for name, size in (("z", SSD_WIDTH), ("xbc", SSD_CONV_CH), ("dt", SSD_HEADS), ("q", ATT_WIDTH),
                       ("k", KV_WIDTH), ("v", KV_WIDTH), ("g_att", ATT_WIDTH), ("b_sc", SC_WIDTH),
                       ("c_sc", SC_WIDTH), ("h_sc", SC_WIDTH), ("g_sc", SC_WIDTH)):
        seg[name] = w[:, o:o + size]
        o += size
    pad = jnp.zeros((w.shape[0], N_PROJ - COL_DT - SSD_HEADS), w.dtype)
    return jnp.concatenate([seg["z"], seg["xbc"], seg["q"], seg["g_att"], seg["b_sc"], seg["c_sc"], seg["h_sc"],
                            seg["g_sc"], seg["k"], seg["v"], seg["dt"], pad], axis=1).astype(BF16)


def _lane_pad(v):
    return jnp.pad(v, (0, LANES - v.shape[0])).reshape(1, LANES)


def _layer_params(i, w_in, w_out, norm_pre, norm_post, ssd_conv_w, ssd_conv_b, ssd_a_log, ssd_dt_bias, ssd_d,
                  ssd_norm, attn_sinks, sc_conv_w, ple_proj, ple_gate, expand):
    return {
        "w_in": _regroup_w_in(w_in[i]),
        "w_out": w_out[i].astype(BF16),
        "norm_pre": norm_pre[i].reshape(1, D_MODEL),
        "norm_post": norm_post[i].reshape(1, D_MODEL),
        "ssd_conv_w": ssd_conv_w[i],
        "ssd_conv_b": ssd_conv_b[i].reshape(1, SSD_CONV_CH),
        "a_log": _lane_pad(ssd_a_log[i]),
        "dt_bias": _lane_pad(ssd_dt_bias[i]),
        "d_skip": jnp.repeat(ssd_d[i], SSD_HEAD_DIM).reshape(1, SSD_WIDTH),
        "ssd_norm": ssd_norm[i].reshape(1, SSD_WIDTH),
        "sinks": jnp.broadcast_to(attn_sinks[i][:, None], (N_Q_HEADS, LANES)),
        "sc_conv_w": sc_conv_w[i],
        "ple_proj": ple_proj[i].astype(BF16),
        "ple_gate": ple_gate[i].astype(BF16),
        "expand": expand,
    }


def _rope_tables(pos0, n):
    half = HEAD_DIM // 2
    inv_freq = jnp.exp(-math.log(ROPE_THETA) * jnp.arange(half, dtype=F32) * (2.0 / HEAD_DIM))
    pos = pos0 + jnp.arange(n, dtype=jnp.int32)
    ang = pos.astype(F32)[:, None] * inv_freq[None, :]
    cos, sin = jnp.cos(ang), jnp.sin(ang)
    reps = LANES // HEAD_DIM
    return (jnp.tile(jnp.concatenate([cos, cos], axis=1), (1, reps)),
            jnp.tile(jnp.concatenate([-sin, sin], axis=1), (1, reps)))


def _layer(x2d, p2d, prm, rope_tab, *, batch, seqlen, conv_state, ssm_state, k_prev, v_prev, sc_state):
    decode = conv_state is not None
    y_dtype = F32 if decode else BF16
    proj = _in_proj(x2d, prm["norm_pre"], prm["w_in"])
    if decode:
        conv_pad = jnp.pad(conv_state, ((0, 0), (SUBLANES - (SSD_CONV - 1), 0), (0, 0)))
        sc_pad = jnp.pad(sc_state, ((0, 0), (SUBLANES - (SC_CONV - 1), 0), (0, 0)))
        ssm_in = ssm_state.reshape(batch, SSD_WIDTH, SSD_STATE)
        kp = k_prev.reshape(batch, WINDOW, KV_WIDTH)
        vp = v_prev.reshape(batch, WINDOW, KV_WIDTH)
    else:
        conv_pad = sc_pad = ssm_in = kp = vp = None
    y_ssd, conv_new, ssm_new = _ssd_mixer(proj, conv_pad, ssm_in, prm, batch=batch, seqlen=seqlen, y_dtype=y_dtype)
    y_att, y_sc, k_new, v_new, sc_new = _att_mixer(proj, sc_pad, kp, vp, prm, rope_tab, batch=batch, seqlen=seqlen,
                                                   y_dtype=y_dtype)
    x_new = _out_proj(y_ssd, y_att, y_sc, x2d, p2d, prm)
    return x_new, conv_new, ssm_new, k_new, v_new, sc_new


def kernel(x_prompt, x_sample, p_prompt, p_sample, state_ssd_conv, state_ssm, cache_k, cache_v, state_sc_conv,
           w_in, w_out, norm_pre, norm_post, ssd_conv_w, ssd_conv_b, ssd_a_log, ssd_dt_bias, ssd_d, ssd_norm,
           attn_sinks, sc_conv_w, ple_proj, ple_gate):
    bp, lp, _ = x_prompt.shape
    bs, ls, _ = x_sample.shape
    depth = w_in.shape[0]
    assert lp % CHUNK == 0 and ls <= SUBLANES and cache_k.shape[2] == WINDOW
    head = jnp.arange(LANES, dtype=jnp.int32)[:, None]
    chan = jnp.arange(SSD_WIDTH, dtype=jnp.int32)[None, :]
    expand = (chan // SSD_HEAD_DIM == head).astype(BF16)
    rope_p = _rope_tables(0, lp)
    rope_s = _rope_tables(PAST_LEN, SUBLANES)
    yp = x_prompt.reshape(bp * lp, D_MODEL)
    ys = x_sample.reshape(bs * ls, D_MODEL)
    outs_p = [[] for _ in range(5)]
    outs_s = [[] for _ in range(5)]
    for i in range(depth):
        prm = _layer_params(i, w_in, w_out, norm_pre, norm_post, ssd_conv_w, ssd_conv_b, ssd_a_log, ssd_dt_bias,
                            ssd_d, ssd_norm, attn_sinks, sc_conv_w, ple_proj, ple_gate, expand)
        yp, *st_p = _layer(yp, p_prompt[i].reshape(bp * lp, PLE_DIM), prm, rope_p, batch=bp, seqlen=lp,
                           conv_state=None, ssm_state=None, k_prev=None, v_prev=None, sc_state=None)
        ys, *st_s = _layer(ys, p_sample[i].reshape(bs * ls, PLE_DIM), prm, rope_s, batch=bs, seqlen=ls,
                           conv_state=state_ssd_conv[i], ssm_state=state_ssm[i], k_prev=cache_k[i],
                           v_prev=cache_v[i], sc_state=state_sc_conv[i])
        for acc, val in zip(outs_p, st_p):
            acc.append(val)
        for acc, val in zip(outs_s, st_s):
            acc.append(val)
    return (yp.reshape(bp, lp, D_MODEL), ys.reshape(bs, ls, D_MODEL),
            *[jnp.stack(v) for v in outs_p], *[jnp.stack(v) for v in outs_s])
```

```python
import functools
import math

import jax
import jax.numpy as jnp
from jax import lax
from jax.experimental import pallas as pl
from jax.experimental.pallas import tpu as pltpu

F32 = jnp.float32
BF16 = jnp.bfloat16

D_MODEL = 2048
D_MIX = 2 * D_MODEL
SSD_WIDTH = D_MIX // 2
SSD_HEAD_DIM = 64
SSD_HEADS = SSD_WIDTH // SSD_HEAD_DIM
SSD_GROUPS = 4
SSD_STATE = 128
SSD_CONV = 4
SSD_CONV_CH = SSD_WIDTH + 2 * SSD_GROUPS * SSD_STATE
ATT_WIDTH = D_MIX // 4
HEAD_DIM = 64
N_Q_HEADS = ATT_WIDTH // HEAD_DIM
N_KV_HEADS = 4
KV_WIDTH = N_KV_HEADS * HEAD_DIM
WINDOW = 128
ROPE_THETA = 10000.0
SC_WIDTH = D_MIX // 4
SC_CONV = 3
PLE_DIM = 256
EPS = 1e-6
PAST_LEN = 8192

LANES = 128
SUBLANES = 8
CHUNK = 128
GROUP_W = SSD_WIDTH // SSD_GROUPS
HEADS_PER_GROUP = SSD_HEADS // SSD_GROUPS

COL_Z = 0
COL_X = 2048
COL_B = 4096
COL_C = 4608
COL_Q = 5120
COL_GATT = 6144
COL_BSC = 7168
COL_CSC = 8192
COL_HSC = 9216
COL_GSC = 10240
COL_K = 11264
COL_V = 11520
COL_DT = 11776
N_PROJ = 12288

VMEM_LIMIT = 56 * 1024 * 1024


def _cparams(ndims):
    return pltpu.CompilerParams(dimension_semantics=("arbitrary",) * ndims, vmem_limit_bytes=VMEM_LIMIT)


def _silu(x):
    h = 0.5 * x
    return h + h * jnp.tanh(h)


def _sigmoid(x):
    return 0.5 + 0.5 * jnp.tanh(0.5 * x)


def _inproj_kernel(x_ref, nw_ref, w_ref, o_ref, h_ref, *, tm, row_chunk):
    @pl.when(pl.program_id(1) == 0)
    def _():
        def body(r, carry):
            rows = pl.ds(pl.multiple_of(r * row_chunk, row_chunk), row_chunk)
            x = x_ref[rows, :]
            ms = jnp.mean(x * x, axis=-1, keepdims=True)
            h_ref[rows, :] = ((x * lax.rsqrt(ms + EPS)) * nw_ref[...]).astype(BF16)
            return carry

        lax.fori_loop(0, tm // row_chunk, body, 0)

    o_ref[...] = jnp.dot(h_ref[...], w_ref[...], preferred_element_type=F32)


def _in_proj(x2d, nw, w_bf16):
    t = x2d.shape[0]
    tm = min(1024, t)
    tn = 1024
    return pl.pallas_call(
        functools.partial(_inproj_kernel, tm=tm, row_chunk=128),
        grid=(t // tm, N_PROJ // tn),
        in_specs=[
            pl.BlockSpec((tm, D_MODEL), lambda i, j: (i, 0)),
            pl.BlockSpec((1, D_MODEL), lambda i, j: (0, 0)),
            pl.BlockSpec((D_MODEL, tn), lambda i, j: (0, j)),
        ],
        out_specs=pl.BlockSpec((tm, tn), lambda i, j: (i, j)),
        out_shape=jax.ShapeDtypeStruct((t, N_PROJ), F32),
        scratch_shapes=[pltpu.VMEM((tm, D_MODEL), BF16)],
        compiler_params=_cparams(2),
        name="in_proj",
    )(x2d, nw, w_bf16)


def _ssd_kernel(*refs, L, nc, has_state, n_alias):
    Q = CHUNK
    n_in = (14 if has_state else 12)
    refs = refs[:n_in] + refs[n_in + n_alias:]
    if has_state:
        (z_ref, x_ref, b_ref, c_ref, dt_ref, cs_ref, ss_ref, cw_ref, cbias_ref, alog_ref, dtb_ref, dsk_ref,
         nw_ref, e_ref, y_ref, cso_ref, sso_ref,
         xpad, xs_scr, bm_scr, cm_scr, y_scr, ea_scr, we_scr, zp_scr, dtp_scr) = refs
    else:
        (z_ref, x_ref, b_ref, c_ref, dt_ref, cw_ref, cbias_ref, alog_ref, dtb_ref, dsk_ref,
         nw_ref, e_ref, y_ref, cso_ref, sso_ref,
         xpad, xs_scr, bm_scr, cm_scr, y_scr, ea_scr, we_scr) = refs
    c = pl.program_id(1)
    padded = L != Q

    @pl.when(c == 0)
    def _init():
        if has_state:
            sso_ref[...] = ss_ref[...]
            xpad[0:SUBLANES, :] = cs_ref[...]
        else:
            sso_ref[...] = jnp.zeros(sso_ref.shape, F32)
            xpad[0:SUBLANES, :] = jnp.zeros((SUBLANES, SSD_CONV_CH), F32)

    if padded:
        xpad[SUBLANES:SUBLANES + Q, :] = jnp.zeros((Q, SSD_CONV_CH), F32)
    xpad[SUBLANES:SUBLANES + L, 0:SSD_WIDTH] = x_ref[...]
    xpad[SUBLANES:SUBLANES + L, SSD_WIDTH:SSD_WIDTH + GROUP_W] = b_ref[...]
    xpad[SUBLANES:SUBLANES + L, SSD_WIDTH + GROUP_W:SSD_CONV_CH] = c_ref[...]

    cwid = 256
    for cb in range(SSD_CONV_CH // cwid):
        cols = slice(cb * cwid, (cb + 1) * cwid)
        base = SUBLANES - (SSD_CONV - 1)
        acc = xpad[base:base + Q, cols] * cw_ref[0:1, cols]
        for j in range(1, SSD_CONV):
            acc = acc + xpad[base + j:base + j + Q, cols] * cw_ref[j:j + 1, cols]
        act = _silu(acc + cbias_ref[:, cols])
        lo = cb * cwid
        if lo < SSD_WIDTH:
            xs_scr[:, lo:lo + cwid] = act
        elif lo < SSD_WIDTH + GROUP_W:
            bm_scr[:, lo - SSD_WIDTH:lo - SSD_WIDTH + cwid] = act.astype(BF16)
        else:
            o = lo - SSD_WIDTH - GROUP_W
            cm_scr[:, o:o + cwid] = act.astype(BF16)

    tail = (SUBLANES - (SSD_CONV - 1) + L) // SUBLANES * SUBLANES
    tail_tile = xpad[tail:tail + SUBLANES, :]

    @pl.when(c == nc - 1)
    def _():
        cso_ref[...] = tail_tile

    if nc > 1:
        xpad[0:SUBLANES, :] = tail_tile

    row = lax.broadcasted_iota(jnp.int32, (Q, LANES), 0)
    if padded:
        dtp_scr[...] = jnp.zeros((Q, LANES), F32)
        dtp_scr[0:L, :] = dt_ref[...]
        dt_raw = dtp_scr[...]
    else:
        dt_raw = dt_ref[...]
    xdt = dt_raw + dtb_ref[...]
    dt = jnp.maximum(xdt, 0.0) + jnp.log1p(jnp.exp(-jnp.abs(xdt)))
    if padded:
        dt = jnp.where(row < L, dt, 0.0)
    a = -jnp.exp(alog_ref[...])
    a_cum = dt * a
    k = 1
    while k < Q:
        a_cum = a_cum + jnp.where(row >= k, pltpu.roll(a_cum, k, 0), 0.0)
        k *= 2
    a_cum_t = a_cum.T
    dt_t = dt.T
    a_last = a_cum[Q - 1:Q, :]

    e_mat = e_ref[...]

    def expand(v):
        hi = v.astype(BF16)
        lo_ = (v - hi.astype(F32)).astype(BF16)
        return (jnp.dot(hi, e_mat, preferred_element_type=F32)
                + jnp.dot(lo_, e_mat, preferred_element_type=F32))

    ea_scr[...] = expand(jnp.exp(a_cum))
    we_scr[...] = expand(jnp.exp(a_last - a_cum) * dt)
    cd_rows = jnp.broadcast_to(jnp.exp(a_cum_t[:, Q - 1:Q]), (LANES, LANES))

    lrow = lax.broadcasted_iota(jnp.int32, (Q, Q), 0)
    scol = lax.broadcasted_iota(jnp.int32, (Q, Q), 1)
    causal = lrow >= scol
    lane = lax.broadcasted_iota(jnp.int32, (Q, LANES), 1)
    low_half = lane < SSD_HEAD_DIM

    for g in range(SSD_GROUPS):
        gcols = slice(g * SSD_STATE, (g + 1) * SSD_STATE)
        cm_g = cm_scr[:, gcols]
        bm_g = bm_scr[:, gcols]
        cbm = lax.dot_general(cm_g, bm_g, (((1,), (1,)), ((), ())), preferred_element_type=F32)
        for p in range(HEADS_PER_GROUP // 2):
            pair = g * (HEADS_PER_GROUP // 2) + p
            ms = []
            for h in (2 * pair, 2 * pair + 1):
                seg = jnp.broadcast_to(a_cum[:, h:h + 1], (Q, Q)) - a_cum_t[h:h + 1, :]
                dec = jnp.exp(jnp.where(causal, seg, -jnp.inf))
                ms.append((cbm * dec * dt_t[h:h + 1, :]).astype(BF16))
            lhs = jnp.concatenate(ms, axis=1)
            xp = xs_scr[:, pair * LANES:(pair + 1) * LANES]
            rhs = jnp.concatenate([jnp.where(low_half, xp, 0.0).astype(BF16),
                                   jnp.where(low_half, 0.0, xp).astype(BF16)], axis=0)
            y_scr[:, pair * LANES:(pair + 1) * LANES] = jnp.dot(lhs, rhs, preferred_element_type=F32)
        wcols = slice(g * GROUP_W, (g + 1) * GROUP_W)
        s_g = sso_ref[wcols, :].astype(BF16)
        y_off = lax.dot_general(cm_g, s_g, (((1,), (1,)), ((), ())), preferred_element_type=F32)
        xs_g = xs_scr[:, wcols]
        y_scr[:, wcols] = y_scr[:, wcols] + y_off * ea_scr[:, wcols] + dsk_ref[:, wcols] * xs_g

    rows_out = Q if not padded else SUBLANES
    if padded:
        zp_scr[...] = jnp.zeros((SUBLANES, SSD_WIDTH), F32)
        zp_scr[0:L, :] = z_ref[...]
    for g in range(SSD_GROUPS):
        wcols = slice(g * GROUP_W, (g + 1) * GROUP_W)
        zg = zp_scr[:, wcols] if padded else z_ref[:, wcols]
        yg = y_scr[0:rows_out, wcols] * _silu(zg)
        msq = jnp.mean(yg * yg, axis=-1, keepdims=True)
        out = (yg * lax.rsqrt(msq + EPS)) * nw_ref[:, wcols]
        y_ref[:, wcols] = out[0:L].astype(y_ref.dtype)

    for g in range(SSD_GROUPS):
        wcols = slice(g * GROUP_W, (g + 1) * GROUP_W)
        gcols = slice(g * SSD_STATE, (g + 1) * SSD_STATE)
        xw_t = (xs_scr[:, wcols] * we_scr[:, wcols]).T.astype(BF16)
        st = jnp.dot(xw_t, bm_scr[:, gcols], preferred_element_type=F32)
        for hh in range(HEADS_PER_GROUP):
            h = g * HEADS_PER_GROUP + hh
            r0 = h * SSD_HEAD_DIM
            sso_ref[r0:r0 + SSD_HEAD_DIM, :] = (sso_ref[r0:r0 + SSD_HEAD_DIM, :] * cd_rows[h:h + 1, :]
                                                + st[hh * SSD_HEAD_DIM:(hh + 1) * SSD_HEAD_DIM, :])


def _ssd_mixer(proj, conv_state_pad, ssm_all, ssm_out_prev, prm, *, layer, depth, batch, seqlen, y_dtype):
    has_state = conv_state_pad is not None
    if seqlen >= CHUNK:
        L, nc = CHUNK, seqlen // CHUNK
        src = proj

        def blk(width, col):
            return pl.BlockSpec((L, width), lambda b, c: (b * nc + c, col // width))

        y_shape = jax.ShapeDtypeStruct((batch * seqlen, D_MIX), y_dtype)
        y_spec = pl.BlockSpec((L, SSD_WIDTH), lambda b, c: (b * nc + c, 0))
    else:
        L, nc = seqlen, 1
        src = proj.reshape(batch, seqlen, N_PROJ)

        def blk(width, col):
            return pl.BlockSpec((None, L, width), lambda b, c: (b, 0, col // width))

        y_shape = jax.ShapeDtypeStruct((batch, seqlen, D_MIX), y_dtype)
        y_spec = pl.BlockSpec((None, L, SSD_WIDTH), lambda b, c: (b, 0, 0))

    def const(shape):
        return pl.BlockSpec(shape, lambda b, c: (0,) * len(shape))

    state_spec = pl.BlockSpec((None, None, SSD_WIDTH, SSD_STATE), lambda b, c: (layer, b, 0, 0))
    in_specs = [blk(SSD_WIDTH, COL_Z), blk(SSD_WIDTH, COL_X), blk(GROUP_W, COL_B), blk(GROUP_W, COL_C),
                blk(LANES, COL_DT)]
    args = [src, src, src, src, src]
    if has_state:
        in_specs += [pl.BlockSpec((None, SUBLANES, SSD_CONV_CH), lambda b, c: (b, 0, 0)), state_spec]
        args += [conv_state_pad, ssm_all]
    in_specs += [const((SSD_CONV, SSD_CONV_CH)), const((1, SSD_CONV_CH)), const((1, LANES)), const((1, LANES)),
                 const((1, SSD_WIDTH)), const((1, SSD_WIDTH)), const((LANES, SSD_WIDTH))]
    args += [prm["ssd_conv_w"], prm["ssd_conv_b"], prm["a_log"], prm["dt_bias"], prm["d_skip"], prm["ssd_norm"],
             prm["expand"]]
    aliases = {}
    if ssm_out_prev is not None:
        aliases[len(args)] = 2
        in_specs += [pl.BlockSpec(memory_space=pl.ANY)]
        args += [ssm_out_prev]
    scratch = [pltpu.VMEM((SUBLANES + CHUNK, SSD_CONV_CH), F32),
               pltpu.VMEM((CHUNK, SSD_WIDTH), F32),
               pltpu.VMEM((CHUNK, GROUP_W), BF16),
               pltpu.VMEM((CHUNK, GROUP_W), BF16),
               pltpu.VMEM((CHUNK, SSD_WIDTH), F32),
               pltpu.VMEM((CHUNK, SSD_WIDTH), F32),
               pltpu.VMEM((CHUNK, SSD_WIDTH), F32)]
    if has_state:
        scratch += [pltpu.VMEM((SUBLANES, SSD_WIDTH), F32), pltpu.VMEM((CHUNK, LANES), F32)]
    y, cso, sso = pl.pallas_call(
        functools.partial(_ssd_kernel, L=L, nc=nc, has_state=has_state, n_alias=len(aliases)),
        grid=(batch, nc),
        in_specs=in_specs,
        out_specs=[y_spec,
                   pl.BlockSpec((None, SUBLANES, SSD_CONV_CH), lambda b, c: (b, 0, 0)),
                   state_spec],
        out_shape=[y_shape,
                   jax.ShapeDtypeStruct((batch, SUBLANES, SSD_CONV_CH), F32),
                   jax.ShapeDtypeStruct((depth, batch, SSD_WIDTH, SSD_STATE), F32)],
        scratch_shapes=scratch,
        input_output_aliases=aliases,
        compiler_params=_cparams(2),
        name="ssd_prompt" if not has_state else "ssd_decode",
    )(*args)
    off = SUBLANES - (SSD_CONV - 1) + L - (SUBLANES - (SSD_CONV - 1) + L) // SUBLANES * SUBLANES
    return y, cso[:, off:off + SSD_CONV - 1, :], sso


def _att_kernel(*refs, L, nb, carry, n_alias):
    TQ = CHUNK if L == CHUNK else SUBLANES
    W = WINDOW
    n_in = 12 if carry else 15
    refs = refs[:n_in] + refs[n_in + n_alias:]
    if carry:
        (q_ref, k_ref, v_ref, g_ref, sb_ref, sc_ref, sh_ref, sg_ref, cos_ref, sin_ref, sinks_ref, scw_ref,
         y_ref, ko_ref, vo_ref, sco_ref,
         kprev, vprev, xpad2) = refs
    else:
        (q_ref, k_ref, v_ref, g_ref, sb_ref, sc_ref, sh_ref, sg_ref, cos_ref, sin_ref, sinks_ref, scw_ref,
         sch_ref, kp_ref, vp_ref,
         y_ref, ko_ref, vo_ref, sco_ref,
         xpad2, pad_w, pad_kv, kc_scr, vc_scr) = refs
    i = pl.program_id(1)
    padded = L != TQ

    def full(ref, scr):
        if not padded:
            return ref[...]
        scr[...] = jnp.zeros(scr.shape, F32)
        scr[0:L, :] = ref[...]
        return scr[...]

    if padded:
        vals = {name: full(ref, pad_w) for name, ref in
                (("q", q_ref), ("g", g_ref), ("sb", sb_ref), ("sc", sc_ref), ("sh", sh_ref), ("sg", sg_ref))}

        def get(name, cols=slice(None)):
            return vals[name][:, cols]
    else:
        srcs = {"q": q_ref, "g": g_ref, "sb": sb_ref, "sc": sc_ref, "sh": sh_ref, "sg": sg_ref}

        def get(name, cols=slice(None)):
            return srcs[name][:, cols]

    lane = lax.broadcasted_iota(jnp.int32, (TQ, LANES), 1)
    first_half = (lane % HEAD_DIM) < (HEAD_DIM // 2)
    cos = cos_ref[...]
    sin = sin_ref[...]

    def rope(x):
        sw = jnp.where(first_half, pltpu.roll(x, LANES - HEAD_DIM // 2, 1), pltpu.roll(x, HEAD_DIM // 2, 1))
        return x * cos + sw * sin

    k_new = full(k_ref, pad_kv if padded else None)
    k_rot = jnp.concatenate([rope(k_new[:, 0:LANES]), rope(k_new[:, LANES:2 * LANES])], axis=1)
    v_new = full(v_ref, pad_kv if padded else None)
    if carry:
        @pl.when(i == 0)
        def _():
            kprev[...] = jnp.zeros((W, KV_WIDTH), F32)
            vprev[...] = jnp.zeros((W, KV_WIDTH), F32)

        k_prev = kprev[...]
        v_prev = vprev[...]
        k_cur, v_cur = k_rot, v_new
    else:
        k_prev = kp_ref[...]
        v_prev = vp_ref[...]
        kc_scr[...] = jnp.zeros((W, KV_WIDTH), F32)
        kc_scr[0:TQ, :] = k_rot
        vc_scr[...] = jnp.zeros((W, KV_WIDTH), F32)
        vc_scr[0:TQ, :] = v_new
        k_cur = kc_scr[...]
        v_cur = vc_scr[...]

    rowk = lax.broadcasted_iota(jnp.int32, (W, KV_WIDTH), 0)

    def shifted(prev, cur):
        merged = jnp.where(rowk < L, cur, prev)
        return merged if L == W else pltpu.roll(merged, W - L, 0)

    @pl.when(i == nb - 1)
    def _():
        ko_ref[...] = shifted(k_prev, k_cur)
        vo_ref[...] = shifted(v_prev, v_cur)

    if carry:
        kprev[...] = k_cur
        vprev[...] = v_cur

    k_all = jnp.concatenate([k_prev, k_cur], axis=0)
    v_all = jnp.concatenate([v_prev, v_cur], axis=0)
    lane2 = lax.broadcasted_iota(jnp.int32, (2 * W, LANES), 1)
    low2 = lane2 < HEAD_DIM
    low = lane < HEAD_DIM

    t_idx = lax.broadcasted_iota(jnp.int32, (TQ, 2 * W), 0)
    s_idx = lax.broadcasted_iota(jnp.int32, (TQ, 2 * W), 1)
    prev_ok = (s_idx < W) & (s_idx > t_idx)
    if carry:
        prev_ok = prev_ok & (i > 0)
    mask = prev_ok | ((s_idx >= W) & ((s_idx - W) <= t_idx))

    for kvp in range(N_KV_HEADS // 2):
        kpair = k_all[:, kvp * LANES:(kvp + 1) * LANES]
        vpair = v_all[:, kvp * LANES:(kvp + 1) * LANES]
        kroll = pltpu.roll(kpair, HEAD_DIM, 1)
        vroll = pltpu.roll(vpair, HEAD_DIM, 1)
        for sub in range(2):
            kv = 2 * kvp + sub
            if sub == 0:
                k2 = jnp.where(low2, kpair, kroll).astype(BF16)
                va = jnp.where(low2, vpair, 0.0)
                vb = jnp.where(low2, 0.0, vroll)
            else:
                k2 = jnp.where(low2, kroll, kpair).astype(BF16)
                va = jnp.where(low2, vroll, 0.0)
                vb = jnp.where(low2, 0.0, vpair)
            v_rhs = jnp.concatenate([va, vb], axis=0).astype(BF16)
            for qq in range(2):
                qp = 2 * kv + qq
                qcols = slice(qp * LANES, (qp + 1) * LANES)
                q_rot = rope(get("q", qcols)) * (HEAD_DIM ** -0.5)
                ps, dens = [], []
                for half in range(2):
                    h = 2 * qp + half
                    qm = jnp.where(low, q_rot, 0.0) if half == 0 else jnp.where(low, 0.0, q_rot)
                    s = lax.dot_general(qm.astype(BF16), k2, (((1,), (1,)), ((), ())),
                                        preferred_element_type=F32)
                    s = jnp.where(mask, s, -jnp.inf)
                    sink = sinks_ref[h:h + 1, 0:1]
                    m = jnp.maximum(jnp.max(s, axis=-1, keepdims=True), sink)
                    p_un = jnp.exp(s - m)
                    dens.append(jnp.sum(p_un, axis=-1, keepdims=True) + jnp.exp(sink - m))
                    ps.append(p_un.astype(BF16))
                pv = jnp.dot(jnp.concatenate(ps, axis=1), v_rhs, preferred_element_type=F32)
                o = pv / jnp.where(low, dens[0], dens[1])
                y_ref[:, qcols] = (o * _silu(get("g", qcols)))[0:L].astype(y_ref.dtype)

    @pl.when(i == 0)
    def _():
        if carry:
            xpad2[0:SUBLANES, :] = jnp.zeros((SUBLANES, SC_WIDTH), F32)
        else:
            xpad2[0:SUBLANES, :] = sch_ref[...]

    u = get("sc") * get("sh")
    xpad2[SUBLANES:SUBLANES + TQ, :] = u
    base = SUBLANES - (SC_CONV - 1)
    conv = xpad2[base:base + TQ, :] * scw_ref[0:1, :]
    for j in range(1, SC_CONV):
        conv = conv + xpad2[base + j:base + j + TQ, :] * scw_ref[j:j + 1, :]
    ysc = (get("sb") * conv) * _silu(get("sg"))
    y_ref[:, ATT_WIDTH:ATT_WIDTH + SC_WIDTH] = ysc[0:L].astype(y_ref.dtype)

    tail = (base + L) // SUBLANES * SUBLANES
    tail_tile = xpad2[tail:tail + SUBLANES, :]

    @pl.when(i == nb - 1)
    def _():
        sco_ref[...] = tail_tile

    if carry:
        xpad2[0:SUBLANES, :] = tail_tile


def _att_mixer(proj, y_mix, sc_state_pad, k_all_prev, v_all_prev, kv_out_prev, prm, rope_tab, *, layer, depth,
               batch, seqlen):
    carry = sc_state_pad is None
    if seqlen >= CHUNK:
        L, nb = CHUNK, seqlen // CHUNK
        src = proj

        def blk(width, col):
            return pl.BlockSpec((L, width), lambda b, i: (b * nb + i, col // width))

        y_spec = pl.BlockSpec((L, ATT_WIDTH + SC_WIDTH), lambda b, i: (b * nb + i, 1))
        tq = CHUNK
    else:
        L, nb = seqlen, 1
        src = proj.reshape(batch, seqlen, N_PROJ)

        def blk(width, col):
            return pl.BlockSpec((None, L, width), lambda b, i: (b, 0, col // width))

        y_spec = pl.BlockSpec((None, L, ATT_WIDTH + SC_WIDTH), lambda b, i: (b, 0, 1))
        tq = SUBLANES

    def const(shape):
        return pl.BlockSpec(shape, lambda b, i: (0,) * len(shape))

    cos_t, sin_t = rope_tab
    in_specs = [blk(ATT_WIDTH, COL_Q), blk(KV_WIDTH, COL_K), blk(KV_WIDTH, COL_V), blk(ATT_WIDTH, COL_GATT),
                blk(SC_WIDTH, COL_BSC), blk(SC_WIDTH, COL_CSC), blk(SC_WIDTH, COL_HSC), blk(SC_WIDTH, COL_GSC),
                pl.BlockSpec((tq, LANES), lambda b, i: (i, 0)), pl.BlockSpec((tq, LANES), lambda b, i: (i, 0)),
                const((N_Q_HEADS, LANES)), const((SC_CONV, SC_WIDTH))]
    args = [src] * 8 + [cos_t, sin_t, prm["sinks"], prm["sc_conv_w"]]
    scratch = []
    if carry:
        scratch += [pltpu.VMEM((WINDOW, KV_WIDTH), F32), pltpu.VMEM((WINDOW, KV_WIDTH), F32)]
    kv_spec = pl.BlockSpec((None, None, WINDOW, KV_WIDTH), lambda b, i: (layer, b, 0, 0))
    if not carry:
        in_specs += [pl.BlockSpec((None, SUBLANES, SC_WIDTH), lambda b, i: (b, 0, 0)), kv_spec, kv_spec]
        args += [sc_state_pad, k_all_prev, v_all_prev]
    scratch += [pltpu.VMEM((SUBLANES + tq, SC_WIDTH), F32)]
    if not carry:
        scratch += [pltpu.VMEM((tq, ATT_WIDTH), F32), pltpu.VMEM((tq, KV_WIDTH), F32),
                    pltpu.VMEM((WINDOW, KV_WIDTH), F32), pltpu.VMEM((WINDOW, KV_WIDTH), F32)]
    aliases = {len(args): 0}
    in_specs += [pl.BlockSpec(memory_space=pl.ANY)]
    args += [y_mix]
    if kv_out_prev is not None:
        for out_idx, buf in zip((1, 2), kv_out_prev):
            aliases[len(args)] = out_idx
            in_specs += [pl.BlockSpec(memory_space=pl.ANY)]
            args += [buf]
    y, ko, vo, sco = pl.pallas_call(
        functools.partial(_att_kernel, L=L, nb=nb, carry=carry, n_alias=len(aliases)),
        grid=(batch, nb),
        in_specs=in_specs,
        out_specs=[y_spec, kv_spec, kv_spec,
                   pl.BlockSpec((None, SUBLANES, SC_WIDTH), lambda b, i: (b, 0, 0))],
        out_shape=[jax.ShapeDtypeStruct(y_mix.shape, y_mix.dtype),
                   jax.ShapeDtypeStruct((depth, batch, WINDOW, KV_WIDTH), F32),
                   jax.ShapeDtypeStruct((depth, batch, WINDOW, KV_WIDTH), F32),
                   jax.ShapeDtypeStruct((batch, SUBLANES, SC_WIDTH), F32)],
        scratch_shapes=scratch,
        input_output_aliases=aliases,
        compiler_params=_cparams(2),
        name="att_prompt" if carry else "att_decode",
    )(*args)
    base = SUBLANES - (SC_CONV - 1)
    off = base + L - (base + L) // SUBLANES * SUBLANES
    return y, (ko, vo), sco[:, off:off + SC_CONV - 1, :]


OUT_TM = 1024
OUT_KT = 1024
OUT_NT = 512
N_KSTEPS = D_MIX // OUT_KT
N_NSTEPS = D_MODEL // OUT_NT


def _outproj_kernel(y_ref, wo_ref, x_ref, npost_ref, p_ref, pp_ref, gw_ref, o_ref, acc_ref, xnb_ref):
    k = pl.program_id(1)

    @pl.when(k < N_KSTEPS)
    def _():
        y = y_ref[...].astype(BF16)
        for j in range(N_NSTEPS):
            part = jnp.dot(y, wo_ref[:, j * OUT_NT:(j + 1) * OUT_NT], preferred_element_type=F32)

            @pl.when(k == 0)
            def _():
                acc_ref[j] = part

            @pl.when(k > 0)
            def _():
                acc_ref[j] += part

    @pl.when(k == N_KSTEPS - 1)
    def _():
        ssq = None
        for j in range(N_NSTEPS):
            mix = acc_ref[j]
            s = jnp.sum(mix * mix, axis=-1, keepdims=True)
            ssq = s if ssq is None else ssq + s
        scale = lax.rsqrt(ssq * (1.0 / D_MODEL) + EPS)
        for j in range(N_NSTEPS):
            cols = slice(j * OUT_NT, (j + 1) * OUT_NT)
            xn = x_ref[:, cols] + (acc_ref[j] * scale) * npost_ref[:, cols]
            acc_ref[j] = xn
            xnb_ref[:, cols] = xn.astype(BF16)

    @pl.when(k >= N_KSTEPS)
    def _():
        j = k - N_KSTEPS
        gate = _sigmoid(jnp.dot(xnb_ref[...], gw_ref[...], preferred_element_type=F32))
        e = jnp.dot(p_ref[...].astype(BF16), pp_ref[...], preferred_element_type=F32)
        o_ref[...] = acc_ref[j] + gate * e


def _out_proj(y_mix, x2d, p_all, prm, *, layer):
    t = x2d.shape[0]
    tm = min(OUT_TM, t)

    def nstep(k):
        return jnp.maximum(k - N_KSTEPS, 0)

    return pl.pallas_call(
        _outproj_kernel,
        grid=(t // tm, N_KSTEPS + N_NSTEPS),
        in_specs=[
            pl.BlockSpec((tm, OUT_KT), lambda i, k: (i, jnp.minimum(k, N_KSTEPS - 1))),
            pl.BlockSpec((OUT_KT, D_MODEL), lambda i, k: (jnp.minimum(k, N_KSTEPS - 1), 0)),
            pl.BlockSpec((tm, D_MODEL), lambda i, k: (i, 0)),
            pl.BlockSpec((1, D_MODEL), lambda i, k: (0, 0)),
            pl.BlockSpec((None, tm, PLE_DIM), lambda i, k: (layer, i, 0)),
            pl.BlockSpec((PLE_DIM, OUT_NT), lambda i, k: (0, nstep(k))),
            pl.BlockSpec((D_MODEL, OUT_NT), lambda i, k: (0, nstep(k))),
        ],
        out_specs=pl.BlockSpec((tm, OUT_NT), lambda i, k: (i, nstep(k))),
        out_shape=jax.ShapeDtypeStruct((t, D_MODEL), F32),
        scratch_shapes=[pltpu.VMEM((N_NSTEPS, tm, OUT_NT), F32),
                        pltpu.VMEM((tm, D_MODEL), BF16)],
        compiler_params=_cparams(2),
        name="out_proj",
    )(y_mix.reshape(t, D_MIX), prm["w_out"], x2d, prm["norm_post"], p_all, prm["ple_proj"], prm["ple_gate"])


def _regroup_w_in(w):
    o = 0
    seg = {}
    for name, size in (("z", SSD_WIDTH), ("xbc", SSD_CONV_CH), ("dt", SSD_HEADS), ("q", ATT_WIDTH),
                       ("k", KV_WIDTH), ("v", KV_WIDTH), ("g_att", ATT_WIDTH), ("b_sc", SC_WIDTH),
                       ("c_sc", SC_WIDTH), ("h_sc", SC_WIDTH), ("g_sc", SC_WIDTH)):
        seg[name] = w[:, o:o + size]
        o += size
    pad = jnp.zeros((w.shape[0], N_PROJ - COL_DT - SSD_HEADS), w.dtype)
    return jnp.concatenate([seg["z"], seg["xbc"], seg["q"], seg["g_att"], seg["b_sc"], seg["c_sc"], seg["h_sc"],
                            seg["g_sc"], seg["k"], seg["v"], seg["dt"], pad], axis=1)


def _lane_pad(v):
    return jnp.pad(v, (0, LANES - v.shape[0])).reshape(1, LANES)


def _layer_params(i, w_in, w_out, norm_pre, norm_post, ssd_conv_w, ssd_conv_b, ssd_a_log, ssd_dt_bias, ssd_d,
                  ssd_norm, attn_sinks, sc_conv_w, ple_proj, ple_gate, expand):
    return {
        "w_in": _regroup_w_in(w_in[i].astype(BF16)),
        "w_out": w_out[i].astype(BF16),
        "norm_pre": norm_pre[i].reshape(1, D_MODEL),
        "norm_post": norm_post[i].reshape(1, D_MODEL),
        "ssd_conv_w": ssd_conv_w[i],
        "ssd_conv_b": ssd_conv_b[i].reshape(1, SSD_CONV_CH),
        "a_log": _lane_pad(ssd_a_log[i]),
        "dt_bias": _lane_pad(ssd_dt_bias[i]),
        "d_skip": jnp.repeat(ssd_d[i], SSD_HEAD_DIM).reshape(1, SSD_WIDTH),
        "ssd_norm": ssd_norm[i].reshape(1, SSD_WIDTH),
        "sinks": jnp.broadcast_to(attn_sinks[i][:, None], (N_Q_HEADS, LANES)),
        "sc_conv_w": sc_conv_w[i],
        "ple_proj": ple_proj[i].astype(BF16),
        "ple_gate": ple_gate[i].astype(BF16),
        "expand": expand,
    }


def _rope_tables(pos0, n):
    half = HEAD_DIM // 2
    inv_freq = jnp.exp(-math.log(ROPE_THETA) * jnp.arange(half, dtype=F32) * (2.0 / HEAD_DIM))
    pos = pos0 + jnp.arange(n, dtype=jnp.int32)
    ang = pos.astype(F32)[:, None] * inv_freq[None, :]
    cos, sin = jnp.cos(ang), jnp.sin(ang)
    reps = LANES // HEAD_DIM
    return (jnp.tile(jnp.concatenate([cos, cos], axis=1), (1, reps)),
            jnp.tile(jnp.concatenate([-sin, sin], axis=1), (1, reps)))


def _layer(x2d, p_all, prm, rope_tab, carried, *, layer, depth, batch, seqlen, conv_state, ssm_all, k_all, v_all,
           sc_state):
    decode = conv_state is not None
    ssm_out_prev, kv_out_prev = carried
    proj = _in_proj(x2d, prm["norm_pre"], prm["w_in"])
    if decode:
        conv_pad = jnp.pad(conv_state, ((0, 0), (SUBLANES - (SSD_CONV - 1), 0), (0, 0)))
        sc_pad = jnp.pad(sc_state, ((0, 0), (SUBLANES - (SC_CONV - 1), 0), (0, 0)))
    else:
        conv_pad = sc_pad = None
    y_mix, conv_new, ssm_out = _ssd_mixer(proj, conv_pad, ssm_all, ssm_out_prev, prm, layer=layer, depth=depth,
                                          batch=batch, seqlen=seqlen, y_dtype=F32 if decode else BF16)
    y_mix, kv_out, sc_new = _att_mixer(proj, y_mix, sc_pad, k_all, v_all, kv_out_prev, prm, rope_tab, layer=layer,
                                       depth=depth, batch=batch, seqlen=seqlen)
    x_new = _out_proj(y_mix, x2d, p_all, prm, layer=layer)
    return x_new, conv_new, sc_new, (ssm_out, kv_out)


def kernel(x_prompt, x_sample, p_prompt, p_sample, state_ssd_conv, state_ssm, cache_k, cache_v, state_sc_conv,
           w_in, w_out, norm_pre, norm_post, ssd_conv_w, ssd_conv_b, ssd_a_log, ssd_dt_bias, ssd_d, ssd_norm,
           attn_sinks, sc_conv_w, ple_proj, ple_gate):
    bp, lp, _ = x_prompt.shape
    bs, ls, _ = x_sample.shape
    depth = w_in.shape[0]
    assert lp % CHUNK == 0 and ls <= SUBLANES and cache_k.shape[2] == WINDOW
    head = jnp.arange(LANES, dtype=jnp.int32)[:, None]
    chan = jnp.arange(SSD_WIDTH, dtype=jnp.int32)[None, :]
    expand = (chan // SSD_HEAD_DIM == head).astype(BF16)
    rope_p = _rope_tables(0, lp)
    rope_s = _rope_tables(PAST_LEN, SUBLANES)
    yp = x_prompt.reshape(bp * lp, D_MODEL)
    ys = x_sample.reshape(bs * ls, D_MODEL)
    pp_all = p_prompt.reshape(depth, bp * lp, PLE_DIM)
    ps_all = p_sample.reshape(depth, bs * ls, PLE_DIM)
    ssm_all = state_ssm.reshape(depth, bs, SSD_WIDTH, SSD_STATE)
    k_all = cache_k.reshape(depth, bs, WINDOW, KV_WIDTH)
    v_all = cache_v.reshape(depth, bs, WINDOW, KV_WIDTH)
    carried_p = carried_s = (None, None)
    conv_p, sc_p, conv_s, sc_s = [], [], [], []
    for i in range(depth):
        prm = _layer_params(i, w_in, w_out, norm_pre, norm_post, ssd_conv_w, ssd_conv_b, ssd_a_log, ssd_dt_bias,
                            ssd_d, ssd_norm, attn_sinks, sc_conv_w, ple_proj, ple_gate, expand)
        yp, cv, sc, carried_p = _layer(yp, pp_all, prm, rope_p, carried_p, layer=i, depth=depth, batch=bp, seqlen=lp,
                                       conv_state=None, ssm_all=None, k_all=None, v_all=None, sc_state=None)
        conv_p.append(cv)
        sc_p.append(sc)
        ys, cv, sc, carried_s = _layer(ys, ps_all, prm, rope_s, carried_s, layer=i, depth=depth, batch=bs, seqlen=ls,
                                       conv_state=state_ssd_conv[i], ssm_all=ssm_all, k_all=k_all, v_all=v_all,
                                       sc_state=state_sc_conv[i])
        conv_s.append(cv)
        sc_s.append(sc)

    def states(conv, carried, sc, batch):
        ssm_out, (k_out, v_out) = carried
        return (jnp.stack(conv), ssm_out.reshape(depth, batch, SSD_HEADS, SSD_HEAD_DIM, SSD_STATE),
                k_out.reshape(depth, batch, WINDOW, N_KV_HEADS, HEAD_DIM),
                v_out.reshape(depth, batch, WINDOW, N_KV_HEADS, HEAD_DIM), jnp.stack(sc))

    return (yp.reshape(bp, lp, D_MODEL), ys.reshape(bs, ls, D_MODEL),
            *states(conv_p, carried_p, sc_p, bp), *states(conv_s, carried_s, sc_s, bs))
```

```python
import functools
import math

import jax
import jax.numpy as jnp
from jax import lax
from jax.experimental import pallas as pl
from jax.experimental.pallas import tpu as pltpu

F32 = jnp.float32
BF16 = jnp.bfloat16

D_MODEL = 2048
D_MIX = 2 * D_MODEL
SSD_WIDTH = D_MIX // 2
SSD_HEAD_DIM = 64
SSD_HEADS = SSD_WIDTH // SSD_HEAD_DIM
SSD_GROUPS = 4
SSD_STATE = 128
SSD_CONV = 4
SSD_CONV_CH = SSD_WIDTH + 2 * SSD_GROUPS * SSD_STATE
ATT_WIDTH = D_MIX // 4
HEAD_DIM = 64
N_Q_HEADS = ATT_WIDTH // HEAD_DIM
N_KV_HEADS = 4
KV_WIDTH = N_KV_HEADS * HEAD_DIM
WINDOW = 128
ROPE_THETA = 10000.0
SC_WIDTH = D_MIX // 4
SC_CONV = 3
PLE_DIM = 256
EPS = 1e-6
PAST_LEN = 8192

LANES = 128
SUBLANES = 8
CHUNK = 128
GROUP_W = SSD_WIDTH // SSD_GROUPS
HEADS_PER_GROUP = SSD_HEADS // SSD_GROUPS

COL_Z = 0
COL_X = 2048
COL_B = 4096
COL_C = 4608
COL_Q = 5120
COL_GATT = 6144
COL_BSC = 7168
COL_CSC = 8192
COL_HSC = 9216
COL_GSC = 10240
COL_K = 11264
COL_V = 11520
COL_DT = 11776
N_PROJ = 12288

VMEM_LIMIT = 56 * 1024 * 1024


def _cparams(ndims):
    return pltpu.CompilerParams(dimension_semantics=("arbitrary",) * ndims, vmem_limit_bytes=VMEM_LIMIT)


def _silu(x):
    h = 0.5 * x
    return h + h * jnp.tanh(h)


def _sigmoid(x):
    return 0.5 + 0.5 * jnp.tanh(0.5 * x)


def _inproj_kernel(x_ref, nw_ref, w_ref, o_ref, h_ref, *, tm, row_chunk):
    @pl.when(pl.program_id(1) == 0)
    def _():
        def body(r, carry):
            rows = pl.ds(pl.multiple_of(r * row_chunk, row_chunk), row_chunk)
            x = x_ref[rows, :]
            ms = jnp.mean(x * x, axis=-1, keepdims=True)
            h_ref[rows, :] = ((x * lax.rsqrt(ms + EPS)) * nw_ref[...]).astype(BF16)
            return carry

        lax.fori_loop(0, tm // row_chunk, body, 0)

    o_ref[...] = jnp.dot(h_ref[...], w_ref[...], preferred_element_type=F32)


def _in_proj(x2d, nw, w_bf16):
    t = x2d.shape[0]
    tm = min(1024, t)
    tn = 1024
    return pl.pallas_call(
        functools.partial(_inproj_kernel, tm=tm, row_chunk=128),
        grid=(t // tm, N_PROJ // tn),
        in_specs=[
            pl.BlockSpec((tm, D_MODEL), lambda i, j: (i, 0)),
            pl.BlockSpec((1, D_MODEL), lambda i, j: (0, 0)),
            pl.BlockSpec((D_MODEL, tn), lambda i, j: (0, j)),
        ],
        out_specs=pl.BlockSpec((tm, tn), lambda i, j: (i, j)),
        out_shape=jax.ShapeDtypeStruct((t, N_PROJ), F32),
        scratch_shapes=[pltpu.VMEM((tm, D_MODEL), BF16)],
        compiler_params=_cparams(2),
        name="in_proj",
    )(x2d, nw, w_bf16)


def _ssd_kernel(*refs, L, nc, has_state, n_alias):
    Q = CHUNK
    n_in = (14 if has_state else 12)
    refs = refs[:n_in] + refs[n_in + n_alias:]
    if has_state:
        (z_ref, x_ref, b_ref, c_ref, dt_ref, cs_ref, ss_ref, cw_ref, cbias_ref, alog_ref, dtb_ref, dsk_ref,
         nw_ref, e_ref, y_ref, cso_ref, sso_ref,
         xpad, xs_scr, bm_scr, cm_scr, y_scr, ea_scr, we_scr, zp_scr, dtp_scr) = refs
    else:
        (z_ref, x_ref, b_ref, c_ref, dt_ref, cw_ref, cbias_ref, alog_ref, dtb_ref, dsk_ref,
         nw_ref, e_ref, y_ref, cso_ref, sso_ref,
         xpad, xs_scr, bm_scr, cm_scr, y_scr, ea_scr, we_scr) = refs
    c = pl.program_id(1)
    padded = L != Q

    @pl.when(c == 0)
    def _init():
        if has_state:
            sso_ref[...] = ss_ref[...]
            xpad[0:SUBLANES, :] = cs_ref[...]
        else:
            sso_ref[...] = jnp.zeros(sso_ref.shape, F32)
            xpad[0:SUBLANES, :] = jnp.zeros((SUBLANES, SSD_CONV_CH), F32)

    if padded:
        xpad[SUBLANES:SUBLANES + Q, :] = jnp.zeros((Q, SSD_CONV_CH), F32)
    xpad[SUBLANES:SUBLANES + L, 0:SSD_WIDTH] = x_ref[...]
    xpad[SUBLANES:SUBLANES + L, SSD_WIDTH:SSD_WIDTH + GROUP_W] = b_ref[...]
    xpad[SUBLANES:SUBLANES + L, SSD_WIDTH + GROUP_W:SSD_CONV_CH] = c_ref[...]

    cwid = 256
    for cb in range(SSD_CONV_CH // cwid):
        cols = slice(cb * cwid, (cb + 1) * cwid)
        base = SUBLANES - (SSD_CONV - 1)
        acc = xpad[base:base + Q, cols] * cw_ref[0:1, cols]
        for j in range(1, SSD_CONV):
            acc = acc + xpad[base + j:base + j + Q, cols] * cw_ref[j:j + 1, cols]
        act = _silu(acc + cbias_ref[:, cols])
        lo = cb * cwid
        if lo < SSD_WIDTH:
            xs_scr[:, lo:lo + cwid] = act
        elif lo < SSD_WIDTH + GROUP_W:
            bm_scr[:, lo - SSD_WIDTH:lo - SSD_WIDTH + cwid] = act.astype(BF16)
        else:
            o = lo - SSD_WIDTH - GROUP_W
            cm_scr[:, o:o + cwid] = act.astype(BF16)

    tail = (SUBLANES - (SSD_CONV - 1) + L) // SUBLANES * SUBLANES
    tail_tile = xpad[tail:tail + SUBLANES, :]

    @pl.when(c == nc - 1)
    def _():
        cso_ref[...] = tail_tile

    if nc > 1:
        xpad[0:SUBLANES, :] = tail_tile

    row = lax.broadcasted_iota(jnp.int32, (Q, LANES), 0)
    if padded:
        dtp_scr[...] = jnp.zeros((Q, LANES), F32)
        dtp_scr[0:L, :] = dt_ref[...]
        dt_raw = dtp_scr[...]
    else:
        dt_raw = dt_ref[...]
    xdt = dt_raw + dtb_ref[...]
    dt = jnp.maximum(xdt, 0.0) + jnp.log1p(jnp.exp(-jnp.abs(xdt)))
    if padded:
        dt = jnp.where(row < L, dt, 0.0)
    a = -jnp.exp(alog_ref[...])
    a_cum = dt * a
    k = 1
    while k < Q:
        a_cum = a_cum + jnp.where(row >= k, pltpu.roll(a_cum, k, 0), 0.0)
        k *= 2
    a_cum_t = a_cum.T
    dt_t = dt.T
    a_last = a_cum[Q - 1:Q, :]

    e_mat = e_ref[...]

    def expand(v):
        hi = v.astype(BF16)
        lo_ = (v - hi.astype(F32)).astype(BF16)
        return (jnp.dot(hi, e_mat, preferred_element_type=F32)
                + jnp.dot(lo_, e_mat, preferred_element_type=F32))

    ea_scr[...] = expand(jnp.exp(a_cum))
    we_scr[...] = expand(jnp.exp(a_last - a_cum) * dt)
    cd_rows = jnp.broadcast_to(jnp.exp(a_cum_t[:, Q - 1:Q]), (LANES, LANES))

    lrow = lax.broadcasted_iota(jnp.int32, (Q, Q), 0)
    scol = lax.broadcasted_iota(jnp.int32, (Q, Q), 1)
    causal = lrow >= scol
    lane = lax.broadcasted_iota(jnp.int32, (Q, LANES), 1)
    low_half = lane < SSD_HEAD_DIM

    for g in range(SSD_GROUPS):
        gcols = slice(g * SSD_STATE, (g + 1) * SSD_STATE)
        cm_g = cm_scr[:, gcols]
        bm_g = bm_scr[:, gcols]
        cbm = lax.dot_general(cm_g, bm_g, (((1,), (1,)), ((), ())), preferred_element_type=F32)
        for p in range(HEADS_PER_GROUP // 2):
            pair = g * (HEADS_PER_GROUP // 2) + p
            ms = []
            for h in (2 * pair, 2 * pair + 1):
                seg = jnp.broadcast_to(a_cum[:, h:h + 1], (Q, Q)) - a_cum_t[h:h + 1, :]
                dec = jnp.exp(jnp.where(causal, seg, -jnp.inf))
                ms.append((cbm * dec * dt_t[h:h + 1, :]).astype(BF16))
            lhs = jnp.concatenate(ms, axis=1)
            xp = xs_scr[:, pair * LANES:(pair + 1) * LANES]
            rhs = jnp.concatenate([jnp.where(low_half, xp, 0.0).astype(BF16),
                                   jnp.where(low_half, 0.0, xp).astype(BF16)], axis=0)
            y_scr[:, pair * LANES:(pair + 1) * LANES] = jnp.dot(lhs, rhs, preferred_element_type=F32)
        wcols = slice(g * GROUP_W, (g + 1) * GROUP_W)
        s_g = sso_ref[wcols, :].astype(BF16)
        y_off = lax.dot_general(cm_g, s_g, (((1,), (1,)), ((), ())), preferred_element_type=F32)
        xs_g = xs_scr[:, wcols]
        y_scr[:, wcols] = y_scr[:, wcols] + y_off * ea_scr[:, wcols] + dsk_ref[:, wcols] * xs_g

    rows_out = Q if not padded else SUBLANES
    if padded:
        zp_scr[...] = jnp.zeros((SUBLANES, SSD_WIDTH), F32)
        zp_scr[0:L, :] = z_ref[...]
    for g in range(SSD_GROUPS):
        wcols = slice(g * GROUP_W, (g + 1) * GROUP_W)
        zg = zp_scr[:, wcols] if padded else z_ref[:, wcols]
        yg = y_scr[0:rows_out, wcols] * _silu(zg)
        msq = jnp.mean(yg * yg, axis=-1, keepdims=True)
        out = (yg * lax.rsqrt(msq + EPS)) * nw_ref[:, wcols]
        y_ref[:, wcols] = out[0:L].astype(y_ref.dtype)

    for g in range(SSD_GROUPS):
        wcols = slice(g * GROUP_W, (g + 1) * GROUP_W)
        gcols = slice(g * SSD_STATE, (g + 1) * SSD_STATE)
        xw_t = (xs_scr[:, wcols] * we_scr[:, wcols]).T.astype(BF16)
        st = jnp.dot(xw_t, bm_scr[:, gcols], preferred_element_type=F32)
        for hh in range(HEADS_PER_GROUP):
            h = g * HEADS_PER_GROUP + hh
            r0 = h * SSD_HEAD_DIM
            sso_ref[r0:r0 + SSD_HEAD_DIM, :] = (sso_ref[r0:r0 + SSD_HEAD_DIM, :] * cd_rows[h:h + 1, :]
                                                + st[hh * SSD_HEAD_DIM:(hh + 1) * SSD_HEAD_DIM, :])


def _ssd_mixer(proj, conv_state_pad, ssm_all, ssm_out_prev, prm, *, layer, depth, batch, seqlen, y_dtype):
    has_state = conv_state_pad is not None
    if seqlen >= CHUNK:
        L, nc = CHUNK, seqlen // CHUNK
        src = proj

        def blk(width, col):
            return pl.BlockSpec((L, width), lambda b, c: (b * nc + c, col // width))

        y_shape = jax.ShapeDtypeStruct((batch * seqlen, D_MIX), y_dtype)
        y_spec = pl.BlockSpec((L, SSD_WIDTH), lambda b, c: (b * nc + c, 0))
    else:
        L, nc = seqlen, 1
        src = proj.reshape(batch, seqlen, N_PROJ)

        def blk(width, col):
            return pl.BlockSpec((None, L, width), lambda b, c: (b, 0, col // width))

        y_shape = jax.ShapeDtypeStruct((batch, seqlen, D_MIX), y_dtype)
        y_spec = pl.BlockSpec((None, L, SSD_WIDTH), lambda b, c: (b, 0, 0))

    def const(shape):
        return pl.BlockSpec(shape, lambda b, c: (0,) * len(shape))

    state_spec = pl.BlockSpec((None, None, SSD_WIDTH, SSD_STATE), lambda b, c: (layer, b, 0, 0))
    in_specs = [blk(SSD_WIDTH, COL_Z), blk(SSD_WIDTH, COL_X), blk(GROUP_W, COL_B), blk(GROUP_W, COL_C),
                blk(LANES, COL_DT)]
    args = [src, src, src, src, src]
    if has_state:
        in_specs += [pl.BlockSpec((None, SUBLANES, SSD_CONV_CH), lambda b, c: (b, 0, 0)), state_spec]
        args += [conv_state_pad, ssm_all]
    in_specs += [const((SSD_CONV, SSD_CONV_CH)), const((1, SSD_CONV_CH)), const((1, LANES)), const((1, LANES)),
                 const((1, SSD_WIDTH)), const((1, SSD_WIDTH)), const((LANES, SSD_WIDTH))]
    args += [prm["ssd_conv_w"], prm["ssd_conv_b"], prm["a_log"], prm["dt_bias"], prm["d_skip"], prm["ssd_norm"],
             prm["expand"]]
    aliases = {}
    if ssm_out_prev is not None:
        aliases[len(args)] = 2
        in_specs += [pl.BlockSpec(memory_space=pl.ANY)]
        args += [ssm_out_prev]
    scratch = [pltpu.VMEM((SUBLANES + CHUNK, SSD_CONV_CH), F32),
               pltpu.VMEM((CHUNK, SSD_WIDTH), F32),
               pltpu.VMEM((CHUNK, GROUP_W), BF16),
               pltpu.VMEM((CHUNK, GROUP_W), BF16),
               pltpu.VMEM((CHUNK, SSD_WIDTH), F32),
               pltpu.VMEM((CHUNK, SSD_WIDTH), F32),
               pltpu.VMEM((CHUNK, SSD_WIDTH), F32)]
    if has_state:
        scratch += [pltpu.VMEM((SUBLANES, SSD_WIDTH), F32), pltpu.VMEM((CHUNK, LANES), F32)]
    y, cso, sso = pl.pallas_call(
        functools.partial(_ssd_kernel, L=L, nc=nc, has_state=has_state, n_alias=len(aliases)),
        grid=(batch, nc),
        in_specs=in_specs,
        out_specs=[y_spec,
                   pl.BlockSpec((None, SUBLANES, SSD_CONV_CH), lambda b, c: (b, 0, 0)),
                   state_spec],
        out_shape=[y_shape,
                   jax.ShapeDtypeStruct((batch, SUBLANES, SSD_CONV_CH), F32),
                   jax.ShapeDtypeStruct((depth, batch, SSD_WIDTH, SSD_STATE), F32)],
        scratch_shapes=scratch,
        input_output_aliases=aliases,
        compiler_params=_cparams(2),
        name="ssd_prompt" if not has_state else "ssd_decode",
    )(*args)
    off = SUBLANES - (SSD_CONV - 1) + L - (SUBLANES - (SSD_CONV - 1) + L) // SUBLANES * SUBLANES
    return y, cso[:, off:off + SSD_CONV - 1, :], sso


def _att_stack_cols(tq):
    return -(-(N_Q_HEADS // N_KV_HEADS) * tq // LANES) * LANES


def _att_kernel(*refs, L, nb, carry, n_alias):
    TQ = CHUNK if L == CHUNK else SUBLANES
    W = WINDOW
    n_in = 12 if carry else 15
    refs = refs[:n_in] + refs[n_in + n_alias:]
    if carry:
        (q_ref, k_ref, v_ref, g_ref, sb_ref, sc_ref, sh_ref, sg_ref, cos_ref, sin_ref, sinks_ref, scw_ref,
         y_ref, ko_ref, vo_ref, sco_ref,
         kd, vt, xpad2) = refs
    else:
        (q_ref, k_ref, v_ref, g_ref, sb_ref, sc_ref, sh_ref, sg_ref, cos_ref, sin_ref, sinks_ref, scw_ref,
         sch_ref, kp_ref, vp_ref,
         y_ref, ko_ref, vo_ref, sco_ref,
         kd, vt, xpad2, pad_w, pad_kv, kc_scr, vc_scr) = refs
    i = pl.program_id(1)
    padded = L != TQ
    R = _att_stack_cols(TQ)

    def full(ref, scr):
        if not padded:
            return ref[...]
        scr[...] = jnp.zeros(scr.shape, F32)
        scr[0:L, :] = ref[...]
        return scr[...]

    if padded:
        vals = {name: full(ref, pad_w) for name, ref in
                (("q", q_ref), ("g", g_ref), ("sb", sb_ref), ("sc", sc_ref), ("sh", sh_ref), ("sg", sg_ref))}

        def get(name, cols=slice(None)):
            return vals[name][:, cols]
    else:
        srcs = {"q": q_ref, "g": g_ref, "sb": sb_ref, "sc": sc_ref, "sh": sh_ref, "sg": sg_ref}

        def get(name, cols=slice(None)):
            return srcs[name][:, cols]

    lane = lax.broadcasted_iota(jnp.int32, (TQ, LANES), 1)
    first_half = (lane % HEAD_DIM) < (HEAD_DIM // 2)
    cos = cos_ref[...]
    sin = sin_ref[...]

    def rope(x):
        sw = jnp.where(first_half, pltpu.roll(x, LANES - HEAD_DIM // 2, 1), pltpu.roll(x, HEAD_DIM // 2, 1))
        return x * cos + sw * sin

    k_new = full(k_ref, pad_kv if padded else None)
    k_rot = jnp.concatenate([rope(k_new[:, 0:LANES]), rope(k_new[:, LANES:2 * LANES])], axis=1)
    v_new = full(v_ref, pad_kv if padded else None)
    low_w = lax.broadcasted_iota(jnp.int32, (W, LANES), 1) < HEAD_DIM

    def stage_kv(slot, k_blk, v_blk):
        for kvp in range(N_KV_HEADS // 2):
            kpair = k_blk[:, kvp * LANES:(kvp + 1) * LANES]
            kroll = pltpu.roll(kpair, HEAD_DIM, 1)
            kd[slot, 2 * kvp] = jnp.where(low_w, kpair, kroll).astype(BF16)
            kd[slot, 2 * kvp + 1] = jnp.where(low_w, kroll, kpair).astype(BF16)
            vt[slot, kvp] = v_blk[:, kvp * LANES:(kvp + 1) * LANES].T.astype(BF16)

    rowk = lax.broadcasted_iota(jnp.int32, (W, KV_WIDTH), 0)
    if carry:
        cur_slot = lax.rem(i, 2)

        @pl.when(i == 0)
        def _():
            kd[1] = jnp.zeros((N_KV_HEADS, W, LANES), BF16)
            vt[1] = jnp.zeros((N_KV_HEADS // 2, LANES, W), BF16)

        stage_kv(cur_slot, k_rot, v_new)

        @pl.when(i == nb - 1)
        def _():
            ko_ref[...] = k_rot
            vo_ref[...] = v_new
    else:
        k_prev = kp_ref[...]
        v_prev = vp_ref[...]
        kc_scr[...] = jnp.zeros((W, KV_WIDTH), F32)
        kc_scr[0:TQ, :] = k_rot
        vc_scr[...] = jnp.zeros((W, KV_WIDTH), F32)
        vc_scr[0:TQ, :] = v_new
        k_cur = kc_scr[...]
        v_cur = vc_scr[...]
        stage_kv(0, k_prev, v_prev)
        stage_kv(1, k_cur, v_cur)
        ko_ref[...] = pltpu.roll(jnp.where(rowk < L, k_cur, k_prev), W - L, 0)
        vo_ref[...] = pltpu.roll(jnp.where(rowk < L, v_cur, v_prev), W - L, 0)

    s_idx = lax.broadcasted_iota(jnp.int32, (W, R), 0)
    t_idx = lax.broadcasted_iota(jnp.int32, (W, R), 1) & (TQ - 1)
    cur_mask = s_idx <= t_idx
    prev_mask = s_idx > t_idx
    if carry:
        prev_mask = prev_mask & (i > 0)
        slot0_cur = cur_slot == 0
        masks = ((cur_mask & slot0_cur) | (prev_mask & jnp.logical_not(slot0_cur)),
                 (prev_mask & slot0_cur) | (cur_mask & jnp.logical_not(slot0_cur)))
    else:
        masks = (prev_mask, cur_mask)
    low = lane < HEAD_DIM
    heads_per_kv = N_Q_HEADS // N_KV_HEADS

    for kvh in range(N_KV_HEADS):
        kvp, half = divmod(kvh, 2)
        qs = []
        for qq in range(heads_per_kv // 2):
            qp = kvh * (heads_per_kv // 2) + qq
            q_rot = rope(get("q", slice(qp * LANES, (qp + 1) * LANES))) * (HEAD_DIM ** -0.5)
            qs += [jnp.where(low, q_rot, 0.0), jnp.where(low, 0.0, q_rot)]
        if heads_per_kv * TQ < R:
            qs.append(jnp.zeros((R - heads_per_kv * TQ, LANES), F32))
        q_stack = jnp.concatenate(qs, axis=0).astype(BF16)
        s = [jnp.where(masks[slot],
                       lax.dot_general(kd[slot, kvh], q_stack, (((1,), (1,)), ((), ())),
                                       preferred_element_type=F32), -jnp.inf) for slot in range(2)]
        sink = sinks_ref[kvh:kvh + 1, :]
        m = jnp.maximum(jnp.maximum(jnp.max(s[0], axis=0, keepdims=True), jnp.max(s[1], axis=0, keepdims=True)),
                        sink)
        p_un = [jnp.exp(s[slot] - m) for slot in range(2)]
        den = (jnp.sum(p_un[0], axis=0, keepdims=True) + jnp.sum(p_un[1], axis=0, keepdims=True)
               + jnp.exp(sink - m))
        o_t = (jnp.dot(vt[0, kvp], p_un[0].astype(BF16), preferred_element_type=F32)
               + jnp.dot(vt[1, kvp], p_un[1].astype(BF16), preferred_element_type=F32)) * (1.0 / den)
        for qq in range(heads_per_kv // 2):
            qp = kvh * (heads_per_kv // 2) + qq
            qcols = slice(qp * LANES, (qp + 1) * LANES)
            if carry:
                rows = slice(half * HEAD_DIM, (half + 1) * HEAD_DIM)
                o = jnp.concatenate([o_t[rows, (2 * qq) * TQ:(2 * qq + 1) * TQ],
                                     o_t[rows, (2 * qq + 1) * TQ:(2 * qq + 2) * TQ]], axis=0).T
            else:
                if qq == 0:
                    o_all = o_t.T
                oa = o_all[(2 * qq) * TQ:(2 * qq + 1) * TQ, :]
                ob = o_all[(2 * qq + 1) * TQ:(2 * qq + 2) * TQ, :]
                if half == 0:
                    ob = pltpu.roll(ob, HEAD_DIM, 1)
                else:
                    oa = pltpu.roll(oa, HEAD_DIM, 1)
                o = jnp.where(low, oa, ob)
            y_ref[:, qcols] = (o * _silu(get("g", qcols)))[0:L].astype(y_ref.dtype)

    @pl.when(i == 0)
    def _():
        if carry:
            xpad2[0:SUBLANES, :] = jnp.zeros((SUBLANES, SC_WIDTH), F32)
        else:
            xpad2[0:SUBLANES, :] = sch_ref[...]

    u = get("sc") * get("sh")
    xpad2[SUBLANES:SUBLANES + TQ, :] = u
    base = SUBLANES - (SC_CONV - 1)
    conv = xpad2[base:base + TQ, :] * scw_ref[0:1, :]
    for j in range(1, SC_CONV):
        conv = conv + xpad2[base + j:base + j + TQ, :] * scw_ref[j:j + 1, :]
    ysc = (get("sb") * conv) * _silu(get("sg"))
    y_ref[:, ATT_WIDTH:ATT_WIDTH + SC_WIDTH] = ysc[0:L].astype(y_ref.dtype)

    tail = (base + L) // SUBLANES * SUBLANES
    tail_tile = xpad2[tail:tail + SUBLANES, :]

    @pl.when(i == nb - 1)
    def _():
        sco_ref[...] = tail_tile

    if carry:
        xpad2[0:SUBLANES, :] = tail_tile


def _att_mixer(proj, y_mix, sc_state_pad, k_all_prev, v_all_prev, kv_out_prev, prm, rope_tab, *, layer, depth,
               batch, seqlen):
    carry = sc_state_pad is None
    if seqlen >= CHUNK:
        L, nb = CHUNK, seqlen // CHUNK
        src = proj

        def blk(width, col):
            return pl.BlockSpec((L, width), lambda b, i: (b * nb + i, col // width))

        y_spec = pl.BlockSpec((L, ATT_WIDTH + SC_WIDTH), lambda b, i: (b * nb + i, 1))
        tq = CHUNK
    else:
        L, nb = seqlen, 1
        src = proj.reshape(batch, seqlen, N_PROJ)

        def blk(width, col):
            return pl.BlockSpec((None, L, width), lambda b, i: (b, 0, col // width))

        y_spec = pl.BlockSpec((None, L, ATT_WIDTH + SC_WIDTH), lambda b, i: (b, 0, 1))
        tq = SUBLANES

    def const(shape):
        return pl.BlockSpec(shape, lambda b, i: (0,) * len(shape))

    cos_t, sin_t = rope_tab
    n_cols = _att_stack_cols(tq)
    sink_rows = jnp.repeat(prm["sinks"].reshape(N_KV_HEADS, N_Q_HEADS // N_KV_HEADS), tq, axis=1)
    sink_rows = jnp.pad(sink_rows, ((0, 0), (0, n_cols - sink_rows.shape[1])))
    in_specs = [blk(ATT_WIDTH, COL_Q), blk(KV_WIDTH, COL_K), blk(KV_WIDTH, COL_V), blk(ATT_WIDTH, COL_GATT),
                blk(SC_WIDTH, COL_BSC), blk(SC_WIDTH, COL_CSC), blk(SC_WIDTH, COL_HSC), blk(SC_WIDTH, COL_GSC),
                pl.BlockSpec((tq, LANES), lambda b, i: (i, 0)), pl.BlockSpec((tq, LANES), lambda b, i: (i, 0)),
                const((N_KV_HEADS, n_cols)), const((SC_CONV, SC_WIDTH))]
    args = [src] * 8 + [cos_t, sin_t, sink_rows, prm["sc_conv_w"]]
    scratch = [pltpu.VMEM((2, N_KV_HEADS, WINDOW, LANES), BF16),
               pltpu.VMEM((2, N_KV_HEADS // 2, LANES, WINDOW), BF16)]
    kv_spec = pl.BlockSpec((None, None, WINDOW, KV_WIDTH), lambda b, i: (layer, b, 0, 0))
    if not carry:
        in_specs += [pl.BlockSpec((None, SUBLANES, SC_WIDTH), lambda b, i: (b, 0, 0)), kv_spec, kv_spec]
        args += [sc_state_pad, k_all_prev, v_all_prev]
    scratch += [pltpu.VMEM((SUBLANES + tq, SC_WIDTH), F32)]
    if not carry:
        scratch += [pltpu.VMEM((tq, ATT_WIDTH), F32), pltpu.VMEM((tq, KV_WIDTH), F32),
                    pltpu.VMEM((WINDOW, KV_WIDTH), F32), pltpu.VMEM((WINDOW, KV_WIDTH), F32)]
    aliases = {len(args): 0}
    in_specs += [pl.BlockSpec(memory_space=pl.ANY)]
    args += [y_mix]
    if kv_out_prev is not None:
        for out_idx, buf in zip((1, 2), kv_out_prev):
            aliases[len(args)] = out_idx
            in_specs += [pl.BlockSpec(memory_space=pl.ANY)]
            args += [buf]
    y, ko, vo, sco = pl.pallas_call(
        functools.partial(_att_kernel, L=L, nb=nb, carry=carry, n_alias=len(aliases)),
        grid=(batch, nb),
        in_specs=in_specs,
        out_specs=[y_spec, kv_spec, kv_spec,
                   pl.BlockSpec((None, SUBLANES, SC_WIDTH), lambda b, i: (b, 0, 0))],
        out_shape=[jax.ShapeDtypeStruct(y_mix.shape, y_mix.dtype),
                   jax.ShapeDtypeStruct((depth, batch, WINDOW, KV_WIDTH), F32),
                   jax.ShapeDtypeStruct((depth, batch, WINDOW, KV_WIDTH), F32),
                   jax.ShapeDtypeStruct((batch, SUBLANES, SC_WIDTH), F32)],
        scratch_shapes=scratch,
        input_output_aliases=aliases,
        compiler_params=_cparams(2),
        name="att_prompt" if carry else "att_decode",
    )(*args)
    base = SUBLANES - (SC_CONV - 1)
    off = base + L - (base + L) // SUBLANES * SUBLANES
    return y, (ko, vo), sco[:, off:off + SC_CONV - 1, :]


OUT_TM = 1024
OUT_KT = 1024
OUT_NT = 512
N_KSTEPS = D_MIX // OUT_KT
N_NSTEPS = D_MODEL // OUT_NT


def _outproj_kernel(y_ref, wo_ref, x_ref, npost_ref, p_ref, pp_ref, gw_ref, o_ref, acc_ref, xnb_ref):
    k = pl.program_id(1)

    @pl.when(k < N_KSTEPS)
    def _():
        y = y_ref[...].astype(BF16)
        for j in range(N_NSTEPS):
            part = jnp.dot(y, wo_ref[:, j * OUT_NT:(j + 1) * OUT_NT], preferred_element_type=F32)

            @pl.when(k == 0)
            def _():
                acc_ref[j] = part

            @pl.when(k > 0)
            def _():
                acc_ref[j] += part

    @pl.when(k == N_KSTEPS - 1)
    def _():
        ssq = None
        for j in range(N_NSTEPS):
            mix = acc_ref[j]
            s = jnp.sum(mix * mix, axis=-1, keepdims=True)
            ssq = s if ssq is None else ssq + s
        scale = lax.rsqrt(ssq * (1.0 / D_MODEL) + EPS)
        for j in range(N_NSTEPS):
            cols = slice(j * OUT_NT, (j + 1) * OUT_NT)
            xn = x_ref[:, cols] + (acc_ref[j] * scale) * npost_ref[:, cols]
            acc_ref[j] = xn
            xnb_ref[:, cols] = xn.astype(BF16)

    @pl.when(k >= N_KSTEPS)
    def _():
        j = k - N_KSTEPS
        gate = _sigmoid(jnp.dot(xnb_ref[...], gw_ref[...], preferred_element_type=F32))
        e = jnp.dot(p_ref[...].astype(BF16), pp_ref[...], preferred_element_type=F32)
        o_ref[...] = acc_ref[j] + gate * e


def _out_proj(y_mix, x2d, p_all, prm, *, layer):
    t = x2d.shape[0]
    tm = min(OUT_TM, t)

    def nstep(k):
        return jnp.maximum(k - N_KSTEPS, 0)

    return pl.pallas_call(
        _outproj_kernel,
        grid=(t // tm, N_KSTEPS + N_NSTEPS),
        in_specs=[
            pl.BlockSpec((tm, OUT_KT), lambda i, k: (i, jnp.minimum(k, N_KSTEPS - 1))),
            pl.BlockSpec((OUT_KT, D_MODEL), lambda i, k: (jnp.minimum(k, N_KSTEPS - 1), 0)),
            pl.BlockSpec((tm, D_MODEL), lambda i, k: (i, 0)),
            pl.BlockSpec((1, D_MODEL), lambda i, k: (0, 0)),
            pl.BlockSpec((None, tm, PLE_DIM), lambda i, k: (layer, i, 0)),
            pl.BlockSpec((PLE_DIM, OUT_NT), lambda i, k: (0, nstep(k))),
            pl.BlockSpec((D_MODEL, OUT_NT), lambda i, k: (0, nstep(k))),
        ],
        out_specs=pl.BlockSpec((tm, OUT_NT), lambda i, k: (i, nstep(k))),
        out_shape=jax.ShapeDtypeStruct((t, D_MODEL), F32),
        scratch_shapes=[pltpu.VMEM((N_NSTEPS, tm, OUT_NT), F32),
                        pltpu.VMEM((tm, D_MODEL), BF16)],
        compiler_params=_cparams(2),
        name="out_proj",
    )(y_mix.reshape(t, D_MIX), prm["w_out"], x2d, prm["norm_post"], p_all, prm["ple_proj"], prm["ple_gate"])


_W_IN_SRC = {}
_o = 0
for _name, _size in (("z", SSD_WIDTH), ("xbc", SSD_CONV_CH), ("dt", SSD_HEADS), ("q", ATT_WIDTH), ("k", KV_WIDTH),
                     ("v", KV_WIDTH), ("g_att", ATT_WIDTH), ("b_sc", SC_WIDTH), ("c_sc", SC_WIDTH),
                     ("h_sc", SC_WIDTH), ("g_sc", SC_WIDTH)):
    _W_IN_SRC[_name] = (_o, _size)
    _o += _size
N_IN = _o
_W_IN_DST = {"z": COL_Z, "xbc": COL_X, "q": COL_Q, "g_att": COL_GATT, "b_sc": COL_BSC, "c_sc": COL_CSC,
             "h_sc": COL_HSC, "g_sc": COL_GSC, "k": COL_K, "v": COL_V}
REGROUP_ROWS = 256


def _regroup_kernel(w_ref, o_ref):
    lane = lax.broadcasted_iota(jnp.int32, (REGROUP_ROWS, LANES), 1)
    for name, dst in _W_IN_DST.items():
        src, width = _W_IN_SRC[name]
        shift = src % LANES
        if shift == 0:
            o_ref[:, dst:dst + width] = w_ref[:, src:src + width].astype(BF16)
            continue
        base = src - shift
        prev = pltpu.roll(w_ref[:, base:base + LANES], LANES - shift, 1)
        for t in range(width // LANES):
            nxt = pltpu.roll(w_ref[:, base + (t + 1) * LANES:base + (t + 2) * LANES], LANES - shift, 1)
            tile = jnp.where(lane < LANES - shift, prev, nxt)
            o_ref[:, dst + t * LANES:dst + (t + 1) * LANES] = tile.astype(BF16)
            prev = nxt
    src, width = _W_IN_SRC["dt"]
    o_ref[:, COL_DT:COL_DT + LANES] = jnp.where(lane < width, w_ref[:, src:src + LANES], 0.0).astype(BF16)
    o_ref[:, COL_DT + LANES:N_PROJ] = jnp.zeros((REGROUP_ROWS, N_PROJ - COL_DT - LANES), BF16)


def _regroup_w_in(w_all, layer):
    return pl.pallas_call(
        _regroup_kernel,
        grid=(D_MODEL // REGROUP_ROWS,),
        in_specs=[pl.BlockSpec((None, REGROUP_ROWS, -(-N_IN // LANES) * LANES), lambda r: (layer, r, 0))],
        out_specs=pl.BlockSpec((REGROUP_ROWS, N_PROJ), lambda r: (r, 0)),
        out_shape=jax.ShapeDtypeStruct((D_MODEL, N_PROJ), BF16),
        compiler_params=_cparams(1),
        name="regroup_w_in",
    )(w_all)


def _lane_pad(v):
    return jnp.pad(v, (0, LANES - v.shape[0])).reshape(1, LANES)


def _layer_params(i, w_in, w_out, norm_pre, norm_post, ssd_conv_w, ssd_conv_b, ssd_a_log, ssd_dt_bias, ssd_d,
                  ssd_norm, attn_sinks, sc_conv_w, ple_proj, ple_gate, expand):
    return {
        "w_in": _regroup_w_in(w_in, i),
        "w_out": w_out[i].astype(BF16),
        "norm_pre": norm_pre[i].reshape(1, D_MODEL),
        "norm_post": norm_post[i].reshape(1, D_MODEL),
        "ssd_conv_w": ssd_conv_w[i],
        "ssd_conv_b": ssd_conv_b[i].reshape(1, SSD_CONV_CH),
        "a_log": _lane_pad(ssd_a_log[i]),
        "dt_bias": _lane_pad(ssd_dt_bias[i]),
        "d_skip": jnp.repeat(ssd_d[i], SSD_HEAD_DIM).reshape(1, SSD_WIDTH),
        "ssd_norm": ssd_norm[i].reshape(1, SSD_WIDTH),
        "sinks": attn_sinks[i],
        "sc_conv_w": sc_conv_w[i],
        "ple_proj": ple_proj[i].astype(BF16),
        "ple_gate": ple_gate[i].astype(BF16),
        "expand": expand,
    }


def _rope_tables(pos0, n):
    half = HEAD_DIM // 2
    inv_freq = jnp.exp(-math.log(ROPE_THETA) * jnp.arange(half, dtype=F32) * (2.0 / HEAD_DIM))
    pos = pos0 + jnp.arange(n, dtype=jnp.int32)
    ang = pos.astype(F32)[:, None] * inv_freq[None, :]
    cos, sin = jnp.cos(ang), jnp.sin(ang)
    reps = LANES // HEAD_DIM
    return (jnp.tile(jnp.concatenate([cos, cos], axis=1), (1, reps)),
            jnp.tile(jnp.concatenate([-sin, sin], axis=1), (1, reps)))


def _layer(x2d, p_all, prm, rope_tab, carried, *, layer, depth, batch, seqlen, conv_state, ssm_all, k_all, v_all,
           sc_state):
    decode = conv_state is not None
    ssm_out_prev, kv_out_prev = carried
    proj = _in_proj(x2d, prm["norm_pre"], prm["w_in"])
    if decode:
        conv_pad = jnp.pad(conv_state, ((0, 0), (SUBLANES - (SSD_CONV - 1), 0), (0, 0)))
        sc_pad = jnp.pad(sc_state, ((0, 0), (SUBLANES - (SC_CONV - 1), 0), (0, 0)))
    else:
        conv_pad = sc_pad = None
    y_mix, conv_new, ssm_out = _ssd_mixer(proj, conv_pad, ssm_all, ssm_out_prev, prm, layer=layer, depth=depth,
                                          batch=batch, seqlen=seqlen, y_dtype=F32 if decode else BF16)
    y_mix, kv_out, sc_new = _att_mixer(proj, y_mix, sc_pad, k_all, v_all, kv_out_prev, prm, rope_tab, layer=layer,
                                       depth=depth, batch=batch, seqlen=seqlen)
    x_new = _out_proj(y_mix, x2d, p_all, prm, layer=layer)
    return x_new, conv_new, sc_new, (ssm_out, kv_out)


def kernel(x_prompt, x_sample, p_prompt, p_sample, state_ssd_conv, state_ssm, cache_k, cache_v, state_sc_conv,
           w_in, w_out, norm_pre, norm_post, ssd_conv_w, ssd_conv_b, ssd_a_log, ssd_dt_bias, ssd_d, ssd_norm,
           attn_sinks, sc_conv_w, ple_proj, ple_gate):
    bp, lp, _ = x_prompt.shape
    bs, ls, _ = x_sample.shape
    depth = w_in.shape[0]
    assert lp % CHUNK == 0 and ls <= SUBLANES and cache_k.shape[2] == WINDOW
    head = jnp.arange(LANES, dtype=jnp.int32)[:, None]
    chan = jnp.arange(SSD_WIDTH, dtype=jnp.int32)[None, :]
    expand = (chan // SSD_HEAD_DIM == head).astype(BF16)
    rope_p = _rope_tables(0, lp)
    rope_s = _rope_tables(PAST_LEN, SUBLANES)
    yp = x_prompt.reshape(bp * lp, D_MODEL)
    ys = x_sample.reshape(bs * ls, D_MODEL)
    pp_all = p_prompt.reshape(depth, bp * lp, PLE_DIM)
    ps_all = p_sample.reshape(depth, bs * ls, PLE_DIM)
    ssm_all = state_ssm.reshape(depth, bs, SSD_WIDTH, SSD_STATE)
    k_all = cache_k.reshape(depth, bs, WINDOW, KV_WIDTH)
    v_all = cache_v.reshape(depth, bs, WINDOW, KV_WIDTH)
    carried_p = carried_s = (None, None)
    conv_p, sc_p, conv_s, sc_s = [], [], [], []
    for i in range(depth):
        prm = _layer_params(i, w_in, w_out, norm_pre, norm_post, ssd_conv_w, ssd_conv_b, ssd_a_log, ssd_dt_bias,
                            ssd_d, ssd_norm, attn_sinks, sc_conv_w, ple_proj, ple_gate, expand)
        yp, cv, sc, carried_p = _layer(yp, pp_all, prm, rope_p, carried_p, layer=i, depth=depth, batch=bp, seqlen=lp,
                                       conv_state=None, ssm_all=None, k_all=None, v_all=None, sc_state=None)
        conv_p.append(cv)
        sc_p.append(sc)
        ys, cv, sc, carried_s = _layer(ys, ps_all, prm, rope_s, carried_s, layer=i, depth=depth, batch=bs, seqlen=ls,
                                       conv_state=state_ssd_conv[i], ssm_all=ssm_all, k_all=k_all, v_all=v_all,
                                       sc_state=state_sc_conv[i])
        conv_s.append(cv)
        sc_s.append(sc)

    def states(conv, carried, sc, batch):
        ssm_out, (k_out, v_out) = carried
        return (jnp.stack(conv), ssm_out.reshape(depth, batch, SSD_HEADS, SSD_HEAD_DIM, SSD_STATE),
                k_out.reshape(depth, batch, WINDOW, N_KV_HEADS, HEAD_DIM),
                v_out.reshape(depth, batch, WINDOW, N_KV_HEADS, HEAD_DIM), jnp.stack(sc))

    return (yp.reshape(bp, lp, D_MODEL), ys.reshape(bs, ls, D_MODEL),
            *states(conv_p, carried_p, sc_p, bp), *states(conv_s, carried_s, sc_s, bs))
```

```python
import functools
import math

import jax
import jax.numpy as jnp
from jax import lax
from jax.experimental import pallas as pl
from jax.experimental.pallas import tpu as pltpu

F32 = jnp.float32
BF16 = jnp.bfloat16

D_MODEL = 2048
D_MIX = 2 * D_MODEL
SSD_WIDTH = D_MIX // 2
SSD_HEAD_DIM = 64
SSD_HEADS = SSD_WIDTH // SSD_HEAD_DIM
SSD_GROUPS = 4
SSD_STATE = 128
SSD_CONV = 4
SSD_CONV_CH = SSD_WIDTH + 2 * SSD_GROUPS * SSD_STATE
ATT_WIDTH = D_MIX // 4
HEAD_DIM = 64
N_Q_HEADS = ATT_WIDTH // HEAD_DIM
N_KV_HEADS = 4
KV_WIDTH = N_KV_HEADS * HEAD_DIM
WINDOW = 128
ROPE_THETA = 10000.0
SC_WIDTH = D_MIX // 4
SC_CONV = 3
PLE_DIM = 256
EPS = 1e-6
PAST_LEN = 8192

LANES = 128
SUBLANES = 8
CHUNK = 128
GROUP_W = SSD_WIDTH // SSD_GROUPS
HEADS_PER_GROUP = SSD_HEADS // SSD_GROUPS

COL_Z = 0
COL_X = 2048
COL_B = 4096
COL_C = 4608
COL_Q = 5120
COL_GATT = 6144
COL_BSC = 7168
COL_CSC = 8192
COL_HSC = 9216
COL_GSC = 10240
COL_K = 11264
COL_V = 11520
COL_DT = 11776
N_PROJ = 12288

VMEM_LIMIT = 56 * 1024 * 1024


def _cparams(ndims):
    return pltpu.CompilerParams(dimension_semantics=("arbitrary",) * ndims, vmem_limit_bytes=VMEM_LIMIT)


def _silu(x):
    h = 0.5 * x
    return h + h * jnp.tanh(h)


def _sigmoid(x):
    return 0.5 + 0.5 * jnp.tanh(0.5 * x)


def _inproj_kernel(x_ref, nw_ref, w_ref, o_ref, h_ref, *, tm, row_chunk):
    @pl.when(pl.program_id(1) == 0)
    def _():
        def body(r, carry):
            rows = pl.ds(pl.multiple_of(r * row_chunk, row_chunk), row_chunk)
            x = x_ref[rows, :]
            ms = jnp.mean(x * x, axis=-1, keepdims=True)
            h_ref[rows, :] = ((x * lax.rsqrt(ms + EPS)) * nw_ref[...]).astype(BF16)
            return carry

        lax.fori_loop(0, tm // row_chunk, body, 0)

    o_ref[...] = jnp.dot(h_ref[...], w_ref[...], preferred_element_type=F32)


def _in_proj(x2d, nw, w_bf16):
    t = x2d.shape[0]
    tm = min(1024, t)
    tn = 1024
    return pl.pallas_call(
        functools.partial(_inproj_kernel, tm=tm, row_chunk=128),
        grid=(t // tm, N_PROJ // tn),
        in_specs=[
            pl.BlockSpec((tm, D_MODEL), lambda i, j: (i, 0)),
            pl.BlockSpec((1, D_MODEL), lambda i, j: (0, 0)),
            pl.BlockSpec((D_MODEL, tn), lambda i, j: (0, j)),
        ],
        out_specs=pl.BlockSpec((tm, tn), lambda i, j: (i, j)),
        out_shape=jax.ShapeDtypeStruct((t, N_PROJ), F32),
        scratch_shapes=[pltpu.VMEM((tm, D_MODEL), BF16)],
        compiler_params=_cparams(2),
        name="in_proj",
    )(x2d, nw, w_bf16)


DEC_G = 4
DEC_TILE = SUBLANES


def _softplus(x):
    return jnp.maximum(x, 0.0) + jnp.log1p(jnp.exp(-jnp.abs(x)))


def _ssd_conv_silu(xpad, cw_ref, cbias_ref, xs_scr, bm_scr, cm_scr):
    cwid = 256
    base = SUBLANES - (SSD_CONV - 1)
    for cb in range(SSD_CONV_CH // cwid):
        cols = slice(cb * cwid, (cb + 1) * cwid)
        acc = xpad[base:base + CHUNK, cols] * cw_ref[0:1, cols]
        for j in range(1, SSD_CONV):
            acc = acc + xpad[base + j:base + j + CHUNK, cols] * cw_ref[j:j + 1, cols]
        act = _silu(acc + cbias_ref[:, cols])
        lo = cb * cwid
        if lo < SSD_WIDTH:
            xs_scr[:, lo:lo + cwid] = act
        elif lo < SSD_WIDTH + GROUP_W:
            bm_scr[:, lo - SSD_WIDTH:lo - SSD_WIDTH + cwid] = act.astype(BF16)
        else:
            o = lo - SSD_WIDTH - GROUP_W
            cm_scr[:, o:o + cwid] = act.astype(BF16)


def _ssd_expand(v, e_mat):
    hi = v.astype(BF16)
    lo = (v - hi.astype(F32)).astype(BF16)
    return jnp.dot(hi, e_mat, preferred_element_type=F32) + jnp.dot(lo, e_mat, preferred_element_type=F32)


def _ssd_diag(a_cum, dt, allowed, xs_scr, bm_scr, cm_scr, y_scr):
    Q = CHUNK
    a_cum_t = a_cum.T
    dt_t = dt.T
    lane = lax.broadcasted_iota(jnp.int32, (Q, LANES), 1)
    low_half = lane < SSD_HEAD_DIM
    for g in range(SSD_GROUPS):
        gcols = slice(g * SSD_STATE, (g + 1) * SSD_STATE)
        cbm = lax.dot_general(cm_scr[:, gcols], bm_scr[:, gcols], (((1,), (1,)), ((), ())),
                              preferred_element_type=F32)
        for p in range(HEADS_PER_GROUP // 2):
            pair = g * (HEADS_PER_GROUP // 2) + p
            ms = []
            for h in (2 * pair, 2 * pair + 1):
                seg = jnp.broadcast_to(a_cum[:, h:h + 1], (Q, Q)) - a_cum_t[h:h + 1, :]
                dec = jnp.exp(jnp.where(allowed, seg, -jnp.inf))
                ms.append((cbm * dec * dt_t[h:h + 1, :]).astype(BF16))
            lhs = jnp.concatenate(ms, axis=1)
            xp = xs_scr[:, pair * LANES:(pair + 1) * LANES]
            rhs = jnp.concatenate([jnp.where(low_half, xp, 0.0).astype(BF16),
                                   jnp.where(low_half, 0.0, xp).astype(BF16)], axis=0)
            y_scr[:, pair * LANES:(pair + 1) * LANES] = jnp.dot(lhs, rhs, preferred_element_type=F32)


def _ssd_gate_norm(yg, zg, nw):
    yg = yg * _silu(zg)
    msq = jnp.mean(yg * yg, axis=-1, keepdims=True)
    return (yg * lax.rsqrt(msq + EPS)) * nw


def _ssd_decode_kernel(*refs, L, n_alias):
    Q = CHUNK
    n_in = 14
    refs = refs[:n_in] + refs[n_in + n_alias:]
    (z_ref, x_ref, b_ref, c_ref, dt_ref, cs_ref, ss_ref, cw_ref, cbias_ref, alog_ref, dtb_ref, dsk_ref,
     nw_ref, e_ref, y_ref, cso_ref, sso_ref,
     xpad, xs_scr, bm_scr, cm_scr, y_scr, ea_scr, we_scr, zp_scr, dtp_scr) = refs
    t0 = DEC_TILE - L
    used = DEC_G * DEC_TILE

    xpad[...] = jnp.zeros(xpad.shape, F32)
    for g in range(DEC_G):
        r0 = SUBLANES + DEC_TILE * g
        tok = slice(L * g, L * (g + 1))
        xpad[r0:r0 + DEC_TILE, :] = cs_ref[g]
        xpad[r0 + t0:r0 + DEC_TILE, 0:SSD_WIDTH] = x_ref[tok, :]
        xpad[r0 + t0:r0 + DEC_TILE, SSD_WIDTH:SSD_WIDTH + GROUP_W] = b_ref[tok, :]
        xpad[r0 + t0:r0 + DEC_TILE, SSD_WIDTH + GROUP_W:SSD_CONV_CH] = c_ref[tok, :]
    for g in range(DEC_G):
        r0 = SUBLANES + DEC_TILE * g
        cso_ref[g] = xpad[r0:r0 + DEC_TILE, :]
    _ssd_conv_silu(xpad, cw_ref, cbias_ref, xs_scr, bm_scr, cm_scr)

    row = lax.broadcasted_iota(jnp.int32, (Q, LANES), 0)
    in_tile = row & (DEC_TILE - 1)
    dtp_scr[...] = jnp.zeros((Q, LANES), F32)
    zp_scr[...] = jnp.zeros(zp_scr.shape, F32)
    for g in range(DEC_G):
        dtp_scr[DEC_TILE * g + t0:DEC_TILE * (g + 1), :] = dt_ref[L * g:L * (g + 1), :]
        zp_scr[DEC_TILE * g + t0:DEC_TILE * (g + 1), :] = z_ref[L * g:L * (g + 1), :]
    dt = jnp.where((in_tile >= t0) & (row < used), _softplus(dtp_scr[...] + dtb_ref[...]), 0.0)
    a_cum = dt * (-jnp.exp(alog_ref[...]))
    k = 1
    while k < DEC_TILE:
        a_cum = a_cum + jnp.where(in_tile >= k, pltpu.roll(a_cum, k, 0), 0.0)
        k *= 2
    tot = jnp.where(in_tile == DEC_TILE - 1, a_cum, 0.0)
    k = 1
    while k < DEC_TILE:
        tot = tot + pltpu.roll(tot, Q - k, 0)
        k *= 2
    e_mat = e_ref[...]
    ea_scr[...] = _ssd_expand(jnp.exp(a_cum), e_mat)
    we_scr[...] = _ssd_expand(jnp.exp(tot - a_cum) * dt, e_mat)
    cd_t = jnp.exp(tot).T

    lrow = lax.broadcasted_iota(jnp.int32, (Q, Q), 0)
    scol = lax.broadcasted_iota(jnp.int32, (Q, Q), 1)
    allowed = (lrow >= scol) & ((lrow // DEC_TILE) == (scol // DEC_TILE))
    _ssd_diag(a_cum, dt, allowed, xs_scr, bm_scr, cm_scr, y_scr)

    pair_rows = 2 * DEC_TILE
    first_of_pair = lax.broadcasted_iota(jnp.int32, (pair_rows, GROUP_W), 0) < DEC_TILE
    for g in range(SSD_GROUPS):
        gcols = slice(g * SSD_STATE, (g + 1) * SSD_STATE)
        wcols = slice(g * GROUP_W, (g + 1) * GROUP_W)
        for sp in range(DEC_G // 2):
            rows = slice(sp * pair_rows, (sp + 1) * pair_rows)
            offs = [lax.dot_general(cm_scr[rows, gcols], ss_ref[2 * sp + j, wcols, :].astype(BF16),
                                    (((1,), (1,)), ((), ())), preferred_element_type=F32) for j in range(2)]
            y_off = jnp.where(first_of_pair, offs[0], offs[1])
            y_scr[rows, wcols] = y_scr[rows, wcols] + y_off * ea_scr[rows, wcols]
        yg = y_scr[0:used, wcols] + dsk_ref[:, wcols] * xs_scr[0:used, wcols]
        out = _ssd_gate_norm(yg, zp_scr[:, wcols], nw_ref[:, wcols])
        for s in range(DEC_G):
            y_ref[L * s:L * (s + 1), wcols] = out[DEC_TILE * s + t0:DEC_TILE * (s + 1)].astype(y_ref.dtype)

    row_seq = lax.broadcasted_iota(jnp.int32, (Q, SSD_STATE), 0) // DEC_TILE
    for g in range(SSD_GROUPS):
        wcols = slice(g * GROUP_W, (g + 1) * GROUP_W)
        gcols = slice(g * SSD_STATE, (g + 1) * SSD_STATE)
        xw_t = (xs_scr[:, wcols] * we_scr[:, wcols]).T.astype(BF16)
        bm_g = bm_scr[:, gcols]
        for s in range(DEC_G):
            st = jnp.dot(xw_t, jnp.where(row_seq == s, bm_g, jnp.zeros_like(bm_g)),
                         preferred_element_type=F32)
            last = DEC_TILE * (s + 1) - 1
            cd_rows = jnp.broadcast_to(cd_t[:, last:last + 1], (LANES, LANES))
            for hh in range(HEADS_PER_GROUP):
                h = g * HEADS_PER_GROUP + hh
                r0 = h * SSD_HEAD_DIM
                sso_ref[s, r0:r0 + SSD_HEAD_DIM, :] = (ss_ref[s, r0:r0 + SSD_HEAD_DIM, :] * cd_rows[h:h + 1, :]
                                                       + st[hh * SSD_HEAD_DIM:(hh + 1) * SSD_HEAD_DIM, :])


def _ssd_kernel(*refs, nc, n_alias):
    Q = CHUNK
    n_in = 12
    refs = refs[:n_in] + refs[n_in + n_alias:]
    (z_ref, x_ref, b_ref, c_ref, dt_ref, cw_ref, cbias_ref, alog_ref, dtb_ref, dsk_ref, nw_ref, e_ref,
     y_ref, cso_ref, sso_ref,
     xpad, xs_scr, bm_scr, cm_scr, y_scr, ea_scr, we_scr) = refs
    c = pl.program_id(1)

    @pl.when(c == 0)
    def _init():
        sso_ref[...] = jnp.zeros(sso_ref.shape, F32)
        xpad[0:SUBLANES, :] = jnp.zeros((SUBLANES, SSD_CONV_CH), F32)

    xpad[SUBLANES:SUBLANES + Q, 0:SSD_WIDTH] = x_ref[...]
    xpad[SUBLANES:SUBLANES + Q, SSD_WIDTH:SSD_WIDTH + GROUP_W] = b_ref[...]
    xpad[SUBLANES:SUBLANES + Q, SSD_WIDTH + GROUP_W:SSD_CONV_CH] = c_ref[...]
    _ssd_conv_silu(xpad, cw_ref, cbias_ref, xs_scr, bm_scr, cm_scr)

    tail_tile = xpad[Q:Q + SUBLANES, :]

    @pl.when(c == nc - 1)
    def _():
        cso_ref[...] = tail_tile

    xpad[0:SUBLANES, :] = tail_tile

    row = lax.broadcasted_iota(jnp.int32, (Q, LANES), 0)
    dt = _softplus(dt_ref[...] + dtb_ref[...])
    a_cum = dt * (-jnp.exp(alog_ref[...]))
    k = 1
    while k < Q:
        a_cum = a_cum + jnp.where(row >= k, pltpu.roll(a_cum, k, 0), 0.0)
        k *= 2
    a_last = a_cum[Q - 1:Q, :]
    e_mat = e_ref[...]
    ea_scr[...] = _ssd_expand(jnp.exp(a_cum), e_mat)
    we_scr[...] = _ssd_expand(jnp.exp(a_last - a_cum) * dt, e_mat)
    cd_rows = jnp.broadcast_to(jnp.exp(a_cum.T[:, Q - 1:Q]), (LANES, LANES))

    causal = lax.broadcasted_iota(jnp.int32, (Q, Q), 0) >= lax.broadcasted_iota(jnp.int32, (Q, Q), 1)
    _ssd_diag(a_cum, dt, causal, xs_scr, bm_scr, cm_scr, y_scr)

    for g in range(SSD_GROUPS):
        gcols = slice(g * SSD_STATE, (g + 1) * SSD_STATE)
        wcols = slice(g * GROUP_W, (g + 1) * GROUP_W)
        s_g = sso_ref[wcols, :].astype(BF16)
        y_off = lax.dot_general(cm_scr[:, gcols], s_g, (((1,), (1,)), ((), ())), preferred_element_type=F32)
        yg = y_scr[:, wcols] + y_off * ea_scr[:, wcols] + dsk_ref[:, wcols] * xs_scr[:, wcols]
        y_ref[:, wcols] = _ssd_gate_norm(yg, z_ref[:, wcols], nw_ref[:, wcols]).astype(y_ref.dtype)

    for g in range(SSD_GROUPS):
        wcols = slice(g * GROUP_W, (g + 1) * GROUP_W)
        gcols = slice(g * SSD_STATE, (g + 1) * SSD_STATE)
        xw_t = (xs_scr[:, wcols] * we_scr[:, wcols]).T.astype(BF16)
        st = jnp.dot(xw_t, bm_scr[:, gcols], preferred_element_type=F32)
        for hh in range(HEADS_PER_GROUP):
            h = g * HEADS_PER_GROUP + hh
            r0 = h * SSD_HEAD_DIM
            sso_ref[r0:r0 + SSD_HEAD_DIM, :] = (sso_ref[r0:r0 + SSD_HEAD_DIM, :] * cd_rows[h:h + 1, :]
                                                + st[hh * SSD_HEAD_DIM:(hh + 1) * SSD_HEAD_DIM, :])


def _ssd_specs_common(prm):
    shapes = [(SSD_CONV, SSD_CONV_CH), (1, SSD_CONV_CH), (1, LANES), (1, LANES), (1, SSD_WIDTH), (1, SSD_WIDTH),
              (LANES, SSD_WIDTH)]
    args = [prm["ssd_conv_w"], prm["ssd_conv_b"], prm["a_log"], prm["dt_bias"], prm["d_skip"], prm["ssd_norm"],
            prm["expand"]]
    scratch = [pltpu.VMEM((SUBLANES + CHUNK, SSD_CONV_CH), F32),
               pltpu.VMEM((CHUNK, SSD_WIDTH), F32),
               pltpu.VMEM((CHUNK, GROUP_W), BF16),
               pltpu.VMEM((CHUNK, GROUP_W), BF16),
               pltpu.VMEM((CHUNK, SSD_WIDTH), F32),
               pltpu.VMEM((CHUNK, SSD_WIDTH), F32),
               pltpu.VMEM((CHUNK, SSD_WIDTH), F32)]
    return shapes, args, scratch


def _ssd_mixer(proj, ssm_out_prev, prm, *, layer, depth, batch, seqlen):
    nc = seqlen // CHUNK

    def blk(width, col):
        return pl.BlockSpec((CHUNK, width), lambda b, c: (b * nc + c, col // width))

    def const(shape):
        return pl.BlockSpec(shape, lambda b, c: (0,) * len(shape))

    state_spec = pl.BlockSpec((None, None, SSD_WIDTH, SSD_STATE), lambda b, c: (layer, b, 0, 0))
    shapes, cargs, scratch = _ssd_specs_common(prm)
    in_specs = [blk(SSD_WIDTH, COL_Z), blk(SSD_WIDTH, COL_X), blk(GROUP_W, COL_B), blk(GROUP_W, COL_C),
                blk(LANES, COL_DT)] + [const(s) for s in shapes]
    args = [proj] * 5 + cargs
    aliases = {}
    if ssm_out_prev is not None:
        aliases[len(args)] = 2
        in_specs += [pl.BlockSpec(memory_space=pl.ANY)]
        args += [ssm_out_prev]
    y, cso, sso = pl.pallas_call(
        functools.partial(_ssd_kernel, nc=nc, n_alias=len(aliases)),
        grid=(batch, nc),
        in_specs=in_specs,
        out_specs=[pl.BlockSpec((CHUNK, SSD_WIDTH), lambda b, c: (b * nc + c, 0)),
                   pl.BlockSpec((None, SUBLANES, SSD_CONV_CH), lambda b, c: (b, 0, 0)),
                   state_spec],
        out_shape=[jax.ShapeDtypeStruct((batch * seqlen, D_MIX), BF16),
                   jax.ShapeDtypeStruct((batch, SUBLANES, SSD_CONV_CH), F32),
                   jax.ShapeDtypeStruct((depth, batch, SSD_WIDTH, SSD_STATE), F32)],
        scratch_shapes=scratch,
        input_output_aliases=aliases,
        compiler_params=_cparams(2),
        name="ssd_prompt",
    )(*args)
    return y, cso[:, SUBLANES - (SSD_CONV - 1):, :], sso


def _ssd_decode_mixer(proj, conv_state, ssm_all, ssm_out_prev, prm, *, layer, depth, batch, seqlen):
    t0 = DEC_TILE - seqlen
    hist = SSD_CONV - 1
    conv_pad = jnp.pad(conv_state, ((0, 0), (t0 - hist, DEC_TILE - t0), (0, 0)))
    rows = DEC_G * seqlen

    def blk(width, col):
        return pl.BlockSpec((rows, width), lambda s: (s, col // width))

    def const(shape):
        return pl.BlockSpec(shape, lambda s: (0,) * len(shape))

    state_spec = pl.BlockSpec((None, DEC_G, SSD_WIDTH, SSD_STATE), lambda s: (layer, s, 0, 0))
    shapes, cargs, scratch = _ssd_specs_common(prm)
    in_specs = [blk(SSD_WIDTH, COL_Z), blk(SSD_WIDTH, COL_X), blk(GROUP_W, COL_B), blk(GROUP_W, COL_C),
                blk(LANES, COL_DT),
                pl.BlockSpec((DEC_G, DEC_TILE, SSD_CONV_CH), lambda s: (s, 0, 0)), state_spec]
    in_specs += [const(s) for s in shapes]
    args = [proj] * 5 + [conv_pad, ssm_all] + cargs
    aliases = {}
    if ssm_out_prev is not None:
        aliases[len(args)] = 2
        in_specs += [pl.BlockSpec(memory_space=pl.ANY)]
        args += [ssm_out_prev]
    scratch += [pltpu.VMEM((DEC_G * DEC_TILE, SSD_WIDTH), F32), pltpu.VMEM((CHUNK, LANES), F32)]
    y, cso, sso = pl.pallas_call(
        functools.partial(_ssd_decode_kernel, L=seqlen, n_alias=len(aliases)),
        grid=(batch // DEC_G,),
        in_specs=in_specs,
        out_specs=[pl.BlockSpec((rows, SSD_WIDTH), lambda s: (s, 0)),
                   pl.BlockSpec((DEC_G, DEC_TILE, SSD_CONV_CH), lambda s: (s, 0, 0)),
                   state_spec],
        out_shape=[jax.ShapeDtypeStruct((batch * seqlen, D_MIX), F32),
                   jax.ShapeDtypeStruct((batch, DEC_TILE, SSD_CONV_CH), F32),
                   jax.ShapeDtypeStruct((depth, batch, SSD_WIDTH, SSD_STATE), F32)],
        scratch_shapes=scratch,
        input_output_aliases=aliases,
        compiler_params=_cparams(1),
        name="ssd_decode",
    )(*args)
    return y, cso[:, DEC_TILE - hist:, :], sso


def _att_stack_cols(tq):
    return -(-(N_Q_HEADS // N_KV_HEADS) * tq // LANES) * LANES


def _att_kernel(*refs, L, nb, carry, n_alias):
    TQ = CHUNK if L == CHUNK else SUBLANES
    W = WINDOW
    n_in = 12 if carry else 15
    refs = refs[:n_in] + refs[n_in + n_alias:]
    if carry:
        (q_ref, k_ref, v_ref, g_ref, sb_ref, sc_ref, sh_ref, sg_ref, cos_ref, sin_ref, sinks_ref, scw_ref,
         y_ref, ko_ref, vo_ref, sco_ref,
         kd, vt, xpad2) = refs
    else:
        (q_ref, k_ref, v_ref, g_ref, sb_ref, sc_ref, sh_ref, sg_ref, cos_ref, sin_ref, sinks_ref, scw_ref,
         sch_ref, kp_ref, vp_ref,
         y_ref, ko_ref, vo_ref, sco_ref,
         kd, vt, xpad2, pad_w, pad_kv, kc_scr, vc_scr) = refs
    i = pl.program_id(1)
    padded = L != TQ
    R = _att_stack_cols(TQ)

    def full(ref, scr):
        if not padded:
            return ref[...]
        scr[...] = jnp.zeros(scr.shape, F32)
        scr[0:L, :] = ref[...]
        return scr[...]

    if padded:
        vals = {name: full(ref, pad_w) for name, ref in
                (("q", q_ref), ("g", g_ref), ("sb", sb_ref), ("sc", sc_ref), ("sh", sh_ref), ("sg", sg_ref))}

        def get(name, cols=slice(None)):
            return vals[name][:, cols]
    else:
        srcs = {"q": q_ref, "g": g_ref, "sb": sb_ref, "sc": sc_ref, "sh": sh_ref, "sg": sg_ref}

        def get(name, cols=slice(None)):
            return srcs[name][:, cols]

    lane = lax.broadcasted_iota(jnp.int32, (TQ, LANES), 1)
    first_half = (lane % HEAD_DIM) < (HEAD_DIM // 2)
    cos = cos_ref[...]
    sin = sin_ref[...]

    def rope(x):
        sw = jnp.where(first_half, pltpu.roll(x, LANES - HEAD_DIM // 2, 1), pltpu.roll(x, HEAD_DIM // 2, 1))
        return x * cos + sw * sin

    k_new = full(k_ref, pad_kv if padded else None)
    k_rot = jnp.concatenate([rope(k_new[:, 0:LANES]), rope(k_new[:, LANES:2 * LANES])], axis=1)
    v_new = full(v_ref, pad_kv if padded else None)
    low_w = lax.broadcasted_iota(jnp.int32, (W, LANES), 1) < HEAD_DIM

    def stage_kv(slot, k_blk, v_blk):
        for kvp in range(N_KV_HEADS // 2):
            kpair = k_blk[:, kvp * LANES:(kvp + 1) * LANES]
            kroll = pltpu.roll(kpair, HEAD_DIM, 1)
            kd[slot, 2 * kvp] = jnp.where(low_w, kpair, kroll).astype(BF16)
            kd[slot, 2 * kvp + 1] = jnp.where(low_w, kroll, kpair).astype(BF16)
            vt[slot, kvp] = v_blk[:, kvp * LANES:(kvp + 1) * LANES].T.astype(BF16)

    rowk = lax.broadcasted_iota(jnp.int32, (W, KV_WIDTH), 0)
    if carry:
        cur_slot = lax.rem(i, 2)

        @pl.when(i == 0)
        def _():
            kd[1] = jnp.zeros((N_KV_HEADS, W, LANES), BF16)
            vt[1] = jnp.zeros((N_KV_HEADS // 2, LANES, W), BF16)

        stage_kv(cur_slot, k_rot, v_new)

        @pl.when(i == nb - 1)
        def _():
            ko_ref[...] = k_rot
            vo_ref[...] = v_new
    else:
        k_prev = kp_ref[...]
        v_prev = vp_ref[...]
        kc_scr[...] = jnp.zeros((W, KV_WIDTH), F32)
        kc_scr[0:TQ, :] = k_rot
        vc_scr[...] = jnp.zeros((W, KV_WIDTH), F32)
        vc_scr[0:TQ, :] = v_new
        k_cur = kc_scr[...]
        v_cur = vc_scr[...]
        stage_kv(0, k_prev, v_prev)
        stage_kv(1, k_cur, v_cur)
        ko_ref[...] = pltpu.roll(jnp.where(rowk < L, k_cur, k_prev), W - L, 0)
        vo_ref[...] = pltpu.roll(jnp.where(rowk < L, v_cur, v_prev), W - L, 0)

    s_idx = lax.broadcasted_iota(jnp.int32, (W, R), 0)
    t_idx = lax.broadcasted_iota(jnp.int32, (W, R), 1) & (TQ - 1)
    cur_mask = s_idx <= t_idx
    prev_mask = s_idx > t_idx
    if carry:
        prev_mask = prev_mask & (i > 0)
        slot0_cur = cur_slot == 0
        masks = ((cur_mask & slot0_cur) | (prev_mask & jnp.logical_not(slot0_cur)),
                 (prev_mask & slot0_cur) | (cur_mask & jnp.logical_not(slot0_cur)))
    else:
        masks = (prev_mask, cur_mask)
    low = lane < HEAD_DIM
    heads_per_kv = N_Q_HEADS // N_KV_HEADS

    for kvh in range(N_KV_HEADS):
        kvp, half = divmod(kvh, 2)
        qs = []
        for qq in range(heads_per_kv // 2):
            qp = kvh * (heads_per_kv // 2) + qq
            q_rot = rope(get("q", slice(qp * LANES, (qp + 1) * LANES))) * (HEAD_DIM ** -0.5)
            qs += [jnp.where(low, q_rot, 0.0), jnp.where(low, 0.0, q_rot)]
        if heads_per_kv * TQ < R:
            qs.append(jnp.zeros((R - heads_per_kv * TQ, LANES), F32))
        q_stack = jnp.concatenate(qs, axis=0).astype(BF16)
        s = [jnp.where(masks[slot],
                       lax.dot_general(kd[slot, kvh], q_stack, (((1,), (1,)), ((), ())),
                                       preferred_element_type=F32), -jnp.inf) for slot in range(2)]
        sink = sinks_ref[kvh:kvh + 1, :]
        m = jnp.maximum(jnp.maximum(jnp.max(s[0], axis=0, keepdims=True), jnp.max(s[1], axis=0, keepdims=True)),
                        sink)
        p_un = [jnp.exp(s[slot] - m) for slot in range(2)]
        den = (jnp.sum(p_un[0], axis=0, keepdims=True) + jnp.sum(p_un[1], axis=0, keepdims=True)
               + jnp.exp(sink - m))
        o_t = (jnp.dot(vt[0, kvp], p_un[0].astype(BF16), preferred_element_type=F32)
               + jnp.dot(vt[1, kvp], p_un[1].astype(BF16), preferred_element_type=F32)) * (1.0 / den)
        for qq in range(heads_per_kv // 2):
            qp = kvh * (heads_per_kv // 2) + qq
            qcols = slice(qp * LANES, (qp + 1) * LANES)
            if carry:
                rows = slice(half * HEAD_DIM, (half + 1) * HEAD_DIM)
                o = jnp.concatenate([o_t[rows, (2 * qq) * TQ:(2 * qq + 1) * TQ],
                                     o_t[rows, (2 * qq + 1) * TQ:(2 * qq + 2) * TQ]], axis=0).T
            else:
                if qq == 0:
                    o_all = o_t.T
                oa = o_all[(2 * qq) * TQ:(2 * qq + 1) * TQ, :]
                ob = o_all[(2 * qq + 1) * TQ:(2 * qq + 2) * TQ, :]
                if half == 0:
                    ob = pltpu.roll(ob, HEAD_DIM, 1)
                else:
                    oa = pltpu.roll(oa, HEAD_DIM, 1)
                o = jnp.where(low, oa, ob)
            y_ref[:, qcols] = (o * _silu(get("g", qcols)))[0:L].astype(y_ref.dtype)

    @pl.when(i == 0)
    def _():
        if carry:
            xpad2[0:SUBLANES, :] = jnp.zeros((SUBLANES, SC_WIDTH), F32)
        else:
            xpad2[0:SUBLANES, :] = sch_ref[...]

    u = get("sc") * get("sh")
    xpad2[SUBLANES:SUBLANES + TQ, :] = u
    base = SUBLANES - (SC_CONV - 1)
    conv = xpad2[base:base + TQ, :] * scw_ref[0:1, :]
    for j in range(1, SC_CONV):
        conv = conv + xpad2[base + j:base + j + TQ, :] * scw_ref[j:j + 1, :]
    ysc = (get("sb") * conv) * _silu(get("sg"))
    y_ref[:, ATT_WIDTH:ATT_WIDTH + SC_WIDTH] = ysc[0:L].astype(y_ref.dtype)

    tail = (base + L) // SUBLANES * SUBLANES
    tail_tile = xpad2[tail:tail + SUBLANES, :]

    @pl.when(i == nb - 1)
    def _():
        sco_ref[...] = tail_tile

    if carry:
        xpad2[0:SUBLANES, :] = tail_tile


def _att_mixer(proj, y_mix, sc_state_pad, k_all_prev, v_all_prev, kv_out_prev, prm, rope_tab, *, layer, depth,
               batch, seqlen):
    carry = sc_state_pad is None
    if seqlen >= CHUNK:
        L, nb = CHUNK, seqlen // CHUNK
        src = proj

        def blk(width, col):
            return pl.BlockSpec((L, width), lambda b, i: (b * nb + i, col // width))

        y_spec = pl.BlockSpec((L, ATT_WIDTH + SC_WIDTH), lambda b, i: (b * nb + i, 1))
        tq = CHUNK
    else:
        L, nb = seqlen, 1
        src = proj

        def blk(width, col):
            return pl.BlockSpec((None, L, width), lambda b, i: (b, 0, col // width))

        y_spec = pl.BlockSpec((None, L, ATT_WIDTH + SC_WIDTH), lambda b, i: (b, 0, 1))
        tq = SUBLANES

    def const(shape):
        return pl.BlockSpec(shape, lambda b, i: (0,) * len(shape))

    cos_t, sin_t = rope_tab
    n_cols = _att_stack_cols(tq)
    sink_rows = jnp.repeat(prm["sinks"].reshape(N_KV_HEADS, N_Q_HEADS // N_KV_HEADS), tq, axis=1)
    sink_rows = jnp.pad(sink_rows, ((0, 0), (0, n_cols - sink_rows.shape[1])))
    in_specs = [blk(ATT_WIDTH, COL_Q), blk(KV_WIDTH, COL_K), blk(KV_WIDTH, COL_V), blk(ATT_WIDTH, COL_GATT),
                blk(SC_WIDTH, COL_BSC), blk(SC_WIDTH, COL_CSC), blk(SC_WIDTH, COL_HSC), blk(SC_WIDTH, COL_GSC),
                pl.BlockSpec((tq, LANES), lambda b, i: (i, 0)), pl.BlockSpec((tq, LANES), lambda b, i: (i, 0)),
                const((N_KV_HEADS, n_cols)), const((SC_CONV, SC_WIDTH))]
    args = [src] * 8 + [cos_t, sin_t, sink_rows, prm["sc_conv_w"]]
    scratch = [pltpu.VMEM((2, N_KV_HEADS, WINDOW, LANES), BF16),
               pltpu.VMEM((2, N_KV_HEADS // 2, LANES, WINDOW), BF16)]
    kv_spec = pl.BlockSpec((None, None, WINDOW, KV_WIDTH), lambda b, i: (layer, b, 0, 0))
    if not carry:
        in_specs += [pl.BlockSpec((None, SUBLANES, SC_WIDTH), lambda b, i: (b, 0, 0)), kv_spec, kv_spec]
        args += [sc_state_pad, k_all_prev, v_all_prev]
    scratch += [pltpu.VMEM((SUBLANES + tq, SC_WIDTH), F32)]
    if not carry:
        scratch += [pltpu.VMEM((tq, ATT_WIDTH), F32), pltpu.VMEM((tq, KV_WIDTH), F32),
                    pltpu.VMEM((WINDOW, KV_WIDTH), F32), pltpu.VMEM((WINDOW, KV_WIDTH), F32)]
    aliases = {len(args): 0}
    in_specs += [pl.BlockSpec(memory_space=pl.ANY)]
    args += [y_mix]
    if kv_out_prev is not None:
        for out_idx, buf in zip((1, 2), kv_out_prev):
            aliases[len(args)] = out_idx
            in_specs += [pl.BlockSpec(memory_space=pl.ANY)]
            args += [buf]
    y, ko, vo, sco = pl.pallas_call(
        functools.partial(_att_kernel, L=L, nb=nb, carry=carry, n_alias=len(aliases)),
        grid=(batch, nb),
        in_specs=in_specs,
        out_specs=[y_spec, kv_spec, kv_spec,
                   pl.BlockSpec((None, SUBLANES, SC_WIDTH), lambda b, i: (b, 0, 0))],
        out_shape=[jax.ShapeDtypeStruct(y_mix.shape, y_mix.dtype),
                   jax.ShapeDtypeStruct((depth, batch, WINDOW, KV_WIDTH), F32),
                   jax.ShapeDtypeStruct((depth, batch, WINDOW, KV_WIDTH), F32),
                   jax.ShapeDtypeStruct((batch, SUBLANES, SC_WIDTH), F32)],
        scratch_shapes=scratch,
        input_output_aliases=aliases,
        compiler_params=_cparams(2),
        name="att_prompt" if carry else "att_decode",
    )(*args)
    base = SUBLANES - (SC_CONV - 1)
    off = base + L - (base + L) // SUBLANES * SUBLANES
    return y, (ko, vo), sco[:, off:off + SC_CONV - 1, :]


OUT_TM = 1024
OUT_KT = 1024
OUT_NT = 512
N_KSTEPS = D_MIX // OUT_KT
N_NSTEPS = D_MODEL // OUT_NT


def _outproj_kernel(y_ref, wo_ref, x_ref, npost_ref, p_ref, pp_ref, gw_ref, o_ref, acc_ref, xnb_ref):
    k = pl.program_id(1)

    @pl.when(k < N_KSTEPS)
    def _():
        y = y_ref[...].astype(BF16)
        for j in range(N_NSTEPS):
            part = jnp.dot(y, wo_ref[:, j * OUT_NT:(j + 1) * OUT_NT], preferred_element_type=F32)

            @pl.when(k == 0)
            def _():
                acc_ref[j] = part

            @pl.when(k > 0)
            def _():
                acc_ref[j] += part

    @pl.when(k == N_KSTEPS - 1)
    def _():
        ssq = None
        for j in range(N_NSTEPS):
            mix = acc_ref[j]
            s = jnp.sum(mix * mix, axis=-1, keepdims=True)
            ssq = s if ssq is None else ssq + s
        scale = lax.rsqrt(ssq * (1.0 / D_MODEL) + EPS)
        for j in range(N_NSTEPS):
            cols = slice(j * OUT_NT, (j + 1) * OUT_NT)
            xn = x_ref[:, cols] + (acc_ref[j] * scale) * npost_ref[:, cols]
            acc_ref[j] = xn
            xnb_ref[:, cols] = xn.astype(BF16)

    @pl.when(k >= N_KSTEPS)
    def _():
        j = k - N_KSTEPS
        gate = _sigmoid(jnp.dot(xnb_ref[...], gw_ref[...], preferred_element_type=F32))
        e = jnp.dot(p_ref[...].astype(BF16), pp_ref[...], preferred_element_type=F32)
        o_ref[...] = acc_ref[j] + gate * e


def _out_proj(y_mix, x2d, p_all, prm, *, layer):
    t = x2d.shape[0]
    tm = min(OUT_TM, t)

    def nstep(k):
        return jnp.maximum(k - N_KSTEPS, 0)

    return pl.pallas_call(
        _outproj_kernel,
        grid=(t // tm, N_KSTEPS + N_NSTEPS),
        in_specs=[
            pl.BlockSpec((tm, OUT_KT), lambda i, k: (i, jnp.minimum(k, N_KSTEPS - 1))),
            pl.BlockSpec((OUT_KT, D_MODEL), lambda i, k: (jnp.minimum(k, N_KSTEPS - 1), 0)),
            pl.BlockSpec((tm, D_MODEL), lambda i, k: (i, 0)),
            pl.BlockSpec((1, D_MODEL), lambda i, k: (0, 0)),
            pl.BlockSpec((None, tm, PLE_DIM), lambda i, k: (layer, i, 0)),
            pl.BlockSpec((PLE_DIM, OUT_NT), lambda i, k: (0, nstep(k))),
            pl.BlockSpec((D_MODEL, OUT_NT), lambda i, k: (0, nstep(k))),
        ],
        out_specs=pl.BlockSpec((tm, OUT_NT), lambda i, k: (i, nstep(k))),
        out_shape=jax.ShapeDtypeStruct((t, D_MODEL), F32),
        scratch_shapes=[pltpu.VMEM((N_NSTEPS, tm, OUT_NT), F32),
                        pltpu.VMEM((tm, D_MODEL), BF16)],
        compiler_params=_cparams(2),
        name="out_proj",
    )(y_mix.reshape(t, D_MIX), prm["w_out"], x2d, prm["norm_post"], p_all, prm["ple_proj"], prm["ple_gate"])


_W_IN_SRC = {}
_o = 0
for _name, _size in (("z", SSD_WIDTH), ("xbc", SSD_CONV_CH), ("dt", SSD_HEADS), ("q", ATT_WIDTH), ("k", KV_WIDTH),
                     ("v", KV_WIDTH), ("g_att", ATT_WIDTH), ("b_sc", SC_WIDTH), ("c_sc", SC_WIDTH),
                     ("h_sc", SC_WIDTH), ("g_sc", SC_WIDTH)):
    _W_IN_SRC[_name] = (_o, _size)
    _o += _size
N_IN = _o
_W_IN_DST = {"z": COL_Z, "xbc": COL_X, "q": COL_Q, "g_att": COL_GATT, "b_sc": COL_BSC, "c_sc": COL_CSC,
             "h_sc": COL_HSC, "g_sc": COL_GSC, "k": COL_K, "v": COL_V}
REGROUP_ROWS = 256


def _regroup_kernel(w_ref, o_ref):
    lane = lax.broadcasted_iota(jnp.int32, (REGROUP_ROWS, LANES), 1)
    for name, dst in _W_IN_DST.items():
        src, width = _W_IN_SRC[name]
        shift = src % LANES
        if shift == 0:
            o_ref[:, dst:dst + width] = w_ref[:, src:src + width].astype(BF16)
            continue
        base = src - shift
        prev = pltpu.roll(w_ref[:, base:base + LANES], LANES - shift, 1)
        for t in range(width // LANES):
            nxt = pltpu.roll(w_ref[:, base + (t + 1) * LANES:base + (t + 2) * LANES], LANES - shift, 1)
            tile = jnp.where(lane < LANES - shift, prev, nxt)
            o_ref[:, dst + t * LANES:dst + (t + 1) * LANES] = tile.astype(BF16)
            prev = nxt
    src, width = _W_IN_SRC["dt"]
    o_ref[:, COL_DT:COL_DT + LANES] = jnp.where(lane < width, w_ref[:, src:src + LANES], 0.0).astype(BF16)
    o_ref[:, COL_DT + LANES:N_PROJ] = jnp.zeros((REGROUP_ROWS, N_PROJ - COL_DT - LANES), BF16)


def _regroup_w_in(w_all, layer):
    return pl.pallas_call(
        _regroup_kernel,
        grid=(D_MODEL // REGROUP_ROWS,),
        in_specs=[pl.BlockSpec((None, REGROUP_ROWS, -(-N_IN // LANES) * LANES), lambda r: (layer, r, 0))],
        out_specs=pl.BlockSpec((REGROUP_ROWS, N_PROJ), lambda r: (r, 0)),
        out_shape=jax.ShapeDtypeStruct((D_MODEL, N_PROJ), BF16),
        compiler_params=_cparams(1),
        name="regroup_w_in",
    )(w_all)


def _lane_pad(v):
    return jnp.pad(v, (0, LANES - v.shape[0])).reshape(1, LANES)


def _layer_params(i, w_in, w_out, norm_pre, norm_post, ssd_conv_w, ssd_conv_b, ssd_a_log, ssd_dt_bias, ssd_d,
                  ssd_norm, attn_sinks, sc_conv_w, ple_proj, ple_gate, expand):
    return {
        "w_in": _regroup_w_in(w_in, i),
        "w_out": w_out[i].astype(BF16),
        "norm_pre": norm_pre[i].reshape(1, D_MODEL),
        "norm_post": norm_post[i].reshape(1, D_MODEL),
        "ssd_conv_w": ssd_conv_w[i],
        "ssd_conv_b": ssd_conv_b[i].reshape(1, SSD_CONV_CH),
        "a_log": _lane_pad(ssd_a_log[i]),
        "dt_bias": _lane_pad(ssd_dt_bias[i]),
        "d_skip": jnp.repeat(ssd_d[i], SSD_HEAD_DIM).reshape(1, SSD_WIDTH),
        "ssd_norm": ssd_norm[i].reshape(1, SSD_WIDTH),
        "sinks": attn_sinks[i],
        "sc_conv_w": sc_conv_w[i],
        "ple_proj": ple_proj[i].astype(BF16),
        "ple_gate": ple_gate[i].astype(BF16),
        "expand": expand,
    }


def _rope_tables(pos0, n):
    half = HEAD_DIM // 2
    inv_freq = jnp.exp(-math.log(ROPE_THETA) * jnp.arange(half, dtype=F32) * (2.0 / HEAD_DIM))
    pos = pos0 + jnp.arange(n, dtype=jnp.int32)
    ang = pos.astype(F32)[:, None] * inv_freq[None, :]
    cos, sin = jnp.cos(ang), jnp.sin(ang)
    reps = LANES // HEAD_DIM
    return (jnp.tile(jnp.concatenate([cos, cos], axis=1), (1, reps)),
            jnp.tile(jnp.concatenate([-sin, sin], axis=1), (1, reps)))


def _layer(x2d, p_all, prm, rope_tab, carried, *, layer, depth, batch, seqlen, conv_state, ssm_all, k_all, v_all,
           sc_state):
    decode = conv_state is not None
    ssm_out_prev, kv_out_prev = carried
    proj = _in_proj(x2d, prm["norm_pre"], prm["w_in"])
    if decode:
        sc_pad = jnp.pad(sc_state, ((0, 0), (SUBLANES - (SC_CONV - 1), 0), (0, 0)))
        y_mix, conv_new, ssm_out = _ssd_decode_mixer(proj, conv_state, ssm_all, ssm_out_prev, prm, layer=layer,
                                                     depth=depth, batch=batch, seqlen=seqlen)
        proj = proj.reshape(batch, seqlen, N_PROJ)
        y_mix = y_mix.reshape(batch, seqlen, D_MIX)
    else:
        sc_pad = None
        y_mix, conv_new, ssm_out = _ssd_mixer(proj, ssm_out_prev, prm, layer=layer, depth=depth, batch=batch,
                                              seqlen=seqlen)
    y_mix, kv_out, sc_new = _att_mixer(proj, y_mix, sc_pad, k_all, v_all, kv_out_prev, prm, rope_tab, layer=layer,
                                       depth=depth, batch=batch, seqlen=seqlen)
    x_new = _out_proj(y_mix, x2d, p_all, prm, layer=layer)
    return x_new, conv_new, sc_new, (ssm_out, kv_out)


def kernel(x_prompt, x_sample, p_prompt, p_sample, state_ssd_conv, state_ssm, cache_k, cache_v, state_sc_conv,
           w_in, w_out, norm_pre, norm_post, ssd_conv_w, ssd_conv_b, ssd_a_log, ssd_dt_bias, ssd_d, ssd_norm,
           attn_sinks, sc_conv_w, ple_proj, ple_gate):
    bp, lp, _ = x_prompt.shape
    bs, ls, _ = x_sample.shape
    depth = w_in.shape[0]
    assert lp % CHUNK == 0 and ls + SSD_CONV - 1 <= DEC_TILE and bs % DEC_G == 0 and cache_k.shape[2] == WINDOW
    head = jnp.arange(LANES, dtype=jnp.int32)[:, None]
    chan = jnp.arange(SSD_WIDTH, dtype=jnp.int32)[None, :]
    expand = (chan // SSD_HEAD_DIM == head).astype(BF16)
    rope_p = _rope_tables(0, lp)
    rope_s = _rope_tables(PAST_LEN, SUBLANES)
    yp = x_prompt.reshape(bp * lp, D_MODEL)
    ys = x_sample.reshape(bs * ls, D_MODEL)
    pp_all = p_prompt.reshape(depth, bp * lp, PLE_DIM)
    ps_all = p_sample.reshape(depth, bs * ls, PLE_DIM)
    ssm_all = state_ssm.reshape(depth, bs, SSD_WIDTH, SSD_STATE)
    k_all = cache_k.reshape(depth, bs, WINDOW, KV_WIDTH)
    v_all = cache_v.reshape(depth, bs, WINDOW, KV_WIDTH)
    carried_p = carried_s = (None, None)
    conv_p, sc_p, conv_s, sc_s = [], [], [], []
    for i in range(depth):
        prm = _layer_params(i, w_in, w_out, norm_pre, norm_post, ssd_conv_w, ssd_conv_b, ssd_a_log, ssd_dt_bias,
                            ssd_d, ssd_norm, attn_sinks, sc_conv_w, ple_proj, ple_gate, expand)
        yp, cv, sc, carried_p = _layer(yp, pp_all, prm, rope_p, carried_p, layer=i, depth=depth, batch=bp, seqlen=lp,
                                       conv_state=None, ssm_all=None, k_all=None, v_all=None, sc_state=None)
        conv_p.append(cv)
        sc_p.append(sc)
        ys, cv, sc, carried_s = _layer(ys, ps_all, prm, rope_s, carried_s, layer=i, depth=depth, batch=bs, seqlen=ls,
                                       conv_state=state_ssd_conv[i], ssm_all=ssm_all, k_all=k_all, v_all=v_all,
                                       sc_state=state_sc_conv[i])
        conv_s.append(cv)
        sc_s.append(sc)

    def states(conv, carried, sc, batch):
        ssm_out, (k_out, v_out) = carried
        return (jnp.stack(conv), ssm_out.reshape(depth, batch, SSD_HEADS, SSD_HEAD_DIM, SSD_STATE),
                k_out.reshape(depth, batch, WINDOW, N_KV_HEADS, HEAD_DIM),
                v_out.reshape(depth, batch, WINDOW, N_KV_HEADS, HEAD_DIM), jnp.stack(sc))

    return (yp.reshape(bp, lp, D_MODEL), ys.reshape(bs, ls, D_MODEL),
            *states(conv_p, carried_p, sc_p, bp), *states(conv_s, carried_s, sc_s, bs))
```

```python
import functools
import math

import jax
import jax.numpy as jnp
from jax import lax
from jax.experimental import pallas as pl
from jax.experimental.pallas import tpu as pltpu

F32 = jnp.float32
BF16 = jnp.bfloat16

D_MODEL = 2048
D_MIX = 2 * D_MODEL
SSD_WIDTH = D_MIX // 2
SSD_HEAD_DIM = 64
SSD_HEADS = SSD_WIDTH // SSD_HEAD_DIM
SSD_GROUPS = 4
SSD_STATE = 128
SSD_CONV = 4
SSD_CONV_CH = SSD_WIDTH + 2 * SSD_GROUPS * SSD_STATE
ATT_WIDTH = D_MIX // 4
HEAD_DIM = 64
N_Q_HEADS = ATT_WIDTH // HEAD_DIM
N_KV_HEADS = 4
KV_WIDTH = N_KV_HEADS * HEAD_DIM
WINDOW = 128
ROPE_THETA = 10000.0
SC_WIDTH = D_MIX // 4
SC_CONV = 3
PLE_DIM = 256
EPS = 1e-6
PAST_LEN = 8192

LANES = 128
SUBLANES = 8
CHUNK = 128
GROUP_W = SSD_WIDTH // SSD_GROUPS
HEADS_PER_GROUP = SSD_HEADS // SSD_GROUPS

COL_Z = 0
COL_X = 2048
COL_B = 4096
COL_C = 4608
COL_Q = 5120
COL_GATT = 6144
COL_BSC = 7168
COL_CSC = 8192
COL_HSC = 9216
COL_GSC = 10240
COL_K = 11264
COL_V = 11520
COL_DT = 11776
N_PROJ = 12288

VMEM_LIMIT = 56 * 1024 * 1024


def _cparams(ndims):
    return pltpu.CompilerParams(dimension_semantics=("arbitrary",) * ndims, vmem_limit_bytes=VMEM_LIMIT)


def _silu(x):
    h = 0.5 * x
    return h + h * jnp.tanh(h)


def _sigmoid(x):
    return 0.5 + 0.5 * jnp.tanh(0.5 * x)


def _inproj_kernel(x_ref, nw_ref, w_ref, o_ref, h_ref, *, tm, row_chunk):
    @pl.when(pl.program_id(1) == 0)
    def _():
        def body(r, carry):
            rows = pl.ds(pl.multiple_of(r * row_chunk, row_chunk), row_chunk)
            x = x_ref[rows, :]
            ms = jnp.mean(x * x, axis=-1, keepdims=True)
            h_ref[rows, :] = ((x * lax.rsqrt(ms + EPS)) * nw_ref[...]).astype(BF16)
            return carry

        lax.fori_loop(0, tm // row_chunk, body, 0)

    o_ref[...] = jnp.dot(h_ref[...], w_ref[...], preferred_element_type=F32)


def _in_proj(x2d, nw, w_bf16):
    t = x2d.shape[0]
    tm = min(1024, t)
    tn = 1024
    return pl.pallas_call(
        functools.partial(_inproj_kernel, tm=tm, row_chunk=128),
        grid=(t // tm, N_PROJ // tn),
        in_specs=[
            pl.BlockSpec((tm, D_MODEL), lambda i, j: (i, 0)),
            pl.BlockSpec((1, D_MODEL), lambda i, j: (0, 0)),
            pl.BlockSpec((D_MODEL, tn), lambda i, j: (0, j)),
        ],
        out_specs=pl.BlockSpec((tm, tn), lambda i, j: (i, j)),
        out_shape=jax.ShapeDtypeStruct((t, N_PROJ), F32),
        scratch_shapes=[pltpu.VMEM((tm, D_MODEL), BF16)],
        compiler_params=_cparams(2),
        name="in_proj",
    )(x2d, nw, w_bf16)


DEC_G = 4
DEC_TILE = SUBLANES


def _softplus(x):
    return jnp.maximum(x, 0.0) + jnp.log1p(jnp.exp(-jnp.abs(x)))


def _ssd_conv_silu(xpad, cw_ref, cbias_ref, xs_scr, bm_scr, cm_scr):
    cwid = 256
    base = SUBLANES - (SSD_CONV - 1)
    for cb in range(SSD_CONV_CH // cwid):
        cols = slice(cb * cwid, (cb + 1) * cwid)
        acc = xpad[base:base + CHUNK, cols] * cw_ref[0:1, cols]
        for j in range(1, SSD_CONV):
            acc = acc + xpad[base + j:base + j + CHUNK, cols] * cw_ref[j:j + 1, cols]
        act = _silu(acc + cbias_ref[:, cols])
        lo = cb * cwid
        if lo < SSD_WIDTH:
            xs_scr[:, lo:lo + cwid] = act
        elif lo < SSD_WIDTH + GROUP_W:
            bm_scr[:, lo - SSD_WIDTH:lo - SSD_WIDTH + cwid] = act.astype(BF16)
        else:
            o = lo - SSD_WIDTH - GROUP_W
            cm_scr[:, o:o + cwid] = act.astype(BF16)


def _ssd_expand(v, e_mat):
    hi = v.astype(BF16)
    lo = (v - hi.astype(F32)).astype(BF16)
    return jnp.dot(hi, e_mat, preferred_element_type=F32) + jnp.dot(lo, e_mat, preferred_element_type=F32)


def _ssd_diag(a_cum, dt, allowed, xs_scr, bm_scr, cm_scr, y_scr, after_group=None):
    Q = CHUNK
    a_cum_t = a_cum.T
    dt_t = dt.T
    lane = lax.broadcasted_iota(jnp.int32, (Q, LANES), 1)
    low_half = lane < SSD_HEAD_DIM
    for g in range(SSD_GROUPS):
        gcols = slice(g * SSD_STATE, (g + 1) * SSD_STATE)
        cbm = lax.dot_general(cm_scr[:, gcols], bm_scr[:, gcols], (((1,), (1,)), ((), ())),
                              preferred_element_type=F32)
        for p in range(HEADS_PER_GROUP // 2):
            pair = g * (HEADS_PER_GROUP // 2) + p
            ms = []
            for h in (2 * pair, 2 * pair + 1):
                seg = jnp.broadcast_to(a_cum[:, h:h + 1], (Q, Q)) - a_cum_t[h:h + 1, :]
                dec = jnp.exp(jnp.where(allowed, seg, -jnp.inf))
                ms.append((cbm * dec * dt_t[h:h + 1, :]).astype(BF16))
            lhs = jnp.concatenate(ms, axis=1)
            xp = xs_scr[:, pair * LANES:(pair + 1) * LANES]
            rhs = jnp.concatenate([jnp.where(low_half, xp, 0.0).astype(BF16),
                                   jnp.where(low_half, 0.0, xp).astype(BF16)], axis=0)
            y_scr[:, pair * LANES:(pair + 1) * LANES] = jnp.dot(lhs, rhs, preferred_element_type=F32)
        if after_group is not None:
            after_group(g)


def _ssd_gate_norm(yg, zg, nw):
    yg = yg * _silu(zg)
    msq = jnp.mean(yg * yg, axis=-1, keepdims=True)
    return (yg * lax.rsqrt(msq + EPS)) * nw


def _ssd_decode_kernel(*refs, L, n_alias):
    Q = CHUNK
    n_in = 14
    refs = refs[:n_in] + refs[n_in + n_alias:]
    (z_ref, x_ref, b_ref, c_ref, dt_ref, cs_ref, ss_ref, cw_ref, cbias_ref, alog_ref, dtb_ref, dsk_ref,
     nw_ref, e_ref, y_ref, cso_ref, sso_ref,
     xpad, xs_scr, bm_scr, cm_scr, y_scr, ea_scr, we_scr, zp_scr, dtp_scr) = refs
    t0 = DEC_TILE - L
    used = DEC_G * DEC_TILE

    xpad[...] = jnp.zeros(xpad.shape, F32)
    for g in range(DEC_G):
        r0 = SUBLANES + DEC_TILE * g
        tok = slice(L * g, L * (g + 1))
        xpad[r0:r0 + DEC_TILE, :] = cs_ref[g]
        xpad[r0 + t0:r0 + DEC_TILE, 0:SSD_WIDTH] = x_ref[tok, :]
        xpad[r0 + t0:r0 + DEC_TILE, SSD_WIDTH:SSD_WIDTH + GROUP_W] = b_ref[tok, :]
        xpad[r0 + t0:r0 + DEC_TILE, SSD_WIDTH + GROUP_W:SSD_CONV_CH] = c_ref[tok, :]
    for g in range(DEC_G):
        r0 = SUBLANES + DEC_TILE * g
        cso_ref[g] = xpad[r0:r0 + DEC_TILE, :]
    _ssd_conv_silu(xpad, cw_ref, cbias_ref, xs_scr, bm_scr, cm_scr)

    row = lax.broadcasted_iota(jnp.int32, (Q, LANES), 0)
    in_tile = row & (DEC_TILE - 1)
    dtp_scr[...] = jnp.zeros((Q, LANES), F32)
    zp_scr[...] = jnp.zeros(zp_scr.shape, F32)
    for g in range(DEC_G):
        dtp_scr[DEC_TILE * g + t0:DEC_TILE * (g + 1), :] = dt_ref[L * g:L * (g + 1), :]
        zp_scr[DEC_TILE * g + t0:DEC_TILE * (g + 1), :] = z_ref[L * g:L * (g + 1), :]
    dt = jnp.where((in_tile >= t0) & (row < used), _softplus(dtp_scr[...] + dtb_ref[...]), 0.0)
    a_cum = dt * (-jnp.exp(alog_ref[...]))
    k = 1
    while k < DEC_TILE:
        a_cum = a_cum + jnp.where(in_tile >= k, pltpu.roll(a_cum, k, 0), 0.0)
        k *= 2
    tot = jnp.where(in_tile == DEC_TILE - 1, a_cum, 0.0)
    k = 1
    while k < DEC_TILE:
        tot = tot + pltpu.roll(tot, Q - k, 0)
        k *= 2
    e_mat = e_ref[...]
    ea_scr[...] = _ssd_expand(jnp.exp(a_cum), e_mat)
    we_scr[...] = _ssd_expand(jnp.exp(tot - a_cum) * dt, e_mat)
    cd_t = jnp.exp(tot).T

    lrow = lax.broadcasted_iota(jnp.int32, (Q, Q), 0)
    scol = lax.broadcasted_iota(jnp.int32, (Q, Q), 1)
    allowed = (lrow >= scol) & ((lrow // DEC_TILE) == (scol // DEC_TILE))
    _ssd_diag(a_cum, dt, allowed, xs_scr, bm_scr, cm_scr, y_scr)

    pair_rows = 2 * DEC_TILE
    first_of_pair = lax.broadcasted_iota(jnp.int32, (pair_rows, GROUP_W), 0) < DEC_TILE
    for g in range(SSD_GROUPS):
        gcols = slice(g * SSD_STATE, (g + 1) * SSD_STATE)
        wcols = slice(g * GROUP_W, (g + 1) * GROUP_W)
        for sp in range(DEC_G // 2):
            rows = slice(sp * pair_rows, (sp + 1) * pair_rows)
            offs = [lax.dot_general(cm_scr[rows, gcols], ss_ref[2 * sp + j, wcols, :].astype(BF16),
                                    (((1,), (1,)), ((), ())), preferred_element_type=F32) for j in range(2)]
            y_off = jnp.where(first_of_pair, offs[0], offs[1])
            y_scr[rows, wcols] = y_scr[rows, wcols] + y_off * ea_scr[rows, wcols]
        yg = y_scr[0:used, wcols] + dsk_ref[:, wcols] * xs_scr[0:used, wcols]
        out = _ssd_gate_norm(yg, zp_scr[:, wcols], nw_ref[:, wcols])
        for s in range(DEC_G):
            y_ref[L * s:L * (s + 1), wcols] = out[DEC_TILE * s + t0:DEC_TILE * (s + 1)].astype(y_ref.dtype)

    row_seq = lax.broadcasted_iota(jnp.int32, (Q, SSD_STATE), 0) // DEC_TILE
    for g in range(SSD_GROUPS):
        wcols = slice(g * GROUP_W, (g + 1) * GROUP_W)
        gcols = slice(g * SSD_STATE, (g + 1) * SSD_STATE)
        xw_t = (xs_scr[:, wcols] * we_scr[:, wcols]).T.astype(BF16)
        bm_g = bm_scr[:, gcols]
        for s in range(DEC_G):
            st = jnp.dot(xw_t, jnp.where(row_seq == s, bm_g, jnp.zeros_like(bm_g)),
                         preferred_element_type=F32)
            last = DEC_TILE * (s + 1) - 1
            cd_rows = jnp.broadcast_to(cd_t[:, last:last + 1], (LANES, LANES))
            for hh in range(HEADS_PER_GROUP):
                h = g * HEADS_PER_GROUP + hh
                r0 = h * SSD_HEAD_DIM
                sso_ref[s, r0:r0 + SSD_HEAD_DIM, :] = (ss_ref[s, r0:r0 + SSD_HEAD_DIM, :] * cd_rows[h:h + 1, :]
                                                       + st[hh * SSD_HEAD_DIM:(hh + 1) * SSD_HEAD_DIM, :])


def _ssd_kernel(*refs, nc, n_alias):
    Q = CHUNK
    n_in = 12
    refs = refs[:n_in] + refs[n_in + n_alias:]
    (z_ref, x_ref, b_ref, c_ref, dt_ref, cw_ref, cbias_ref, alog_ref, dtb_ref, dsk_ref, nw_ref, e_ref,
     y_ref, cso_ref, sso_ref,
     xpad, xs_scr, bm_scr, cm_scr, y_scr, ea_scr, we_scr) = refs
    c = pl.program_id(1)

    @pl.when(c == 0)
    def _init():
        sso_ref[...] = jnp.zeros(sso_ref.shape, F32)
        xpad[0:SUBLANES, :] = jnp.zeros((SUBLANES, SSD_CONV_CH), F32)

    xpad[SUBLANES:SUBLANES + Q, 0:SSD_WIDTH] = x_ref[...]
    xpad[SUBLANES:SUBLANES + Q, SSD_WIDTH:SSD_WIDTH + GROUP_W] = b_ref[...]
    xpad[SUBLANES:SUBLANES + Q, SSD_WIDTH + GROUP_W:SSD_CONV_CH] = c_ref[...]
    _ssd_conv_silu(xpad, cw_ref, cbias_ref, xs_scr, bm_scr, cm_scr)

    tail_tile = xpad[Q:Q + SUBLANES, :]

    @pl.when(c == nc - 1)
    def _():
        cso_ref[...] = tail_tile

    xpad[0:SUBLANES, :] = tail_tile

    row = lax.broadcasted_iota(jnp.int32, (Q, LANES), 0)
    dt = _softplus(dt_ref[...] + dtb_ref[...])
    a_cum = dt * (-jnp.exp(alog_ref[...]))
    k = 1
    while k < Q:
        a_cum = a_cum + jnp.where(row >= k, pltpu.roll(a_cum, k, 0), 0.0)
        k *= 2
    a_last = a_cum[Q - 1:Q, :]
    e_mat = e_ref[...]
    ea_scr[...] = _ssd_expand(jnp.exp(a_cum), e_mat)
    we_scr[...] = _ssd_expand(jnp.exp(a_last - a_cum) * dt, e_mat)
    cd_rows = jnp.broadcast_to(jnp.exp(a_cum.T[:, Q - 1:Q]), (LANES, LANES))

    def carried_state_term(g):
        gcols = slice(g * SSD_STATE, (g + 1) * SSD_STATE)
        wcols = slice(g * GROUP_W, (g + 1) * GROUP_W)
        s_g = sso_ref[wcols, :].astype(BF16)
        y_off = lax.dot_general(cm_scr[:, gcols], s_g, (((1,), (1,)), ((), ())), preferred_element_type=F32)
        y_scr[:, wcols] = y_scr[:, wcols] + y_off * ea_scr[:, wcols] + dsk_ref[:, wcols] * xs_scr[:, wcols]

    causal = lax.broadcasted_iota(jnp.int32, (Q, Q), 0) >= lax.broadcasted_iota(jnp.int32, (Q, Q), 1)
    _ssd_diag(a_cum, dt, causal, xs_scr, bm_scr, cm_scr, y_scr, after_group=carried_state_term)

    for g in range(SSD_GROUPS):
        wcols = slice(g * GROUP_W, (g + 1) * GROUP_W)
        y_ref[:, wcols] = _ssd_gate_norm(y_scr[:, wcols], z_ref[:, wcols], nw_ref[:, wcols]).astype(y_ref.dtype)

    for g in range(SSD_GROUPS):
        wcols = slice(g * GROUP_W, (g + 1) * GROUP_W)
        gcols = slice(g * SSD_STATE, (g + 1) * SSD_STATE)
        xw_t = (xs_scr[:, wcols] * we_scr[:, wcols]).T.astype(BF16)
        st = jnp.dot(xw_t, bm_scr[:, gcols], preferred_element_type=F32)
        for hh in range(HEADS_PER_GROUP):
            h = g * HEADS_PER_GROUP + hh
            r0 = h * SSD_HEAD_DIM
            sso_ref[r0:r0 + SSD_HEAD_DIM, :] = (sso_ref[r0:r0 + SSD_HEAD_DIM, :] * cd_rows[h:h + 1, :]
                                                + st[hh * SSD_HEAD_DIM:(hh + 1) * SSD_HEAD_DIM, :])


def _ssd_specs_common(prm):
    shapes = [(SSD_CONV, SSD_CONV_CH), (1, SSD_CONV_CH), (1, LANES), (1, LANES), (1, SSD_WIDTH), (1, SSD_WIDTH),
              (LANES, SSD_WIDTH)]
    args = [prm["ssd_conv_w"], prm["ssd_conv_b"], prm["a_log"], prm["dt_bias"], prm["d_skip"], prm["ssd_norm"],
            prm["expand"]]
    scratch = [pltpu.VMEM((SUBLANES + CHUNK, SSD_CONV_CH), F32),
               pltpu.VMEM((CHUNK, SSD_WIDTH), F32),
               pltpu.VMEM((CHUNK, GROUP_W), BF16),
               pltpu.VMEM((CHUNK, GROUP_W), BF16),
               pltpu.VMEM((CHUNK, SSD_WIDTH), F32),
               pltpu.VMEM((CHUNK, SSD_WIDTH), F32),
               pltpu.VMEM((CHUNK, SSD_WIDTH), F32)]
    return shapes, args, scratch


def _ssd_mixer(proj, ssm_out_prev, prm, *, layer, depth, batch, seqlen):
    nc = seqlen // CHUNK

    def blk(width, col):
        return pl.BlockSpec((CHUNK, width), lambda b, c: (b * nc + c, col // width))

    def const(shape):
        return pl.BlockSpec(shape, lambda b, c: (0,) * len(shape))

    state_spec = pl.BlockSpec((None, None, SSD_WIDTH, SSD_STATE), lambda b, c: (layer, b, 0, 0))
    shapes, cargs, scratch = _ssd_specs_common(prm)
    in_specs = [blk(SSD_WIDTH, COL_Z), blk(SSD_WIDTH, COL_X), blk(GROUP_W, COL_B), blk(GROUP_W, COL_C),
                blk(LANES, COL_DT)] + [const(s) for s in shapes]
    args = [proj] * 5 + cargs
    aliases = {}
    if ssm_out_prev is not None:
        aliases[len(args)] = 2
        in_specs += [pl.BlockSpec(memory_space=pl.ANY)]
        args += [ssm_out_prev]
    y, cso, sso = pl.pallas_call(
        functools.partial(_ssd_kernel, nc=nc, n_alias=len(aliases)),
        grid=(batch, nc),
        in_specs=in_specs,
        out_specs=[pl.BlockSpec((CHUNK, SSD_WIDTH), lambda b, c: (b * nc + c, 0)),
                   pl.BlockSpec((None, SUBLANES, SSD_CONV_CH), lambda b, c: (b, 0, 0)),
                   state_spec],
        out_shape=[jax.ShapeDtypeStruct((batch * seqlen, D_MIX), BF16),
                   jax.ShapeDtypeStruct((batch, SUBLANES, SSD_CONV_CH), F32),
                   jax.ShapeDtypeStruct((depth, batch, SSD_WIDTH, SSD_STATE), F32)],
        scratch_shapes=scratch,
        input_output_aliases=aliases,
        compiler_params=_cparams(2),
        name="ssd_prompt",
    )(*args)
    return y, cso[:, SUBLANES - (SSD_CONV - 1):, :], sso


def _ssd_decode_mixer(proj, conv_state, ssm_all, ssm_out_prev, prm, *, layer, depth, batch, seqlen):
    t0 = DEC_TILE - seqlen
    hist = SSD_CONV - 1
    conv_pad = jnp.pad(conv_state, ((0, 0), (t0 - hist, DEC_TILE - t0), (0, 0)))
    rows = DEC_G * seqlen

    def blk(width, col):
        return pl.BlockSpec((rows, width), lambda s: (s, col // width))

    def const(shape):
        return pl.BlockSpec(shape, lambda s: (0,) * len(shape))

    state_spec = pl.BlockSpec((None, DEC_G, SSD_WIDTH, SSD_STATE), lambda s: (layer, s, 0, 0))
    shapes, cargs, scratch = _ssd_specs_common(prm)
    in_specs = [blk(SSD_WIDTH, COL_Z), blk(SSD_WIDTH, COL_X), blk(GROUP_W, COL_B), blk(GROUP_W, COL_C),
                blk(LANES, COL_DT),
                pl.BlockSpec((DEC_G, DEC_TILE, SSD_CONV_CH), lambda s: (s, 0, 0)), state_spec]
    in_specs += [const(s) for s in shapes]
    args = [proj] * 5 + [conv_pad, ssm_all] + cargs
    aliases = {}
    if ssm_out_prev is not None:
        aliases[len(args)] = 2
        in_specs += [pl.BlockSpec(memory_space=pl.ANY)]
        args += [ssm_out_prev]
    scratch += [pltpu.VMEM((DEC_G * DEC_TILE, SSD_WIDTH), F32), pltpu.VMEM((CHUNK, LANES), F32)]
    y, cso, sso = pl.pallas_call(
        functools.partial(_ssd_decode_kernel, L=seqlen, n_alias=len(aliases)),
        grid=(batch // DEC_G,),
        in_specs=in_specs,
        out_specs=[pl.BlockSpec((rows, SSD_WIDTH), lambda s: (s, 0)),
                   pl.BlockSpec((DEC_G, DEC_TILE, SSD_CONV_CH), lambda s: (s, 0, 0)),
                   state_spec],
        out_shape=[jax.ShapeDtypeStruct((batch * seqlen, D_MIX), F32),
                   jax.ShapeDtypeStruct((batch, DEC_TILE, SSD_CONV_CH), F32),
                   jax.ShapeDtypeStruct((depth, batch, SSD_WIDTH, SSD_STATE), F32)],
        scratch_shapes=scratch,
        input_output_aliases=aliases,
        compiler_params=_cparams(1),
        name="ssd_decode",
    )(*args)
    return y, cso[:, DEC_TILE - hist:, :], sso


def _att_stack_cols(tq):
    return -(-(N_Q_HEADS // N_KV_HEADS) * tq // LANES) * LANES


def _att_kernel(*refs, L, nb, carry, n_alias):
    TQ = CHUNK if L == CHUNK else SUBLANES
    W = WINDOW
    n_in = 12 if carry else 15
    refs = refs[:n_in] + refs[n_in + n_alias:]
    if carry:
        (q_ref, k_ref, v_ref, g_ref, sb_ref, sc_ref, sh_ref, sg_ref, cos_ref, sin_ref, sinks_ref, scw_ref,
         y_ref, ko_ref, vo_ref, sco_ref,
         kd, vt, xpad2) = refs
    else:
        (q_ref, k_ref, v_ref, g_ref, sb_ref, sc_ref, sh_ref, sg_ref, cos_ref, sin_ref, sinks_ref, scw_ref,
         sch_ref, kp_ref, vp_ref,
         y_ref, ko_ref, vo_ref, sco_ref,
         kd, vt, xpad2, pad_w, pad_kv, kc_scr, vc_scr) = refs
    i = pl.program_id(1)
    padded = L != TQ
    R = _att_stack_cols(TQ)

    def full(ref, scr):
        if not padded:
            return ref[...]
        scr[...] = jnp.zeros(scr.shape, F32)
        scr[0:L, :] = ref[...]
        return scr[...]

    if padded:
        vals = {name: full(ref, pad_w) for name, ref in
                (("q", q_ref), ("g", g_ref), ("sb", sb_ref), ("sc", sc_ref), ("sh", sh_ref), ("sg", sg_ref))}

        def get(name, cols=slice(None)):
            return vals[name][:, cols]
    else:
        srcs = {"q": q_ref, "g": g_ref, "sb": sb_ref, "sc": sc_ref, "sh": sh_ref, "sg": sg_ref}

        def get(name, cols=slice(None)):
            return srcs[name][:, cols]

    lane = lax.broadcasted_iota(jnp.int32, (TQ, LANES), 1)
    first_half = (lane % HEAD_DIM) < (HEAD_DIM // 2)
    cos = cos_ref[...]
    sin = sin_ref[...]

    def rope(x):
        sw = jnp.where(first_half, pltpu.roll(x, LANES - HEAD_DIM // 2, 1), pltpu.roll(x, HEAD_DIM // 2, 1))
        return x * cos + sw * sin

    k_new = full(k_ref, pad_kv if padded else None)
    k_rot = jnp.concatenate([rope(k_new[:, 0:LANES]), rope(k_new[:, LANES:2 * LANES])], axis=1)
    v_new = full(v_ref, pad_kv if padded else None)
    low_w = lax.broadcasted_iota(jnp.int32, (W, LANES), 1) < HEAD_DIM

    def stage_kv(slot, k_blk, v_blk):
        for kvp in range(N_KV_HEADS // 2):
            kpair = k_blk[:, kvp * LANES:(kvp + 1) * LANES]
            kroll = pltpu.roll(kpair, HEAD_DIM, 1)
            kd[slot, 2 * kvp] = jnp.where(low_w, kpair, kroll).astype(BF16)
            kd[slot, 2 * kvp + 1] = jnp.where(low_w, kroll, kpair).astype(BF16)
            vt[slot, kvp] = v_blk[:, kvp * LANES:(kvp + 1) * LANES].T.astype(BF16)

    def to_channel_major(x):
        return jnp.concatenate([x[:, 0:LANES].T, x[:, LANES:2 * LANES].T], axis=0)

    def to_key_major(xt):
        return jnp.concatenate([xt[0:LANES, :].T, xt[LANES:2 * LANES, :].T], axis=1)

    rowk = lax.broadcasted_iota(jnp.int32, (W, KV_WIDTH), 0)
    if carry:
        cur_slot = lax.rem(i, 2)

        @pl.when(i == 0)
        def _():
            kd[1] = jnp.zeros((N_KV_HEADS, W, LANES), BF16)
            vt[1] = jnp.zeros((N_KV_HEADS // 2, LANES, W), BF16)

        stage_kv(cur_slot, k_rot, v_new)

        @pl.when(i == nb - 1)
        def _():
            ko_ref[...] = to_channel_major(k_rot)
            vo_ref[...] = to_channel_major(v_new)
    else:
        k_prev = to_key_major(kp_ref[...])
        v_prev = to_key_major(vp_ref[...])
        kc_scr[...] = jnp.zeros((W, KV_WIDTH), F32)
        kc_scr[0:TQ, :] = k_rot
        vc_scr[...] = jnp.zeros((W, KV_WIDTH), F32)
        vc_scr[0:TQ, :] = v_new
        k_cur = kc_scr[...]
        v_cur = vc_scr[...]
        stage_kv(0, k_prev, v_prev)
        stage_kv(1, k_cur, v_cur)
        ko_ref[...] = to_channel_major(pltpu.roll(jnp.where(rowk < L, k_cur, k_prev), W - L, 0))
        vo_ref[...] = to_channel_major(pltpu.roll(jnp.where(rowk < L, v_cur, v_prev), W - L, 0))

    s_idx = lax.broadcasted_iota(jnp.int32, (W, R), 0)
    t_idx = lax.broadcasted_iota(jnp.int32, (W, R), 1) & (TQ - 1)
    cur_mask = s_idx <= t_idx
    prev_mask = s_idx > t_idx
    if carry:
        prev_mask = prev_mask & (i > 0)
        slot0_cur = cur_slot == 0
        masks = ((cur_mask & slot0_cur) | (prev_mask & jnp.logical_not(slot0_cur)),
                 (prev_mask & slot0_cur) | (cur_mask & jnp.logical_not(slot0_cur)))
    else:
        masks = (prev_mask, cur_mask)
    low = lane < HEAD_DIM
    heads_per_kv = N_Q_HEADS // N_KV_HEADS

    for kvh in range(N_KV_HEADS):
        kvp, half = divmod(kvh, 2)
        qs = []
        for qq in range(heads_per_kv // 2):
            qp = kvh * (heads_per_kv // 2) + qq
            q_rot = rope(get("q", slice(qp * LANES, (qp + 1) * LANES))) * (HEAD_DIM ** -0.5)
            qs += [jnp.where(low, q_rot, 0.0), jnp.where(low, 0.0, q_rot)]
        if heads_per_kv * TQ < R:
            qs.append(jnp.zeros((R - heads_per_kv * TQ, LANES), F32))
        q_stack = jnp.concatenate(qs, axis=0).astype(BF16)
        s = [jnp.where(masks[slot],
                       lax.dot_general(kd[slot, kvh], q_stack, (((1,), (1,)), ((), ())),
                                       preferred_element_type=F32), -jnp.inf) for slot in range(2)]
        sink = sinks_ref[kvh:kvh + 1, :]
        m = jnp.maximum(jnp.maximum(jnp.max(s[0], axis=0, keepdims=True), jnp.max(s[1], axis=0, keepdims=True)),
                        sink)
        p_un = [jnp.exp(s[slot] - m) for slot in range(2)]
        den = (jnp.sum(p_un[0], axis=0, keepdims=True) + jnp.sum(p_un[1], axis=0, keepdims=True)
               + jnp.exp(sink - m))
        o_t = (jnp.dot(vt[0, kvp], p_un[0].astype(BF16), preferred_element_type=F32)
               + jnp.dot(vt[1, kvp], p_un[1].astype(BF16), preferred_element_type=F32)) * (1.0 / den)
        for qq in range(heads_per_kv // 2):
            qp = kvh * (heads_per_kv // 2) + qq
            qcols = slice(qp * LANES, (qp + 1) * LANES)
            if carry:
                rows = slice(half * HEAD_DIM, (half + 1) * HEAD_DIM)
                o = jnp.concatenate([o_t[rows, (2 * qq) * TQ:(2 * qq + 1) * TQ],
                                     o_t[rows, (2 * qq + 1) * TQ:(2 * qq + 2) * TQ]], axis=0).T
            else:
                if qq == 0:
                    o_all = o_t.T
                oa = o_all[(2 * qq) * TQ:(2 * qq + 1) * TQ, :]
                ob = o_all[(2 * qq + 1) * TQ:(2 * qq + 2) * TQ, :]
                if half == 0:
                    ob = pltpu.roll(ob, HEAD_DIM, 1)
                else:
                    oa = pltpu.roll(oa, HEAD_DIM, 1)
                o = jnp.where(low, oa, ob)
            y_ref[:, qcols] = (o * _silu(get("g", qcols)))[0:L].astype(y_ref.dtype)

    @pl.when(i == 0)
    def _():
        if carry:
            xpad2[0:SUBLANES, :] = jnp.zeros((SUBLANES, SC_WIDTH), F32)
        else:
            xpad2[0:SUBLANES, :] = sch_ref[...]

    u = get("sc") * get("sh")
    xpad2[SUBLANES:SUBLANES + TQ, :] = u
    base = SUBLANES - (SC_CONV - 1)
    conv = xpad2[base:base + TQ, :] * scw_ref[0:1, :]
    for j in range(1, SC_CONV):
        conv = conv + xpad2[base + j:base + j + TQ, :] * scw_ref[j:j + 1, :]
    ysc = (get("sb") * conv) * _silu(get("sg"))
    y_ref[:, ATT_WIDTH:ATT_WIDTH + SC_WIDTH] = ysc[0:L].astype(y_ref.dtype)

    tail = (base + L) // SUBLANES * SUBLANES
    tail_tile = xpad2[tail:tail + SUBLANES, :]

    @pl.when(i == nb - 1)
    def _():
        sco_ref[...] = tail_tile

    if carry:
        xpad2[0:SUBLANES, :] = tail_tile


def _att_mixer(proj, y_mix, sc_state_pad, k_all_prev, v_all_prev, kv_out_prev, prm, rope_tab, *, layer, depth,
               batch, seqlen):
    carry = sc_state_pad is None
    if seqlen >= CHUNK:
        L, nb = CHUNK, seqlen // CHUNK
        src = proj

        def blk(width, col):
            return pl.BlockSpec((L, width), lambda b, i: (b * nb + i, col // width))

        y_spec = pl.BlockSpec((L, ATT_WIDTH + SC_WIDTH), lambda b, i: (b * nb + i, 1))
        tq = CHUNK
    else:
        L, nb = seqlen, 1
        src = proj

        def blk(width, col):
            return pl.BlockSpec((None, L, width), lambda b, i: (b, 0, col // width))

        y_spec = pl.BlockSpec((None, L, ATT_WIDTH + SC_WIDTH), lambda b, i: (b, 0, 1))
        tq = SUBLANES

    def const(shape):
        return pl.BlockSpec(shape, lambda b, i: (0,) * len(shape))

    cos_t, sin_t = rope_tab
    n_cols = _att_stack_cols(tq)
    sink_rows = jnp.repeat(prm["sinks"].reshape(N_KV_HEADS, N_Q_HEADS // N_KV_HEADS), tq, axis=1)
    sink_rows = jnp.pad(sink_rows, ((0, 0), (0, n_cols - sink_rows.shape[1])))
    in_specs = [blk(ATT_WIDTH, COL_Q), blk(KV_WIDTH, COL_K), blk(KV_WIDTH, COL_V), blk(ATT_WIDTH, COL_GATT),
                blk(SC_WIDTH, COL_BSC), blk(SC_WIDTH, COL_CSC), blk(SC_WIDTH, COL_HSC), blk(SC_WIDTH, COL_GSC),
                pl.BlockSpec((tq, LANES), lambda b, i: (i, 0)), pl.BlockSpec((tq, LANES), lambda b, i: (i, 0)),
                const((N_KV_HEADS, n_cols)), const((SC_CONV, SC_WIDTH))]
    args = [src] * 8 + [cos_t, sin_t, sink_rows, prm["sc_conv_w"]]
    scratch = [pltpu.VMEM((2, N_KV_HEADS, WINDOW, LANES), BF16),
               pltpu.VMEM((2, N_KV_HEADS // 2, LANES, WINDOW), BF16)]
    kv_spec = pl.BlockSpec((None, None, KV_WIDTH, WINDOW), lambda b, i: (layer, b, 0, 0))
    if not carry:
        in_specs += [pl.BlockSpec((None, SUBLANES, SC_WIDTH), lambda b, i: (b, 0, 0)), kv_spec, kv_spec]
        args += [sc_state_pad, k_all_prev, v_all_prev]
    scratch += [pltpu.VMEM((SUBLANES + tq, SC_WIDTH), F32)]
    if not carry:
        scratch += [pltpu.VMEM((tq, ATT_WIDTH), F32), pltpu.VMEM((tq, KV_WIDTH), F32),
                    pltpu.VMEM((WINDOW, KV_WIDTH), F32), pltpu.VMEM((WINDOW, KV_WIDTH), F32)]
    aliases = {len(args): 0}
    in_specs += [pl.BlockSpec(memory_space=pl.ANY)]
    args += [y_mix]
    if kv_out_prev is not None:
        for out_idx, buf in zip((1, 2), kv_out_prev):
            aliases[len(args)] = out_idx
            in_specs += [pl.BlockSpec(memory_space=pl.ANY)]
            args += [buf]
    y, ko, vo, sco = pl.pallas_call(
        functools.partial(_att_kernel, L=L, nb=nb, carry=carry, n_alias=len(aliases)),
        grid=(batch, nb),
        in_specs=in_specs,
        out_specs=[y_spec, kv_spec, kv_spec,
                   pl.BlockSpec((None, SUBLANES, SC_WIDTH), lambda b, i: (b, 0, 0))],
        out_shape=[jax.ShapeDtypeStruct(y_mix.shape, y_mix.dtype),
                   jax.ShapeDtypeStruct((depth, batch, KV_WIDTH, WINDOW), F32),
                   jax.ShapeDtypeStruct((depth, batch, KV_WIDTH, WINDOW), F32),
                   jax.ShapeDtypeStruct((batch, SUBLANES, SC_WIDTH), F32)],
        scratch_shapes=scratch,
        input_output_aliases=aliases,
        compiler_params=_cparams(2),
        name="att_prompt" if carry else "att_decode",
    )(*args)
    base = SUBLANES - (SC_CONV - 1)
    off = base + L - (base + L) // SUBLANES * SUBLANES
    return y, (ko, vo), sco[:, off:off + SC_CONV - 1, :]


OUT_TM = 1024
OUT_KT = 1024
OUT_NT = 512
N_KSTEPS = D_MIX // OUT_KT
N_NSTEPS = D_MODEL // OUT_NT


def _outproj_kernel(y_ref, wo_ref, x_ref, npost_ref, p_ref, pp_ref, gw_ref, o_ref, acc_ref, xnb_ref):
    k = pl.program_id(1)

    @pl.when(k == 0)
    def _():
        acc_ref[...] = jnp.dot(y_ref[...].astype(BF16), wo_ref[...], preferred_element_type=F32)

    @pl.when((k > 0) & (k < N_KSTEPS))
    def _():
        acc_ref[...] += jnp.dot(y_ref[...].astype(BF16), wo_ref[...], preferred_element_type=F32)

    @pl.when(k == N_KSTEPS - 1)
    def _():
        rows = min(256, acc_ref.shape[0])
        for r in range(acc_ref.shape[0] // rows):
            rs = slice(r * rows, (r + 1) * rows)
            mix = acc_ref[rs, :]
            ms = jnp.mean(mix * mix, axis=-1, keepdims=True)
            xn = x_ref[rs, :] + (mix * lax.rsqrt(ms + EPS)) * npost_ref[...]
            acc_ref[rs, :] = xn
            xnb_ref[rs, :] = xn.astype(BF16)

    for j in range(N_NSTEPS):
        @pl.when(k == N_KSTEPS + j)
        def _():
            gate = _sigmoid(jnp.dot(xnb_ref[...], gw_ref[...], preferred_element_type=F32))
            e = jnp.dot(p_ref[...].astype(BF16), pp_ref[...], preferred_element_type=F32)
            o_ref[...] = acc_ref[:, j * OUT_NT:(j + 1) * OUT_NT] + gate * e


def _out_proj(y_mix, x2d, p_all, prm, *, layer):
    t = x2d.shape[0]
    tm = min(OUT_TM, t)

    def nstep(k):
        return jnp.maximum(k - N_KSTEPS, 0)

    return pl.pallas_call(
        _outproj_kernel,
        grid=(t // tm, N_KSTEPS + N_NSTEPS),
        in_specs=[
            pl.BlockSpec((tm, OUT_KT), lambda i, k: (i, jnp.minimum(k, N_KSTEPS - 1))),
            pl.BlockSpec((OUT_KT, D_MODEL), lambda i, k: (jnp.minimum(k, N_KSTEPS - 1), 0)),
            pl.BlockSpec((tm, D_MODEL), lambda i, k: (i, 0)),
            pl.BlockSpec((1, D_MODEL), lambda i, k: (0, 0)),
            pl.BlockSpec((None, tm, PLE_DIM), lambda i, k: (layer, i, 0)),
            pl.BlockSpec((PLE_DIM, OUT_NT), lambda i, k: (0, nstep(k))),
            pl.BlockSpec((D_MODEL, OUT_NT), lambda i, k: (0, nstep(k))),
        ],
        out_specs=pl.BlockSpec((tm, OUT_NT), lambda i, k: (i, nstep(k))),
        out_shape=jax.ShapeDtypeStruct((t, D_MODEL), F32),
        scratch_shapes=[pltpu.VMEM((tm, D_MODEL), F32),
                        pltpu.VMEM((tm, D_MODEL), BF16)],
        compiler_params=_cparams(2),
        name="out_proj",
    )(y_mix.reshape(t, D_MIX), prm["w_out"], x2d, prm["norm_post"], p_all, prm["ple_proj"], prm["ple_gate"])


_W_IN_SRC = {}
_o = 0
for _name, _size in (("z", SSD_WIDTH), ("xbc", SSD_CONV_CH), ("dt", SSD_HEADS), ("q", ATT_WIDTH), ("k", KV_WIDTH),
                     ("v", KV_WIDTH), ("g_att", ATT_WIDTH), ("b_sc", SC_WIDTH), ("c_sc", SC_WIDTH),
                     ("h_sc", SC_WIDTH), ("g_sc", SC_WIDTH)):
    _W_IN_SRC[_name] = (_o, _size)
    _o += _size
N_IN = _o
_W_IN_DST = {"z": COL_Z, "xbc": COL_X, "q": COL_Q, "g_att": COL_GATT, "b_sc": COL_BSC, "c_sc": COL_CSC,
             "h_sc": COL_HSC, "g_sc": COL_GSC, "k": COL_K, "v": COL_V}
REGROUP_ROWS = 256


def _regroup_kernel(w_ref, o_ref):
    lane = lax.broadcasted_iota(jnp.int32, (REGROUP_ROWS, LANES), 1)
    for name, dst in _W_IN_DST.items():
        src, width = _W_IN_SRC[name]
        shift = src % LANES
        if shift == 0:
            o_ref[:, dst:dst + width] = w_ref[:, src:src + width].astype(BF16)
            continue
        base = src - shift
        prev = pltpu.roll(w_ref[:, base:base + LANES], LANES - shift, 1)
        for t in range(width // LANES):
            nxt = pltpu.roll(w_ref[:, base + (t + 1) * LANES:base + (t + 2) * LANES], LANES - shift, 1)
            tile = jnp.where(lane < LANES - shift, prev, nxt)
            o_ref[:, dst + t * LANES:dst + (t + 1) * LANES] = tile.astype(BF16)
            prev = nxt
    src, width = _W_IN_SRC["dt"]
    o_ref[:, COL_DT:COL_DT + LANES] = jnp.where(lane < width, w_ref[:, src:src + LANES], 0.0).astype(BF16)
    o_ref[:, COL_DT + LANES:N_PROJ] = jnp.zeros((REGROUP_ROWS, N_PROJ - COL_DT - LANES), BF16)


def _regroup_w_in(w_all, layer):
    return pl.pallas_call(
        _regroup_kernel,
        grid=(D_MODEL // REGROUP_ROWS,),
        in_specs=[pl.BlockSpec((None, REGROUP_ROWS, -(-N_IN // LANES) * LANES), lambda r: (layer, r, 0))],
        out_specs=pl.BlockSpec((REGROUP_ROWS, N_PROJ), lambda r: (r, 0)),
        out_shape=jax.ShapeDtypeStruct((D_MODEL, N_PROJ), BF16),
        compiler_params=_cparams(1),
        name="regroup_w_in",
    )(w_all)


def _lane_pad(v):
    return jnp.pad(v, (0, LANES - v.shape[0])).reshape(1, LANES)


def _layer_params(i, w_in, w_out, norm_pre, norm_post, ssd_conv_w, ssd_conv_b, ssd_a_log, ssd_dt_bias, ssd_d,
                  ssd_norm, attn_sinks, sc_conv_w, ple_proj, ple_gate, expand):
    return {
        "w_in": _regroup_w_in(w_in, i),
        "w_out": w_out[i].astype(BF16),
        "norm_pre": norm_pre[i].reshape(1, D_MODEL),
        "norm_post": norm_post[i].reshape(1, D_MODEL),
        "ssd_conv_w": ssd_conv_w[i],
        "ssd_conv_b": ssd_conv_b[i].reshape(1, SSD_CONV_CH),
        "a_log": _lane_pad(ssd_a_log[i]),
        "dt_bias": _lane_pad(ssd_dt_bias[i]),
        "d_skip": jnp.repeat(ssd_d[i], SSD_HEAD_DIM).reshape(1, SSD_WIDTH),
        "ssd_norm": ssd_norm[i].reshape(1, SSD_WIDTH),
        "sinks": attn_sinks[i],
        "sc_conv_w": sc_conv_w[i],
        "ple_proj": ple_proj[i].astype(BF16),
        "ple_gate": ple_gate[i].astype(BF16),
        "expand": expand,
    }


def _rope_tables(pos0, n):
    half = HEAD_DIM // 2
    inv_freq = jnp.exp(-math.log(ROPE_THETA) * jnp.arange(half, dtype=F32) * (2.0 / HEAD_DIM))
    pos = pos0 + jnp.arange(n, dtype=jnp.int32)
    ang = pos.astype(F32)[:, None] * inv_freq[None, :]
    cos, sin = jnp.cos(ang), jnp.sin(ang)
    reps = LANES // HEAD_DIM
    return (jnp.tile(jnp.concatenate([cos, cos], axis=1), (1, reps)),
            jnp.tile(jnp.concatenate([-sin, sin], axis=1), (1, reps)))


def _layer(x2d, p_all, prm, rope_tab, carried, *, layer, depth, batch, seqlen, conv_state, ssm_all, k_all, v_all,
           sc_state):
    decode = conv_state is not None
    ssm_out_prev, kv_out_prev = carried
    proj = _in_proj(x2d, prm["norm_pre"], prm["w_in"])
    if decode:
        sc_pad = jnp.pad(sc_state, ((0, 0), (SUBLANES - (SC_CONV - 1), 0), (0, 0)))
        y_mix, conv_new, ssm_out = _ssd_decode_mixer(proj, conv_state, ssm_all, ssm_out_prev, prm, layer=layer,
                                                     depth=depth, batch=batch, seqlen=seqlen)
        proj = proj.reshape(batch, seqlen, N_PROJ)
        y_mix = y_mix.reshape(batch, seqlen, D_MIX)
    else:
        sc_pad = None
        y_mix, conv_new, ssm_out = _ssd_mixer(proj, ssm_out_prev, prm, layer=layer, depth=depth, batch=batch,
                                              seqlen=seqlen)
    y_mix, kv_out, sc_new = _att_mixer(proj, y_mix, sc_pad, k_all, v_all, kv_out_prev, prm, rope_tab, layer=layer,
                                       depth=depth, batch=batch, seqlen=seqlen)
    x_new = _out_proj(y_mix, x2d, p_all, prm, layer=layer)
    return x_new, conv_new, sc_new, (ssm_out, kv_out)


def kernel(x_prompt, x_sample, p_prompt, p_sample, state_ssd_conv, state_ssm, cache_k, cache_v, state_sc_conv,
           w_in, w_out, norm_pre, norm_post, ssd_conv_w, ssd_conv_b, ssd_a_log, ssd_dt_bias, ssd_d, ssd_norm,
           attn_sinks, sc_conv_w, ple_proj, ple_gate):
    bp, lp, _ = x_prompt.shape
    bs, ls, _ = x_sample.shape
    depth = w_in.shape[0]
    assert lp % CHUNK == 0 and ls + SSD_CONV - 1 <= DEC_TILE and bs % DEC_G == 0 and cache_k.shape[2] == WINDOW
    head = jnp.arange(LANES, dtype=jnp.int32)[:, None]
    chan = jnp.arange(SSD_WIDTH, dtype=jnp.int32)[None, :]
    expand = (chan // SSD_HEAD_DIM == head).astype(BF16)
    rope_p = _rope_tables(0, lp)
    rope_s = _rope_tables(PAST_LEN, SUBLANES)
    yp = x_prompt.reshape(bp * lp, D_MODEL)
    ys = x_sample.reshape(bs * ls, D_MODEL)
    pp_all = p_prompt.reshape(depth, bp * lp, PLE_DIM)
    ps_all = p_sample.reshape(depth, bs * ls, PLE_DIM)
    ssm_all = state_ssm.reshape(depth, bs, SSD_WIDTH, SSD_STATE)
    k_all = cache_k.transpose(0, 1, 3, 4, 2).reshape(depth, bs, KV_WIDTH, WINDOW)
    v_all = cache_v.transpose(0, 1, 3, 4, 2).reshape(depth, bs, KV_WIDTH, WINDOW)
    carried_p = carried_s = (None, None)
    conv_p, sc_p, conv_s, sc_s = [], [], [], []
    for i in range(depth):
        prm = _layer_params(i, w_in, w_out, norm_pre, norm_post, ssd_conv_w, ssd_conv_b, ssd_a_log, ssd_dt_bias,
                            ssd_d, ssd_norm, attn_sinks, sc_conv_w, ple_proj, ple_gate, expand)
        yp, cv, sc, carried_p = _layer(yp, pp_all, prm, rope_p, carried_p, layer=i, depth=depth, batch=bp, seqlen=lp,
                                       conv_state=None, ssm_all=None, k_all=None, v_all=None, sc_state=None)
        conv_p.append(cv)
        sc_p.append(sc)
        ys, cv, sc, carried_s = _layer(ys, ps_all, prm, rope_s, carried_s, layer=i, depth=depth, batch=bs, seqlen=ls,
                                       conv_state=state_ssd_conv[i], ssm_all=ssm_all, k_all=k_all, v_all=v_all,
                                       sc_state=state_sc_conv[i])
        conv_s.append(cv)
        sc_s.append(sc)

    def states(conv, carried, sc, batch):
        ssm_out, (k_out, v_out) = carried
        return (jnp.stack(conv), ssm_out.reshape(depth, batch, SSD_HEADS, SSD_HEAD_DIM, SSD_STATE),
                k_out.reshape(depth, batch, N_KV_HEADS, HEAD_DIM, WINDOW).transpose(0, 1, 4, 2, 3),
                v_out.reshape(depth, batch, N_KV_HEADS, HEAD_DIM, WINDOW).transpose(0, 1, 4, 2, 3), jnp.stack(sc))

    return (yp.reshape(bp, lp, D_MODEL), ys.reshape(bs, ls, D_MODEL),
            *states(conv_p, carried_p, sc_p, bp), *states(conv_s, carried_s, sc_s, bs))
```

```python
import functools
import math

import jax
import jax.numpy as jnp
from jax import lax
from jax.experimental import pallas as pl
from jax.experimental.pallas import tpu as pltpu

F32 = jnp.float32
BF16 = jnp.bfloat16

D_MODEL = 2048
D_MIX = 2 * D_MODEL
SSD_WIDTH = D_MIX // 2
SSD_HEAD_DIM = 64
SSD_HEADS = SSD_WIDTH // SSD_HEAD_DIM
SSD_GROUPS = 4
SSD_STATE = 128
SSD_CONV = 4
SSD_CONV_CH = SSD_WIDTH + 2 * SSD_GROUPS * SSD_STATE
ATT_WIDTH = D_MIX // 4
HEAD_DIM = 64
N_Q_HEADS = ATT_WIDTH // HEAD_DIM
N_KV_HEADS = 4
KV_WIDTH = N_KV_HEADS * HEAD_DIM
WINDOW = 128
ROPE_THETA = 10000.0
SC_WIDTH = D_MIX // 4
SC_CONV = 3
PLE_DIM = 256
EPS = 1e-6
PAST_LEN = 8192

LANES = 128
SUBLANES = 8
CHUNK = 128
GROUP_W = SSD_WIDTH // SSD_GROUPS
HEADS_PER_GROUP = SSD_HEADS // SSD_GROUPS

COL_Z = 0
COL_X = 2048
COL_B = 4096
COL_C = 4608
COL_Q = 5120
COL_GATT = 6144
COL_BSC = 7168
COL_CSC = 8192
COL_HSC = 9216
COL_GSC = 10240
COL_K = 11264
COL_V = 11520
COL_DT = 11776
N_PROJ = 12288

VMEM_LIMIT = 56 * 1024 * 1024


def _cparams(ndims):
    return pltpu.CompilerParams(dimension_semantics=("arbitrary",) * ndims, vmem_limit_bytes=VMEM_LIMIT)


def _silu(x):
    h = 0.5 * x
    return h + h * jnp.tanh(h)


def _sigmoid(x):
    return 0.5 + 0.5 * jnp.tanh(0.5 * x)


def _inproj_kernel(x_ref, nw_ref, w_ref, o_ref, h_ref, *, tm, row_chunk):
    @pl.when(pl.program_id(1) == 0)
    def _():
        def body(r, carry):
            rows = pl.ds(pl.multiple_of(r * row_chunk, row_chunk), row_chunk)
            x = x_ref[rows, :]
            ms = jnp.mean(x * x, axis=-1, keepdims=True)
            h_ref[rows, :] = ((x * lax.rsqrt(ms + EPS)) * nw_ref[...]).astype(BF16)
            return carry

        lax.fori_loop(0, tm // row_chunk, body, 0)

    o_ref[...] = jnp.dot(h_ref[...], w_ref[...], preferred_element_type=F32)


def _in_proj(x2d, nw, w_bf16):
    t = x2d.shape[0]
    tm = min(1024, t)
    tn = 1024
    return pl.pallas_call(
        functools.partial(_inproj_kernel, tm=tm, row_chunk=128),
        grid=(t // tm, N_PROJ // tn),
        in_specs=[
            pl.BlockSpec((tm, D_MODEL), lambda i, j: (i, 0)),
            pl.BlockSpec((1, D_MODEL), lambda i, j: (0, 0)),
            pl.BlockSpec((D_MODEL, tn), lambda i, j: (0, j)),
        ],
        out_specs=pl.BlockSpec((tm, tn), lambda i, j: (i, j)),
        out_shape=jax.ShapeDtypeStruct((t, N_PROJ), F32),
        scratch_shapes=[pltpu.VMEM((tm, D_MODEL), BF16)],
        compiler_params=_cparams(2),
        name="in_proj",
    )(x2d, nw, w_bf16)


DEC_G = 4
DEC_TILE = SUBLANES


def _softplus(x):
    return jnp.maximum(x, 0.0) + jnp.log1p(jnp.exp(-jnp.abs(x)))


def _ssd_conv_silu(xpad, cw_ref, cbias_ref, xs_scr, bm_scr, cm_scr):
    cwid = 256
    base = SUBLANES - (SSD_CONV - 1)
    for cb in range(SSD_CONV_CH // cwid):
        cols = slice(cb * cwid, (cb + 1) * cwid)
        acc = xpad[base:base + CHUNK, cols] * cw_ref[0:1, cols]
        for j in range(1, SSD_CONV):
            acc = acc + xpad[base + j:base + j + CHUNK, cols] * cw_ref[j:j + 1, cols]
        act = _silu(acc + cbias_ref[:, cols])
        lo = cb * cwid
        if lo < SSD_WIDTH:
            xs_scr[:, lo:lo + cwid] = act
        elif lo < SSD_WIDTH + GROUP_W:
            bm_scr[:, lo - SSD_WIDTH:lo - SSD_WIDTH + cwid] = act.astype(BF16)
        else:
            o = lo - SSD_WIDTH - GROUP_W
            cm_scr[:, o:o + cwid] = act.astype(BF16)


def _ssd_expand(v, e_mat):
    hi = v.astype(BF16)
    lo = (v - hi.astype(F32)).astype(BF16)
    return jnp.dot(hi, e_mat, preferred_element_type=F32) + jnp.dot(lo, e_mat, preferred_element_type=F32)


def _ssd_diag(a_cum, dt, allowed, xs_scr, bm_scr, cm_scr, y_scr, after_group=None):
    Q = CHUNK
    a_cum_t = a_cum.T
    dt_t = dt.T
    lane = lax.broadcasted_iota(jnp.int32, (Q, LANES), 1)
    low_half = lane < SSD_HEAD_DIM
    for g in range(SSD_GROUPS):
        gcols = slice(g * SSD_STATE, (g + 1) * SSD_STATE)
        cbm = lax.dot_general(cm_scr[:, gcols], bm_scr[:, gcols], (((1,), (1,)), ((), ())),
                              preferred_element_type=F32)
        for p in range(HEADS_PER_GROUP // 2):
            pair = g * (HEADS_PER_GROUP // 2) + p
            ms = []
            for h in (2 * pair, 2 * pair + 1):
                seg = jnp.broadcast_to(a_cum[:, h:h + 1], (Q, Q)) - a_cum_t[h:h + 1, :]
                dec = jnp.exp(jnp.where(allowed, seg, -jnp.inf))
                ms.append((cbm * dec * dt_t[h:h + 1, :]).astype(BF16))
            lhs = jnp.concatenate(ms, axis=1)
            xp = xs_scr[:, pair * LANES:(pair + 1) * LANES]
            rhs = jnp.concatenate([jnp.where(low_half, xp, 0.0).astype(BF16),
                                   jnp.where(low_half, 0.0, xp).astype(BF16)], axis=0)
            y_scr[:, pair * LANES:(pair + 1) * LANES] = jnp.dot(lhs, rhs, preferred_element_type=F32)
        if after_group is not None:
            after_group(g)


def _ssd_gate_norm(yg, zg, nw):
    yg = yg * _silu(zg)
    msq = jnp.mean(yg * yg, axis=-1, keepdims=True)
    return (yg * lax.rsqrt(msq + EPS)) * nw


def _ssd_decode_kernel(*refs, L, n_alias):
    Q = CHUNK
    n_in = 14
    refs = refs[:n_in] + refs[n_in + n_alias:]
    (z_ref, x_ref, b_ref, c_ref, dt_ref, cs_ref, ss_ref, cw_ref, cbias_ref, alog_ref, dtb_ref, dsk_ref,
     nw_ref, e_ref, y_ref, cso_ref, sso_ref,
     xpad, xs_scr, bm_scr, cm_scr, y_scr, ea_scr, we_scr, zp_scr, dtp_scr) = refs
    t0 = DEC_TILE - L
    used = DEC_G * DEC_TILE

    xpad[...] = jnp.zeros(xpad.shape, F32)
    for g in range(DEC_G):
        r0 = SUBLANES + DEC_TILE * g
        tok = slice(L * g, L * (g + 1))
        xpad[r0:r0 + DEC_TILE, :] = cs_ref[g]
        xpad[r0 + t0:r0 + DEC_TILE, 0:SSD_WIDTH] = x_ref[tok, :]
        xpad[r0 + t0:r0 + DEC_TILE, SSD_WIDTH:SSD_WIDTH + GROUP_W] = b_ref[tok, :]
        xpad[r0 + t0:r0 + DEC_TILE, SSD_WIDTH + GROUP_W:SSD_CONV_CH] = c_ref[tok, :]
    for g in range(DEC_G):
        r0 = SUBLANES + DEC_TILE * g
        cso_ref[g] = xpad[r0:r0 + DEC_TILE, :]
    _ssd_conv_silu(xpad, cw_ref, cbias_ref, xs_scr, bm_scr, cm_scr)

    row = lax.broadcasted_iota(jnp.int32, (Q, LANES), 0)
    in_tile = row & (DEC_TILE - 1)
    dtp_scr[...] = jnp.zeros((Q, LANES), F32)
    zp_scr[...] = jnp.zeros(zp_scr.shape, F32)
    for g in range(DEC_G):
        dtp_scr[DEC_TILE * g + t0:DEC_TILE * (g + 1), :] = dt_ref[L * g:L * (g + 1), :]
        zp_scr[DEC_TILE * g + t0:DEC_TILE * (g + 1), :] = z_ref[L * g:L * (g + 1), :]
    dt = jnp.where((in_tile >= t0) & (row < used), _softplus(dtp_scr[...] + dtb_ref[...]), 0.0)
    a_cum = dt * (-jnp.exp(alog_ref[...]))
    k = 1
    while k < DEC_TILE:
        a_cum = a_cum + jnp.where(in_tile >= k, pltpu.roll(a_cum, k, 0), 0.0)
        k *= 2
    tot = jnp.where(in_tile == DEC_TILE - 1, a_cum, 0.0)
    k = 1
    while k < DEC_TILE:
        tot = tot + pltpu.roll(tot, Q - k, 0)
        k *= 2
    e_mat = e_ref[...]
    ea_scr[...] = _ssd_expand(jnp.exp(a_cum), e_mat)
    we_scr[...] = _ssd_expand(jnp.exp(tot - a_cum) * dt, e_mat)
    cd_t = jnp.exp(tot).T

    lrow = lax.broadcasted_iota(jnp.int32, (Q, Q), 0)
    scol = lax.broadcasted_iota(jnp.int32, (Q, Q), 1)
    allowed = (lrow >= scol) & ((lrow // DEC_TILE) == (scol // DEC_TILE))
    _ssd_diag(a_cum, dt, allowed, xs_scr, bm_scr, cm_scr, y_scr)

    pair_rows = 2 * DEC_TILE
    first_of_pair = lax.broadcasted_iota(jnp.int32, (pair_rows, GROUP_W), 0) < DEC_TILE
    for g in range(SSD_GROUPS):
        gcols = slice(g * SSD_STATE, (g + 1) * SSD_STATE)
        wcols = slice(g * GROUP_W, (g + 1) * GROUP_W)
        for sp in range(DEC_G // 2):
            rows = slice(sp * pair_rows, (sp + 1) * pair_rows)
            offs = [lax.dot_general(cm_scr[rows, gcols], ss_ref[2 * sp + j, wcols, :].astype(BF16),
                                    (((1,), (1,)), ((), ())), preferred_element_type=F32) for j in range(2)]
            y_off = jnp.where(first_of_pair, offs[0], offs[1])
            y_scr[rows, wcols] = y_scr[rows, wcols] + y_off * ea_scr[rows, wcols]
        yg = y_scr[0:used, wcols] + dsk_ref[:, wcols] * xs_scr[0:used, wcols]
        out = _ssd_gate_norm(yg, zp_scr[:, wcols], nw_ref[:, wcols])
        for s in range(DEC_G):
            y_ref[L * s:L * (s + 1), wcols] = out[DEC_TILE * s + t0:DEC_TILE * (s + 1)].astype(y_ref.dtype)

    row_seq = lax.broadcasted_iota(jnp.int32, (Q, SSD_STATE), 0) // DEC_TILE
    for g in range(SSD_GROUPS):
        wcols = slice(g * GROUP_W, (g + 1) * GROUP_W)
        gcols = slice(g * SSD_STATE, (g + 1) * SSD_STATE)
        xw_t = (xs_scr[:, wcols] * we_scr[:, wcols]).T.astype(BF16)
        bm_g = bm_scr[:, gcols]
        for s in range(DEC_G):
            st = jnp.dot(xw_t, jnp.where(row_seq == s, bm_g, jnp.zeros_like(bm_g)),
                         preferred_element_type=F32)
            last = DEC_TILE * (s + 1) - 1
            cd_rows = jnp.broadcast_to(cd_t[:, last:last + 1], (LANES, LANES))
            for hh in range(HEADS_PER_GROUP):
                h = g * HEADS_PER_GROUP + hh
                r0 = h * SSD_HEAD_DIM
                sso_ref[s, r0:r0 + SSD_HEAD_DIM, :] = (ss_ref[s, r0:r0 + SSD_HEAD_DIM, :] * cd_rows[h:h + 1, :]
                                                       + st[hh * SSD_HEAD_DIM:(hh + 1) * SSD_HEAD_DIM, :])


def _ssd_kernel(*refs, nc, n_alias):
    Q = CHUNK
    n_in = 12
    refs = refs[:n_in] + refs[n_in + n_alias:]
    (z_ref, x_ref, b_ref, c_ref, dt_ref, cw_ref, cbias_ref, alog_ref, dtb_ref, dsk_ref, nw_ref, e_ref,
     y_ref, cso_ref, sso_ref,
     xpad, xs_scr, bm_scr, cm_scr, y_scr, ea_scr, we_scr) = refs
    c = pl.program_id(1)

    @pl.when(c == 0)
    def _init():
        sso_ref[...] = jnp.zeros(sso_ref.shape, F32)
        xpad[0:SUBLANES, :] = jnp.zeros((SUBLANES, SSD_CONV_CH), F32)

    xpad[SUBLANES:SUBLANES + Q, 0:SSD_WIDTH] = x_ref[...]
    xpad[SUBLANES:SUBLANES + Q, SSD_WIDTH:SSD_WIDTH + GROUP_W] = b_ref[...]
    xpad[SUBLANES:SUBLANES + Q, SSD_WIDTH + GROUP_W:SSD_CONV_CH] = c_ref[...]
    _ssd_conv_silu(xpad, cw_ref, cbias_ref, xs_scr, bm_scr, cm_scr)

    tail_tile = xpad[Q:Q + SUBLANES, :]

    @pl.when(c == nc - 1)
    def _():
        cso_ref[...] = tail_tile

    xpad[0:SUBLANES, :] = tail_tile

    row = lax.broadcasted_iota(jnp.int32, (Q, LANES), 0)
    dt = _softplus(dt_ref[...] + dtb_ref[...])
    a_cum = dt * (-jnp.exp(alog_ref[...]))
    k = 1
    while k < Q:
        a_cum = a_cum + jnp.where(row >= k, pltpu.roll(a_cum, k, 0), 0.0)
        k *= 2
    a_last = a_cum[Q - 1:Q, :]
    e_mat = e_ref[...]
    ea_scr[...] = _ssd_expand(jnp.exp(a_cum), e_mat)
    we_scr[...] = _ssd_expand(jnp.exp(a_last - a_cum) * dt, e_mat)
    cd_rows = jnp.broadcast_to(jnp.exp(a_cum.T[:, Q - 1:Q]), (LANES, LANES))

    def carried_state_term(g):
        gcols = slice(g * SSD_STATE, (g + 1) * SSD_STATE)
        wcols = slice(g * GROUP_W, (g + 1) * GROUP_W)
        s_g = sso_ref[wcols, :].astype(BF16)
        y_off = lax.dot_general(cm_scr[:, gcols], s_g, (((1,), (1,)), ((), ())), preferred_element_type=F32)
        y_scr[:, wcols] = y_scr[:, wcols] + y_off * ea_scr[:, wcols] + dsk_ref[:, wcols] * xs_scr[:, wcols]

    causal = lax.broadcasted_iota(jnp.int32, (Q, Q), 0) >= lax.broadcasted_iota(jnp.int32, (Q, Q), 1)
    _ssd_diag(a_cum, dt, causal, xs_scr, bm_scr, cm_scr, y_scr, after_group=carried_state_term)

    for g in range(SSD_GROUPS):
        wcols = slice(g * GROUP_W, (g + 1) * GROUP_W)
        y_ref[:, wcols] = _ssd_gate_norm(y_scr[:, wcols], z_ref[:, wcols], nw_ref[:, wcols]).astype(y_ref.dtype)

    for g in range(SSD_GROUPS):
        wcols = slice(g * GROUP_W, (g + 1) * GROUP_W)
        gcols = slice(g * SSD_STATE, (g + 1) * SSD_STATE)
        xw_t = (xs_scr[:, wcols] * we_scr[:, wcols]).T.astype(BF16)
        st = jnp.dot(xw_t, bm_scr[:, gcols], preferred_element_type=F32)
        for hh in range(HEADS_PER_GROUP):
            h = g * HEADS_PER_GROUP + hh
            r0 = h * SSD_HEAD_DIM
            sso_ref[r0:r0 + SSD_HEAD_DIM, :] = (sso_ref[r0:r0 + SSD_HEAD_DIM, :] * cd_rows[h:h + 1, :]
                                                + st[hh * SSD_HEAD_DIM:(hh + 1) * SSD_HEAD_DIM, :])


def _ssd_specs_common(prm):
    shapes = [(SSD_CONV, SSD_CONV_CH), (1, SSD_CONV_CH), (1, LANES), (1, LANES), (1, SSD_WIDTH), (1, SSD_WIDTH),
              (LANES, SSD_WIDTH)]
    args = [prm["ssd_conv_w"], prm["ssd_conv_b"], prm["a_log"], prm["dt_bias"], prm["d_skip"], prm["ssd_norm"],
            prm["expand"]]
    scratch = [pltpu.VMEM((SUBLANES + CHUNK, SSD_CONV_CH), F32),
               pltpu.VMEM((CHUNK, SSD_WIDTH), F32),
               pltpu.VMEM((CHUNK, GROUP_W), BF16),
               pltpu.VMEM((CHUNK, GROUP_W), BF16),
               pltpu.VMEM((CHUNK, SSD_WIDTH), F32),
               pltpu.VMEM((CHUNK, SSD_WIDTH), F32),
               pltpu.VMEM((CHUNK, SSD_WIDTH), F32)]
    return shapes, args, scratch


def _ssd_mixer(proj, ssm_out_prev, prm, *, layer, depth, batch, seqlen):
    nc = seqlen // CHUNK

    def blk(width, col):
        return pl.BlockSpec((CHUNK, width), lambda b, c: (b * nc + c, col // width))

    def const(shape):
        return pl.BlockSpec(shape, lambda b, c: (0,) * len(shape))

    state_spec = pl.BlockSpec((None, None, SSD_WIDTH, SSD_STATE), lambda b, c: (layer, b, 0, 0))
    shapes, cargs, scratch = _ssd_specs_common(prm)
    in_specs = [blk(SSD_WIDTH, COL_Z), blk(SSD_WIDTH, COL_X), blk(GROUP_W, COL_B), blk(GROUP_W, COL_C),
                blk(LANES, COL_DT)] + [const(s) for s in shapes]
    args = [proj] * 5 + cargs
    aliases = {}
    if ssm_out_prev is not None:
        aliases[len(args)] = 2
        in_specs += [pl.BlockSpec(memory_space=pl.ANY)]
        args += [ssm_out_prev]
    y, cso, sso = pl.pallas_call(
        functools.partial(_ssd_kernel, nc=nc, n_alias=len(aliases)),
        grid=(batch, nc),
        in_specs=in_specs,
        out_specs=[pl.BlockSpec((CHUNK, SSD_WIDTH), lambda b, c: (b * nc + c, 0)),
                   pl.BlockSpec((None, SUBLANES, SSD_CONV_CH), lambda b, c: (b, 0, 0)),
                   state_spec],
        out_shape=[jax.ShapeDtypeStruct((batch * seqlen, D_MIX), BF16),
                   jax.ShapeDtypeStruct((batch, SUBLANES, SSD_CONV_CH), F32),
                   jax.ShapeDtypeStruct((depth, batch, SSD_WIDTH, SSD_STATE), F32)],
        scratch_shapes=scratch,
        input_output_aliases=aliases,
        compiler_params=_cparams(2),
        name="ssd_prompt",
    )(*args)
    return y, cso[:, SUBLANES - (SSD_CONV - 1):, :], sso


def _ssd_decode_mixer(proj, conv_state, ssm_all, ssm_out_prev, prm, *, layer, depth, batch, seqlen):
    t0 = DEC_TILE - seqlen
    hist = SSD_CONV - 1
    conv_pad = jnp.pad(conv_state, ((0, 0), (t0 - hist, DEC_TILE - t0), (0, 0)))
    rows = DEC_G * seqlen

    def blk(width, col):
        return pl.BlockSpec((rows, width), lambda s: (s, col // width))

    def const(shape):
        return pl.BlockSpec(shape, lambda s: (0,) * len(shape))

    state_spec = pl.BlockSpec((None, DEC_G, SSD_WIDTH, SSD_STATE), lambda s: (layer, s, 0, 0))
    shapes, cargs, scratch = _ssd_specs_common(prm)
    in_specs = [blk(SSD_WIDTH, COL_Z), blk(SSD_WIDTH, COL_X), blk(GROUP_W, COL_B), blk(GROUP_W, COL_C),
                blk(LANES, COL_DT),
                pl.BlockSpec((DEC_G, DEC_TILE, SSD_CONV_CH), lambda s: (s, 0, 0)), state_spec]
    in_specs += [const(s) for s in shapes]
    args = [proj] * 5 + [conv_pad, ssm_all] + cargs
    aliases = {}
    if ssm_out_prev is not None:
        aliases[len(args)] = 2
        in_specs += [pl.BlockSpec(memory_space=pl.ANY)]
        args += [ssm_out_prev]
    scratch += [pltpu.VMEM((DEC_G * DEC_TILE, SSD_WIDTH), F32), pltpu.VMEM((CHUNK, LANES), F32)]
    y, cso, sso = pl.pallas_call(
        functools.partial(_ssd_decode_kernel, L=seqlen, n_alias=len(aliases)),
        grid=(batch // DEC_G,),
        in_specs=in_specs,
        out_specs=[pl.BlockSpec((rows, SSD_WIDTH), lambda s: (s, 0)),
                   pl.BlockSpec((DEC_G, DEC_TILE, SSD_CONV_CH), lambda s: (s, 0, 0)),
                   state_spec],
        out_shape=[jax.ShapeDtypeStruct((batch * seqlen, D_MIX), F32),
                   jax.ShapeDtypeStruct((batch, DEC_TILE, SSD_CONV_CH), F32),
                   jax.ShapeDtypeStruct((depth, batch, SSD_WIDTH, SSD_STATE), F32)],
        scratch_shapes=scratch,
        input_output_aliases=aliases,
        compiler_params=_cparams(1),
        name="ssd_decode",
    )(*args)
    return y, cso[:, DEC_TILE - hist:, :], sso


def _att_stack_cols(tq):
    return -(-(N_Q_HEADS // N_KV_HEADS) * tq // LANES) * LANES


def _att_kernel(*refs, L, nb, carry, n_alias):
    TQ = CHUNK if L == CHUNK else SUBLANES
    W = WINDOW
    n_in = 12 if carry else 15
    refs = refs[:n_in] + refs[n_in + n_alias:]
    if carry:
        (q_ref, k_ref, v_ref, g_ref, sb_ref, sc_ref, sh_ref, sg_ref, cos_ref, sin_ref, sinks_ref, scw_ref,
         y_ref, ko_ref, vo_ref, sco_ref,
         kd, vt, xpad2) = refs
    else:
        (q_ref, k_ref, v_ref, g_ref, sb_ref, sc_ref, sh_ref, sg_ref, cos_ref, sin_ref, sinks_ref, scw_ref,
         sch_ref, kp_ref, vp_ref,
         y_ref, ko_ref, vo_ref, sco_ref,
         kd, vt, xpad2, pad_w, pad_kv, kc_scr, vc_scr) = refs
    i = pl.program_id(1)
    padded = L != TQ
    R = _att_stack_cols(TQ)

    def full(ref, scr):
        if not padded:
            return ref[...]
        scr[...] = jnp.zeros(scr.shape, F32)
        scr[0:L, :] = ref[...]
        return scr[...]

    if padded:
        vals = {name: full(ref, pad_w) for name, ref in
                (("q", q_ref), ("g", g_ref), ("sb", sb_ref), ("sc", sc_ref), ("sh", sh_ref), ("sg", sg_ref))}

        def get(name, cols=slice(None)):
            return vals[name][:, cols]
    else:
        srcs = {"q": q_ref, "g": g_ref, "sb": sb_ref, "sc": sc_ref, "sh": sh_ref, "sg": sg_ref}

        def get(name, cols=slice(None)):
            return srcs[name][:, cols]

    lane = lax.broadcasted_iota(jnp.int32, (TQ, LANES), 1)
    first_half = (lane % HEAD_DIM) < (HEAD_DIM // 2)
    cos = cos_ref[...]
    sin = sin_ref[...]

    def rope(x):
        sw = jnp.where(first_half, pltpu.roll(x, LANES - HEAD_DIM // 2, 1), pltpu.roll(x, HEAD_DIM // 2, 1))
        return x * cos + sw * sin

    k_new = full(k_ref, pad_kv if padded else None)
    k_rot = jnp.concatenate([rope(k_new[:, 0:LANES]), rope(k_new[:, LANES:2 * LANES])], axis=1)
    v_new = full(v_ref, pad_kv if padded else None)
    low_w = lax.broadcasted_iota(jnp.int32, (W, LANES), 1) < HEAD_DIM

    def stage_kv(slot, k_blk, v_blk):
        for kvp in range(N_KV_HEADS // 2):
            kpair = k_blk[:, kvp * LANES:(kvp + 1) * LANES]
            kroll = pltpu.roll(kpair, HEAD_DIM, 1)
            kd[slot, 2 * kvp] = jnp.where(low_w, kpair, kroll).astype(BF16)
            kd[slot, 2 * kvp + 1] = jnp.where(low_w, kroll, kpair).astype(BF16)
            vt[slot, kvp] = v_blk[:, kvp * LANES:(kvp + 1) * LANES].T.astype(BF16)

    def to_channel_major(x):
        return jnp.concatenate([x[:, 0:LANES].T, x[:, LANES:2 * LANES].T], axis=0)

    def to_key_major(xt):
        return jnp.concatenate([xt[0:LANES, :].T, xt[LANES:2 * LANES, :].T], axis=1)

    rowk = lax.broadcasted_iota(jnp.int32, (W, KV_WIDTH), 0)
    if carry:
        cur_slot = lax.rem(i, 2)

        @pl.when(i == 0)
        def _():
            kd[1] = jnp.zeros((N_KV_HEADS, W, LANES), BF16)
            vt[1] = jnp.zeros((N_KV_HEADS // 2, LANES, W), BF16)

        stage_kv(cur_slot, k_rot, v_new)

        @pl.when(i == nb - 1)
        def _():
            ko_ref[...] = to_channel_major(k_rot)
            vo_ref[...] = to_channel_major(v_new)
    else:
        k_prev = to_key_major(kp_ref[...])
        v_prev = to_key_major(vp_ref[...])
        kc_scr[...] = jnp.zeros((W, KV_WIDTH), F32)
        kc_scr[0:TQ, :] = k_rot
        vc_scr[...] = jnp.zeros((W, KV_WIDTH), F32)
        vc_scr[0:TQ, :] = v_new
        k_cur = kc_scr[...]
        v_cur = vc_scr[...]
        stage_kv(0, k_prev, v_prev)
        stage_kv(1, k_cur, v_cur)
        ko_ref[...] = to_channel_major(pltpu.roll(jnp.where(rowk < L, k_cur, k_prev), W - L, 0))
        vo_ref[...] = to_channel_major(pltpu.roll(jnp.where(rowk < L, v_cur, v_prev), W - L, 0))

    s_idx = lax.broadcasted_iota(jnp.int32, (W, R), 0)
    t_idx = lax.broadcasted_iota(jnp.int32, (W, R), 1) & (TQ - 1)
    cur_mask = s_idx <= t_idx
    prev_mask = s_idx > t_idx
    if carry:
        prev_mask = prev_mask & (i > 0)
        slot0_cur = cur_slot == 0
        masks = ((cur_mask & slot0_cur) | (prev_mask & jnp.logical_not(slot0_cur)),
                 (prev_mask & slot0_cur) | (cur_mask & jnp.logical_not(slot0_cur)))
    else:
        masks = (prev_mask, cur_mask)
    low = lane < HEAD_DIM
    heads_per_kv = N_Q_HEADS // N_KV_HEADS

    for kvh in range(N_KV_HEADS):
        kvp, half = divmod(kvh, 2)
        qs = []
        for qq in range(heads_per_kv // 2):
            qp = kvh * (heads_per_kv // 2) + qq
            q_rot = rope(get("q", slice(qp * LANES, (qp + 1) * LANES))) * (HEAD_DIM ** -0.5)
            qs += [jnp.where(low, q_rot, 0.0), jnp.where(low, 0.0, q_rot)]
        if heads_per_kv * TQ < R:
            qs.append(jnp.zeros((R - heads_per_kv * TQ, LANES), F32))
        q_stack = jnp.concatenate(qs, axis=0).astype(BF16)
        s = [jnp.where(masks[slot],
                       lax.dot_general(kd[slot, kvh], q_stack, (((1,), (1,)), ((), ())),
                                       preferred_element_type=F32), -jnp.inf) for slot in range(2)]
        sink = sinks_ref[kvh:kvh + 1, :]
        m = jnp.maximum(jnp.maximum(jnp.max(s[0], axis=0, keepdims=True), jnp.max(s[1], axis=0, keepdims=True)),
                        sink)
        p_un = [jnp.exp(s[slot] - m) for slot in range(2)]
        den = (jnp.sum(p_un[0], axis=0, keepdims=True) + jnp.sum(p_un[1], axis=0, keepdims=True)
               + jnp.exp(sink - m))
        o_t = (jnp.dot(vt[0, kvp], p_un[0].astype(BF16), preferred_element_type=F32)
               + jnp.dot(vt[1, kvp], p_un[1].astype(BF16), preferred_element_type=F32)) * (1.0 / den)
        for qq in range(heads_per_kv // 2):
            qp = kvh * (heads_per_kv // 2) + qq
            qcols = slice(qp * LANES, (qp + 1) * LANES)
            if carry:
                rows = slice(half * HEAD_DIM, (half + 1) * HEAD_DIM)
                o = jnp.concatenate([o_t[rows, (2 * qq) * TQ:(2 * qq + 1) * TQ],
                                     o_t[rows, (2 * qq + 1) * TQ:(2 * qq + 2) * TQ]], axis=0).T
            else:
                if qq == 0:
                    o_all = o_t.T
                oa = o_all[(2 * qq) * TQ:(2 * qq + 1) * TQ, :]
                ob = o_all[(2 * qq + 1) * TQ:(2 * qq + 2) * TQ, :]
                if half == 0:
                    ob = pltpu.roll(ob, HEAD_DIM, 1)
                else:
                    oa = pltpu.roll(oa, HEAD_DIM, 1)
                o = jnp.where(low, oa, ob)
            y_ref[:, qcols] = (o * _silu(get("g", qcols)))[0:L].astype(y_ref.dtype)

    @pl.when(i == 0)
    def _():
        if carry:
            xpad2[0:SUBLANES, :] = jnp.zeros((SUBLANES, SC_WIDTH), F32)
        else:
            xpad2[0:SUBLANES, :] = sch_ref[...]

    u = get("sc") * get("sh")
    xpad2[SUBLANES:SUBLANES + TQ, :] = u
    base = SUBLANES - (SC_CONV - 1)
    conv = xpad2[base:base + TQ, :] * scw_ref[0:1, :]
    for j in range(1, SC_CONV):
        conv = conv + xpad2[base + j:base + j + TQ, :] * scw_ref[j:j + 1, :]
    ysc = (get("sb") * conv) * _silu(get("sg"))
    y_ref[:, ATT_WIDTH:ATT_WIDTH + SC_WIDTH] = ysc[0:L].astype(y_ref.dtype)

    tail = (base + L) // SUBLANES * SUBLANES
    tail_tile = xpad2[tail:tail + SUBLANES, :]

    @pl.when(i == nb - 1)
    def _():
        sco_ref[...] = tail_tile

    if carry:
        xpad2[0:SUBLANES, :] = tail_tile


def _att_mixer(proj, y_mix, sc_state_pad, k_all_prev, v_all_prev, kv_out_prev, prm, rope_tab, *, layer, depth,
               batch, seqlen):
    carry = sc_state_pad is None
    if seqlen >= CHUNK:
        L, nb = CHUNK, seqlen // CHUNK
        src = proj

        def blk(width, col):
            return pl.BlockSpec((L, width), lambda b, i: (b * nb + i, col // width))

        y_spec = pl.BlockSpec((L, ATT_WIDTH + SC_WIDTH), lambda b, i: (b * nb + i, 1))
        tq = CHUNK
    else:
        L, nb = seqlen, 1
        src = proj

        def blk(width, col):
            return pl.BlockSpec((None, L, width), lambda b, i: (b, 0, col // width))

        y_spec = pl.BlockSpec((None, L, ATT_WIDTH + SC_WIDTH), lambda b, i: (b, 0, 1))
        tq = SUBLANES

    def const(shape):
        return pl.BlockSpec(shape, lambda b, i: (0,) * len(shape))

    cos_t, sin_t = rope_tab
    n_cols = _att_stack_cols(tq)
    sink_rows = jnp.repeat(prm["sinks"].reshape(N_KV_HEADS, N_Q_HEADS // N_KV_HEADS), tq, axis=1)
    sink_rows = jnp.pad(sink_rows, ((0, 0), (0, n_cols - sink_rows.shape[1])))
    in_specs = [blk(ATT_WIDTH, COL_Q), blk(KV_WIDTH, COL_K), blk(KV_WIDTH, COL_V), blk(ATT_WIDTH, COL_GATT),
                blk(SC_WIDTH, COL_BSC), blk(SC_WIDTH, COL_CSC), blk(SC_WIDTH, COL_HSC), blk(SC_WIDTH, COL_GSC),
                pl.BlockSpec((tq, LANES), lambda b, i: (i, 0)), pl.BlockSpec((tq, LANES), lambda b, i: (i, 0)),
                const((N_KV_HEADS, n_cols)), const((SC_CONV, SC_WIDTH))]
    args = [src] * 8 + [cos_t, sin_t, sink_rows, prm["sc_conv_w"]]
    scratch = [pltpu.VMEM((2, N_KV_HEADS, WINDOW, LANES), BF16),
               pltpu.VMEM((2, N_KV_HEADS // 2, LANES, WINDOW), BF16)]
    kv_spec = pl.BlockSpec((None, None, KV_WIDTH, WINDOW), lambda b, i: (layer, b, 0, 0))
    if not carry:
        in_specs += [pl.BlockSpec((None, SUBLANES, SC_WIDTH), lambda b, i: (b, 0, 0)), kv_spec, kv_spec]
        args += [sc_state_pad, k_all_prev, v_all_prev]
    scratch += [pltpu.VMEM((SUBLANES + tq, SC_WIDTH), F32)]
    if not carry:
        scratch += [pltpu.VMEM((tq, ATT_WIDTH), F32), pltpu.VMEM((tq, KV_WIDTH), F32),
                    pltpu.VMEM((WINDOW, KV_WIDTH), F32), pltpu.VMEM((WINDOW, KV_WIDTH), F32)]
    aliases = {len(args): 0}
    in_specs += [pl.BlockSpec(memory_space=pl.ANY)]
    args += [y_mix]
    if kv_out_prev is not None:
        for out_idx, buf in zip((1, 2), kv_out_prev):
            aliases[len(args)] = out_idx
            in_specs += [pl.BlockSpec(memory_space=pl.ANY)]
            args += [buf]
    y, ko, vo, sco = pl.pallas_call(
        functools.partial(_att_kernel, L=L, nb=nb, carry=carry, n_alias=len(aliases)),
        grid=(batch, nb),
        in_specs=in_specs,
        out_specs=[y_spec, kv_spec, kv_spec,
                   pl.BlockSpec((None, SUBLANES, SC_WIDTH), lambda b, i: (b, 0, 0))],
        out_shape=[jax.ShapeDtypeStruct(y_mix.shape, y_mix.dtype),
                   jax.ShapeDtypeStruct((depth, batch, KV_WIDTH, WINDOW), F32),
                   jax.ShapeDtypeStruct((depth, batch, KV_WIDTH, WINDOW), F32),
                   jax.ShapeDtypeStruct((batch, SUBLANES, SC_WIDTH), F32)],
        scratch_shapes=scratch,
        input_output_aliases=aliases,
        compiler_params=_cparams(2),
        name="att_prompt" if carry else "att_decode",
    )(*args)
    base = SUBLANES - (SC_CONV - 1)
    off = base + L - (base + L) // SUBLANES * SUBLANES
    return y, (ko, vo), sco[:, off:off + SC_CONV - 1, :]


OUT_TM = 1024
OUT_KT = 1024
OUT_NT = 512
N_KSTEPS = D_MIX // OUT_KT
N_NSTEPS = D_MODEL // OUT_NT


def _outproj_kernel(y_ref, wo_ref, x_ref, npost_ref, p_ref, pp_ref, gw_ref, o_ref, acc_ref, xnb_ref):
    k = pl.program_id(1)

    @pl.when(k == 0)
    def _():
        acc_ref[...] = jnp.dot(y_ref[...].astype(BF16), wo_ref[...], preferred_element_type=F32)

    @pl.when((k > 0) & (k < N_KSTEPS))
    def _():
        acc_ref[...] += jnp.dot(y_ref[...].astype(BF16), wo_ref[...], preferred_element_type=F32)

    @pl.when(k == N_KSTEPS - 1)
    def _():
        rows = min(256, acc_ref.shape[0])
        for r in range(acc_ref.shape[0] // rows):
            rs = slice(r * rows, (r + 1) * rows)
            mix = acc_ref[rs, :]
            ms = jnp.mean(mix * mix, axis=-1, keepdims=True)
            xn = x_ref[rs, :] + (mix * lax.rsqrt(ms + EPS)) * npost_ref[...]
            acc_ref[rs, :] = xn
            xnb_ref[rs, :] = xn.astype(BF16)

    for j in range(N_NSTEPS):
        @pl.when(k == N_KSTEPS + j)
        def _():
            gate = _sigmoid(jnp.dot(xnb_ref[...], gw_ref[...], preferred_element_type=F32))
            e = jnp.dot(p_ref[...].astype(BF16), pp_ref[...], preferred_element_type=F32)
            o_ref[...] = acc_ref[:, j * OUT_NT:(j + 1) * OUT_NT] + gate * e


def _out_proj(y_mix, x2d, p_all, prm, *, layer):
    t = x2d.shape[0]
    tm = min(OUT_TM, t)

    def nstep(k):
        return jnp.maximum(k - N_KSTEPS, 0)

    n_row_tiles = t // tm

    def row_tile(i, k, first_unused):
        return jnp.minimum(i + (k >= first_unused).astype(jnp.int32), n_row_tiles - 1)

    return pl.pallas_call(
        _outproj_kernel,
        grid=(n_row_tiles, N_KSTEPS + N_NSTEPS),
        in_specs=[
            pl.BlockSpec((tm, OUT_KT), lambda i, k: (row_tile(i, k, N_KSTEPS + 1),
                                                     jnp.where(k > N_KSTEPS, 0, jnp.minimum(k, N_KSTEPS - 1)))),
            pl.BlockSpec((OUT_KT, D_MODEL),
                         lambda i, k: (jnp.where(k > N_KSTEPS + 1, 0, jnp.minimum(k, N_KSTEPS - 1)), 0)),
            pl.BlockSpec((tm, D_MODEL), lambda i, k: (row_tile(i, k, N_KSTEPS), 0)),
            pl.BlockSpec((1, D_MODEL), lambda i, k: (0, 0)),
            pl.BlockSpec((None, tm, PLE_DIM), lambda i, k: (layer, i, 0)),
            pl.BlockSpec((PLE_DIM, OUT_NT), lambda i, k: (0, nstep(k))),
            pl.BlockSpec((D_MODEL, OUT_NT), lambda i, k: (0, nstep(k))),
        ],
        out_specs=pl.BlockSpec((tm, OUT_NT), lambda i, k: (i, nstep(k))),
        out_shape=jax.ShapeDtypeStruct((t, D_MODEL), F32),
        scratch_shapes=[pltpu.VMEM((tm, D_MODEL), F32),
                        pltpu.VMEM((tm, D_MODEL), BF16)],
        compiler_params=_cparams(2),
        name="out_proj",
    )(y_mix.reshape(t, D_MIX), prm["w_out"], x2d, prm["norm_post"], p_all, prm["ple_proj"], prm["ple_gate"])


_W_IN_SRC = {}
_o = 0
for _name, _size in (("z", SSD_WIDTH), ("xbc", SSD_CONV_CH), ("dt", SSD_HEADS), ("q", ATT_WIDTH), ("k", KV_WIDTH),
                     ("v", KV_WIDTH), ("g_att", ATT_WIDTH), ("b_sc", SC_WIDTH), ("c_sc", SC_WIDTH),
                     ("h_sc", SC_WIDTH), ("g_sc", SC_WIDTH)):
    _W_IN_SRC[_name] = (_o, _size)
    _o += _size
N_IN = _o
_W_IN_DST = {"z": COL_Z, "xbc": COL_X, "q": COL_Q, "g_att": COL_GATT, "b_sc": COL_BSC, "c_sc": COL_CSC,
             "h_sc": COL_HSC, "g_sc": COL_GSC, "k": COL_K, "v": COL_V}
REGROUP_COLS = 256
REGROUP_UNIT = 32


def _regroup_tables(layer):
    starts, valid = [], []
    for j in range(N_PROJ // REGROUP_COLS):
        dst = j * REGROUP_COLS
        src, n = 0, 0
        for name, seg_dst in _W_IN_DST.items():
            seg_src, width = _W_IN_SRC[name]
            if seg_dst <= dst < seg_dst + width:
                src, n = seg_src + dst - seg_dst, REGROUP_COLS
        if dst == COL_DT:
            src, n = _W_IN_SRC["dt"]
        assert src % REGROUP_UNIT == 0 and src + REGROUP_COLS <= N_IN
        starts.append((layer * N_IN + src) // REGROUP_UNIT)
        valid.append(n)
    return jnp.asarray(starts, jnp.int32), jnp.asarray(valid, jnp.int32)


def _regroup_kernel(starts_ref, valid_ref, w_ref, o_ref):
    col = lax.broadcasted_iota(jnp.int32, (D_MODEL, REGROUP_COLS), 1)
    o_ref[...] = jnp.where(col < valid_ref[pl.program_id(0)], w_ref[...].T, 0.0).astype(BF16)


def _regroup_w_in(w_cols, layer):
    starts, valid = _regroup_tables(layer)
    return pl.pallas_call(
        _regroup_kernel,
        grid_spec=pltpu.PrefetchScalarGridSpec(
            num_scalar_prefetch=2,
            grid=(N_PROJ // REGROUP_COLS,),
            in_specs=[pl.BlockSpec((pl.Element(REGROUP_COLS), pl.Element(D_MODEL)),
                                   lambda j, s, v: (s[j] * REGROUP_UNIT, 0))],
            out_specs=pl.BlockSpec((D_MODEL, REGROUP_COLS), lambda j, s, v: (0, j))),
        out_shape=jax.ShapeDtypeStruct((D_MODEL, N_PROJ), BF16),
        compiler_params=_cparams(1),
        name="regroup_w_in",
    )(starts, valid, w_cols)


def _lane_pad(v):
    return jnp.pad(v, (0, LANES - v.shape[0])).reshape(1, LANES)


def _layer_params(i, w_in, w_out, norm_pre, norm_post, ssd_conv_w, ssd_conv_b, ssd_a_log, ssd_dt_bias, ssd_d,
                  ssd_norm, attn_sinks, sc_conv_w, ple_proj, ple_gate, expand):
    return {
        "w_in": _regroup_w_in(w_in, i),
        "w_out": w_out[i].astype(BF16),
        "norm_pre": norm_pre[i].reshape(1, D_MODEL),
        "norm_post": norm_post[i].reshape(1, D_MODEL),
        "ssd_conv_w": ssd_conv_w[i],
        "ssd_conv_b": ssd_conv_b[i].reshape(1, SSD_CONV_CH),
        "a_log": _lane_pad(ssd_a_log[i]),
        "dt_bias": _lane_pad(ssd_dt_bias[i]),
        "d_skip": jnp.repeat(ssd_d[i], SSD_HEAD_DIM).reshape(1, SSD_WIDTH),
        "ssd_norm": ssd_norm[i].reshape(1, SSD_WIDTH),
        "sinks": attn_sinks[i],
        "sc_conv_w": sc_conv_w[i],
        "ple_proj": ple_proj[i].astype(BF16),
        "ple_gate": ple_gate[i].astype(BF16),
        "expand": expand,
    }


def _rope_tables(pos0, n):
    half = HEAD_DIM // 2
    inv_freq = jnp.exp(-math.log(ROPE_THETA) * jnp.arange(half, dtype=F32) * (2.0 / HEAD_DIM))
    pos = pos0 + jnp.arange(n, dtype=jnp.int32)
    ang = pos.astype(F32)[:, None] * inv_freq[None, :]
    cos, sin = jnp.cos(ang), jnp.sin(ang)
    reps = LANES // HEAD_DIM
    return (jnp.tile(jnp.concatenate([cos, cos], axis=1), (1, reps)),
            jnp.tile(jnp.concatenate([-sin, sin], axis=1), (1, reps)))


def _layer(x2d, p_all, prm, rope_tab, carried, *, layer, depth, batch, seqlen, conv_state, ssm_all, k_all, v_all,
           sc_state):
    decode = conv_state is not None
    ssm_out_prev, kv_out_prev = carried
    proj = _in_proj(x2d, prm["norm_pre"], prm["w_in"])
    if decode:
        sc_pad = jnp.pad(sc_state, ((0, 0), (SUBLANES - (SC_CONV - 1), 0), (0, 0)))
        y_mix, conv_new, ssm_out = _ssd_decode_mixer(proj, conv_state, ssm_all, ssm_out_prev, prm, layer=layer,
                                                     depth=depth, batch=batch, seqlen=seqlen)
        proj = proj.reshape(batch, seqlen, N_PROJ)
        y_mix = y_mix.reshape(batch, seqlen, D_MIX)
    else:
        sc_pad = None
        y_mix, conv_new, ssm_out = _ssd_mixer(proj, ssm_out_prev, prm, layer=layer, depth=depth, batch=batch,
                                              seqlen=seqlen)
    y_mix, kv_out, sc_new = _att_mixer(proj, y_mix, sc_pad, k_all, v_all, kv_out_prev, prm, rope_tab, layer=layer,
                                       depth=depth, batch=batch, seqlen=seqlen)
    x_new = _out_proj(y_mix, x2d, p_all, prm, layer=layer)
    return x_new, conv_new, sc_new, (ssm_out, kv_out)


def kernel(x_prompt, x_sample, p_prompt, p_sample, state_ssd_conv, state_ssm, cache_k, cache_v, state_sc_conv,
           w_in, w_out, norm_pre, norm_post, ssd_conv_w, ssd_conv_b, ssd_a_log, ssd_dt_bias, ssd_d, ssd_norm,
           attn_sinks, sc_conv_w, ple_proj, ple_gate):
    bp, lp, _ = x_prompt.shape
    bs, ls, _ = x_sample.shape
    depth = w_in.shape[0]
    assert lp % CHUNK == 0 and ls + SSD_CONV - 1 <= DEC_TILE and bs % DEC_G == 0 and cache_k.shape[2] == WINDOW
    head = jnp.arange(LANES, dtype=jnp.int32)[:, None]
    chan = jnp.arange(SSD_WIDTH, dtype=jnp.int32)[None, :]
    expand = (chan // SSD_HEAD_DIM == head).astype(BF16)
    rope_p = _rope_tables(0, lp)
    rope_s = _rope_tables(PAST_LEN, SUBLANES)
    yp = x_prompt.reshape(bp * lp, D_MODEL)
    ys = x_sample.reshape(bs * ls, D_MODEL)
    pp_all = p_prompt.reshape(depth, bp * lp, PLE_DIM)
    ps_all = p_sample.reshape(depth, bs * ls, PLE_DIM)
    ssm_all = state_ssm.reshape(depth, bs, SSD_WIDTH, SSD_STATE)
    w_cols = jnp.swapaxes(w_in, 1, 2).reshape(depth * N_IN, D_MODEL)
    k_all = cache_k.transpose(0, 1, 3, 4, 2).reshape(depth, bs, KV_WIDTH, WINDOW)
    v_all = cache_v.transpose(0, 1, 3, 4, 2).reshape(depth, bs, KV_WIDTH, WINDOW)
    carried_p = carried_s = (None, None)
    conv_p, sc_p, conv_s, sc_s = [], [], [], []
    for i in range(depth):
        prm = _layer_params(i, w_cols, w_out, norm_pre, norm_post, ssd_conv_w, ssd_conv_b, ssd_a_log, ssd_dt_bias,
                            ssd_d, ssd_norm, attn_sinks, sc_conv_w, ple_proj, ple_gate, expand)
        yp, cv, sc, carried_p = _layer(yp, pp_all, prm, rope_p, carried_p, layer=i, depth=depth, batch=bp, seqlen=lp,
                                       conv_state=None, ssm_all=None, k_all=None, v_all=None, sc_state=None)
        conv_p.append(cv)
        sc_p.append(sc)
        ys, cv, sc, carried_s = _layer(ys, ps_all, prm, rope_s, carried_s, layer=i, depth=depth, batch=bs, seqlen=ls,
                                       conv_state=state_ssd_conv[i], ssm_all=ssm_all, k_all=k_all, v_all=v_all,
                                       sc_state=state_sc_conv[i])
        conv_s.append(cv)
        sc_s.append(sc)

    def states(conv, carried, sc, batch):
        ssm_out, (k_out, v_out) = carried
        return (jnp.stack(conv), ssm_out.reshape(depth, batch, SSD_HEADS, SSD_HEAD_DIM, SSD_STATE),
                k_out.reshape(depth, batch, N_KV_HEADS, HEAD_DIM, WINDOW).transpose(0, 1, 4, 2, 3),
                v_out.reshape(depth, batch, N_KV_HEADS, HEAD_DIM, WINDOW).transpose(0, 1, 4, 2, 3), jnp.stack(sc))

    return (yp.reshape(bp, lp, D_MODEL), ys.reshape(bs, ls, D_MODEL),
            *states(conv_p, carried_p, sc_p, bp), *states(conv_s, carried_s, sc_s, bs))
```

```python
import functools
import math

import jax
import jax.numpy as jnp
from jax import lax
from jax.experimental import pallas as pl
from jax.experimental.pallas import tpu as pltpu

F32 = jnp.float32
BF16 = jnp.bfloat16

D_MODEL = 2048
D_MIX = 2 * D_MODEL
SSD_WIDTH = D_MIX // 2
SSD_HEAD_DIM = 64
SSD_HEADS = SSD_WIDTH // SSD_HEAD_DIM
SSD_GROUPS = 4
SSD_STATE = 128
SSD_CONV = 4
SSD_CONV_CH = SSD_WIDTH + 2 * SSD_GROUPS * SSD_STATE
ATT_WIDTH = D_MIX // 4
HEAD_DIM = 64
N_Q_HEADS = ATT_WIDTH // HEAD_DIM
N_KV_HEADS = 4
KV_WIDTH = N_KV_HEADS * HEAD_DIM
WINDOW = 128
ROPE_THETA = 10000.0
SC_WIDTH = D_MIX // 4
SC_CONV = 3
PLE_DIM = 256
EPS = 1e-6
PAST_LEN = 8192

LANES = 128
SUBLANES = 8
CHUNK = 128
GROUP_W = SSD_WIDTH // SSD_GROUPS
HEADS_PER_GROUP = SSD_HEADS // SSD_GROUPS

COL_Z = 0
COL_X = 2048
COL_B = 4096
COL_C = 4608
COL_Q = 5120
COL_GATT = 6144
COL_BSC = 7168
COL_CSC = 8192
COL_HSC = 9216
COL_GSC = 10240
COL_K = 11264
COL_V = 11520
COL_DT = 11776
N_PROJ = 12288

VMEM_LIMIT = 56 * 1024 * 1024


def _cparams(ndims):
    return pltpu.CompilerParams(dimension_semantics=("arbitrary",) * ndims, vmem_limit_bytes=VMEM_LIMIT)


def _silu(x):
    h = 0.5 * x
    return h + h * jnp.tanh(h)


def _sigmoid(x):
    return 0.5 + 0.5 * jnp.tanh(0.5 * x)


def _inproj_kernel(x_ref, nw_ref, w_ref, o_ref, h_ref, *, tm, row_chunk):
    @pl.when(pl.program_id(1) == 0)
    def _():
        def body(r, carry):
            rows = pl.ds(pl.multiple_of(r * row_chunk, row_chunk), row_chunk)
            x = x_ref[rows, :]
            ms = jnp.mean(x * x, axis=-1, keepdims=True)
            h_ref[rows, :] = ((x * lax.rsqrt(ms + EPS)) * nw_ref[...]).astype(BF16)
            return carry

        lax.fori_loop(0, tm // row_chunk, body, 0)

    o_ref[...] = jnp.dot(h_ref[...], w_ref[...], preferred_element_type=F32)


def _in_proj(x2d, nw, w_bf16):
    t = x2d.shape[0]
    tm = min(1024, t)
    tn = 1024
    return pl.pallas_call(
        functools.partial(_inproj_kernel, tm=tm, row_chunk=128),
        grid=(t // tm, N_PROJ // tn),
        in_specs=[
            pl.BlockSpec((tm, D_MODEL), lambda i, j: (i, 0)),
            pl.BlockSpec((1, D_MODEL), lambda i, j: (0, 0)),
            pl.BlockSpec((D_MODEL, tn), lambda i, j: (0, j)),
        ],
        out_specs=pl.BlockSpec((tm, tn), lambda i, j: (i, j)),
        out_shape=jax.ShapeDtypeStruct((t, N_PROJ), F32),
        scratch_shapes=[pltpu.VMEM((tm, D_MODEL), BF16)],
        compiler_params=_cparams(2),
        name="in_proj",
    )(x2d, nw, w_bf16)


DEC_G = 4
DEC_TILE = SUBLANES


def _softplus(x):
    return jnp.maximum(x, 0.0) + jnp.log1p(jnp.exp(-jnp.abs(x)))


def _ssd_conv_silu(xpad, cw_ref, cbias_ref, xs_scr, bm_scr, cm_scr):
    cwid = 256
    base = SUBLANES - (SSD_CONV - 1)
    for cb in range(SSD_CONV_CH // cwid):
        cols = slice(cb * cwid, (cb + 1) * cwid)
        acc = xpad[base:base + CHUNK, cols] * cw_ref[0:1, cols]
        for j in range(1, SSD_CONV):
            acc = acc + xpad[base + j:base + j + CHUNK, cols] * cw_ref[j:j + 1, cols]
        act = _silu(acc + cbias_ref[:, cols])
        lo = cb * cwid
        if lo < SSD_WIDTH:
            xs_scr[:, lo:lo + cwid] = act
        elif lo < SSD_WIDTH + GROUP_W:
            bm_scr[:, lo - SSD_WIDTH:lo - SSD_WIDTH + cwid] = act.astype(BF16)
        else:
            o = lo - SSD_WIDTH - GROUP_W
            cm_scr[:, o:o + cwid] = act.astype(BF16)


def _ssd_expand(v, e_mat):
    hi = v.astype(BF16)
    lo = (v - hi.astype(F32)).astype(BF16)
    return jnp.dot(hi, e_mat, preferred_element_type=F32) + jnp.dot(lo, e_mat, preferred_element_type=F32)


def _ssd_diag(a_cum, dt, allowed, xs_scr, bm_scr, cm_scr, y_scr, after_group=None):
    Q = CHUNK
    a_cum_t = a_cum.T
    dt_t = dt.T
    lane = lax.broadcasted_iota(jnp.int32, (Q, LANES), 1)
    low_half = lane < SSD_HEAD_DIM
    for g in range(SSD_GROUPS):
        gcols = slice(g * SSD_STATE, (g + 1) * SSD_STATE)
        cbm = lax.dot_general(cm_scr[:, gcols], bm_scr[:, gcols], (((1,), (1,)), ((), ())),
                              preferred_element_type=F32)
        for p in range(HEADS_PER_GROUP // 2):
            pair = g * (HEADS_PER_GROUP // 2) + p
            ms = []
            for h in (2 * pair, 2 * pair + 1):
                seg = jnp.broadcast_to(a_cum[:, h:h + 1], (Q, Q)) - a_cum_t[h:h + 1, :]
                dec = jnp.exp(jnp.where(allowed, seg, -jnp.inf))
                ms.append((cbm * dec * dt_t[h:h + 1, :]).astype(BF16))
            lhs = jnp.concatenate(ms, axis=1)
            xp = xs_scr[:, pair * LANES:(pair + 1) * LANES]
            rhs = jnp.concatenate([jnp.where(low_half, xp, 0.0).astype(BF16),
                                   jnp.where(low_half, 0.0, xp).astype(BF16)], axis=0)
            y_scr[:, pair * LANES:(pair + 1) * LANES] = jnp.dot(lhs, rhs, preferred_element_type=F32)
        if after_group is not None:
            after_group(g)


def _ssd_gate_norm(yg, zg, nw):
    yg = yg * _silu(zg)
    msq = jnp.mean(yg * yg, axis=-1, keepdims=True)
    return (yg * lax.rsqrt(msq + EPS)) * nw


def _ssd_decode_kernel(*refs, L, n_alias):
    Q = CHUNK
    n_in = 14
    refs = refs[:n_in] + refs[n_in + n_alias:]
    (z_ref, x_ref, b_ref, c_ref, dt_ref, cs_ref, ss_ref, cw_ref, cbias_ref, alog_ref, dtb_ref, dsk_ref,
     nw_ref, e_ref, y_ref, cso_ref, sso_ref,
     xpad, xs_scr, bm_scr, cm_scr, y_scr, ea_scr, we_scr, zp_scr, dtp_scr) = refs
    t0 = DEC_TILE - L
    used = DEC_G * DEC_TILE

    xpad[...] = jnp.zeros(xpad.shape, F32)
    for g in range(DEC_G):
        r0 = SUBLANES + DEC_TILE * g
        tok = slice(L * g, L * (g + 1))
        xpad[r0:r0 + DEC_TILE, :] = cs_ref[g]
        xpad[r0 + t0:r0 + DEC_TILE, 0:SSD_WIDTH] = x_ref[tok, :]
        xpad[r0 + t0:r0 + DEC_TILE, SSD_WIDTH:SSD_WIDTH + GROUP_W] = b_ref[tok, :]
        xpad[r0 + t0:r0 + DEC_TILE, SSD_WIDTH + GROUP_W:SSD_CONV_CH] = c_ref[tok, :]
    for g in range(DEC_G):
        r0 = SUBLANES + DEC_TILE * g
        cso_ref[g] = xpad[r0:r0 + DEC_TILE, :]
    _ssd_conv_silu(xpad, cw_ref, cbias_ref, xs_scr, bm_scr, cm_scr)

    row = lax.broadcasted_iota(jnp.int32, (Q, LANES), 0)
    in_tile = row & (DEC_TILE - 1)
    dtp_scr[...] = jnp.zeros((Q, LANES), F32)
    zp_scr[...] = jnp.zeros(zp_scr.shape, F32)
    for g in range(DEC_G):
        dtp_scr[DEC_TILE * g + t0:DEC_TILE * (g + 1), :] = dt_ref[L * g:L * (g + 1), :]
        zp_scr[DEC_TILE * g + t0:DEC_TILE * (g + 1), :] = z_ref[L * g:L * (g + 1), :]
    dt = jnp.where((in_tile >= t0) & (row < used), _softplus(dtp_scr[...] + dtb_ref[...]), 0.0)
    a_cum = dt * (-jnp.exp(alog_ref[...]))
    k = 1
    while k < DEC_TILE:
        a_cum = a_cum + jnp.where(in_tile >= k, pltpu.roll(a_cum, k, 0), 0.0)
        k *= 2
    tot = jnp.where(in_tile == DEC_TILE - 1, a_cum, 0.0)
    k = 1
    while k < DEC_TILE:
        tot = tot + pltpu.roll(tot, Q - k, 0)
        k *= 2
    e_mat = e_ref[...]
    ea_scr[...] = _ssd_expand(jnp.exp(a_cum), e_mat)
    we_scr[...] = _ssd_expand(jnp.exp(tot - a_cum) * dt, e_mat)
    cd_t = jnp.exp(tot).T

    lrow = lax.broadcasted_iota(jnp.int32, (Q, Q), 0)
    scol = lax.broadcasted_iota(jnp.int32, (Q, Q), 1)
    allowed = (lrow >= scol) & ((lrow // DEC_TILE) == (scol // DEC_TILE))
    _ssd_diag(a_cum, dt, allowed, xs_scr, bm_scr, cm_scr, y_scr)

    pair_rows = 2 * DEC_TILE
    first_of_pair = lax.broadcasted_iota(jnp.int32, (pair_rows, GROUP_W), 0) < DEC_TILE
    for g in range(SSD_GROUPS):
        gcols = slice(g * SSD_STATE, (g + 1) * SSD_STATE)
        wcols = slice(g * GROUP_W, (g + 1) * GROUP_W)
        for sp in range(DEC_G // 2):
            rows = slice(sp * pair_rows, (sp + 1) * pair_rows)
            offs = [lax.dot_general(cm_scr[rows, gcols], ss_ref[2 * sp + j, wcols, :].astype(BF16),
                                    (((1,), (1,)), ((), ())), preferred_element_type=F32) for j in range(2)]
            y_off = jnp.where(first_of_pair, offs[0], offs[1])
            y_scr[rows, wcols] = y_scr[rows, wcols] + y_off * ea_scr[rows, wcols]
        yg = y_scr[0:used, wcols] + dsk_ref[:, wcols] * xs_scr[0:used, wcols]
        out = _ssd_gate_norm(yg, zp_scr[:, wcols], nw_ref[:, wcols])
        for s in range(DEC_G):
            y_ref[L * s:L * (s + 1), wcols] = out[DEC_TILE * s + t0:DEC_TILE * (s + 1)].astype(y_ref.dtype)

    row_seq = lax.broadcasted_iota(jnp.int32, (Q, SSD_STATE), 0) // DEC_TILE
    for g in range(SSD_GROUPS):
        wcols = slice(g * GROUP_W, (g + 1) * GROUP_W)
        gcols = slice(g * SSD_STATE, (g + 1) * SSD_STATE)
        xw_t = (xs_scr[:, wcols] * we_scr[:, wcols]).T.astype(BF16)
        bm_g = bm_scr[:, gcols]
        for s in range(DEC_G):
            st = jnp.dot(xw_t, jnp.where(row_seq == s, bm_g, jnp.zeros_like(bm_g)),
                         preferred_element_type=F32)
            last = DEC_TILE * (s + 1) - 1
            cd_rows = jnp.broadcast_to(cd_t[:, last:last + 1], (LANES, LANES))
            for hh in range(HEADS_PER_GROUP):
                h = g * HEADS_PER_GROUP + hh
                r0 = h * SSD_HEAD_DIM
                sso_ref[s, r0:r0 + SSD_HEAD_DIM, :] = (ss_ref[s, r0:r0 + SSD_HEAD_DIM, :] * cd_rows[h:h + 1, :]
                                                       + st[hh * SSD_HEAD_DIM:(hh + 1) * SSD_HEAD_DIM, :])


def _ssd_kernel(*refs, nc, n_alias):
    Q = CHUNK
    n_in = 12
    refs = refs[:n_in] + refs[n_in + n_alias:]
    (z_ref, x_ref, b_ref, c_ref, dt_ref, cw_ref, cbias_ref, alog_ref, dtb_ref, dsk_ref, nw_ref, e_ref,
     y_ref, cso_ref, sso_ref,
     xpad, xs_scr, bm_scr, cm_scr, y_scr, ea_scr, we_scr) = refs
    c = pl.program_id(1)

    @pl.when(c == 0)
    def _init():
        sso_ref[...] = jnp.zeros(sso_ref.shape, F32)
        xpad[0:SUBLANES, :] = jnp.zeros((SUBLANES, SSD_CONV_CH), F32)

    xpad[SUBLANES:SUBLANES + Q, 0:SSD_WIDTH] = x_ref[...]
    xpad[SUBLANES:SUBLANES + Q, SSD_WIDTH:SSD_WIDTH + GROUP_W] = b_ref[...]
    xpad[SUBLANES:SUBLANES + Q, SSD_WIDTH + GROUP_W:SSD_CONV_CH] = c_ref[...]
    _ssd_conv_silu(xpad, cw_ref, cbias_ref, xs_scr, bm_scr, cm_scr)

    tail_tile = xpad[Q:Q + SUBLANES, :]

    @pl.when(c == nc - 1)
    def _():
        cso_ref[...] = tail_tile

    xpad[0:SUBLANES, :] = tail_tile

    row = lax.broadcasted_iota(jnp.int32, (Q, LANES), 0)
    dt = _softplus(dt_ref[...] + dtb_ref[...])
    a_cum = dt * (-jnp.exp(alog_ref[...]))
    k = 1
    while k < Q:
        a_cum = a_cum + jnp.where(row >= k, pltpu.roll(a_cum, k, 0), 0.0)
        k *= 2
    a_last = a_cum[Q - 1:Q, :]
    e_mat = e_ref[...]
    ea_scr[...] = _ssd_expand(jnp.exp(a_cum), e_mat)
    we_scr[...] = _ssd_expand(jnp.exp(a_last - a_cum) * dt, e_mat)
    cd_rows = jnp.broadcast_to(jnp.exp(a_cum.T[:, Q - 1:Q]), (LANES, LANES))

    def carried_state_term(g):
        gcols = slice(g * SSD_STATE, (g + 1) * SSD_STATE)
        wcols = slice(g * GROUP_W, (g + 1) * GROUP_W)
        s_g = sso_ref[wcols, :].astype(BF16)
        y_off = lax.dot_general(cm_scr[:, gcols], s_g, (((1,), (1,)), ((), ())), preferred_element_type=F32)
        y_scr[:, wcols] = y_scr[:, wcols] + y_off * ea_scr[:, wcols] + dsk_ref[:, wcols] * xs_scr[:, wcols]

    causal = lax.broadcasted_iota(jnp.int32, (Q, Q), 0) >= lax.broadcasted_iota(jnp.int32, (Q, Q), 1)
    _ssd_diag(a_cum, dt, causal, xs_scr, bm_scr, cm_scr, y_scr, after_group=carried_state_term)

    for g in range(SSD_GROUPS):
        wcols = slice(g * GROUP_W, (g + 1) * GROUP_W)
        y_ref[:, wcols] = _ssd_gate_norm(y_scr[:, wcols], z_ref[:, wcols], nw_ref[:, wcols]).astype(y_ref.dtype)

    for g in range(SSD_GROUPS):
        wcols = slice(g * GROUP_W, (g + 1) * GROUP_W)
        gcols = slice(g * SSD_STATE, (g + 1) * SSD_STATE)
        xw_t = (xs_scr[:, wcols] * we_scr[:, wcols]).T.astype(BF16)
        st = jnp.dot(xw_t, bm_scr[:, gcols], preferred_element_type=F32)
        for hh in range(HEADS_PER_GROUP):
            h = g * HEADS_PER_GROUP + hh
            r0 = h * SSD_HEAD_DIM
            sso_ref[r0:r0 + SSD_HEAD_DIM, :] = (sso_ref[r0:r0 + SSD_HEAD_DIM, :] * cd_rows[h:h + 1, :]
                                                + st[hh * SSD_HEAD_DIM:(hh + 1) * SSD_HEAD_DIM, :])


def _ssd_specs_common(prm):
    shapes = [(SSD_CONV, SSD_CONV_CH), (1, SSD_CONV_CH), (1, LANES), (1, LANES), (1, SSD_WIDTH), (1, SSD_WIDTH),
              (LANES, SSD_WIDTH)]
    args = [prm["ssd_conv_w"], prm["ssd_conv_b"], prm["a_log"], prm["dt_bias"], prm["d_skip"], prm["ssd_norm"],
            prm["expand"]]
    scratch = [pltpu.VMEM((SUBLANES + CHUNK, SSD_CONV_CH), F32),
               pltpu.VMEM((CHUNK, SSD_WIDTH), F32),
               pltpu.VMEM((CHUNK, GROUP_W), BF16),
               pltpu.VMEM((CHUNK, GROUP_W), BF16),
               pltpu.VMEM((CHUNK, SSD_WIDTH), F32),
               pltpu.VMEM((CHUNK, SSD_WIDTH), F32),
               pltpu.VMEM((CHUNK, SSD_WIDTH), F32)]
    return shapes, args, scratch


def _ssd_mixer(proj, ssm_out_prev, prm, *, layer, depth, batch, seqlen):
    nc = seqlen // CHUNK

    def blk(width, col):
        return pl.BlockSpec((CHUNK, width), lambda b, c: (b * nc + c, col // width))

    def const(shape):
        return pl.BlockSpec(shape, lambda b, c: (0,) * len(shape))

    state_spec = pl.BlockSpec((None, None, SSD_WIDTH, SSD_STATE), lambda b, c: (layer, b, 0, 0))
    shapes, cargs, scratch = _ssd_specs_common(prm)
    in_specs = [blk(SSD_WIDTH, COL_Z), blk(SSD_WIDTH, COL_X), blk(GROUP_W, COL_B), blk(GROUP_W, COL_C),
                blk(LANES, COL_DT)] + [const(s) for s in shapes]
    args = [proj] * 5 + cargs
    aliases = {}
    if ssm_out_prev is not None:
        aliases[len(args)] = 2
        in_specs += [pl.BlockSpec(memory_space=pl.ANY)]
        args += [ssm_out_prev]
    y, cso, sso = pl.pallas_call(
        functools.partial(_ssd_kernel, nc=nc, n_alias=len(aliases)),
        grid=(batch, nc),
        in_specs=in_specs,
        out_specs=[pl.BlockSpec((CHUNK, SSD_WIDTH), lambda b, c: (b * nc + c, 0)),
                   pl.BlockSpec((None, SUBLANES, SSD_CONV_CH), lambda b, c: (b, 0, 0)),
                   state_spec],
        out_shape=[jax.ShapeDtypeStruct((batch * seqlen, D_MIX), BF16),
                   jax.ShapeDtypeStruct((batch, SUBLANES, SSD_CONV_CH), F32),
                   jax.ShapeDtypeStruct((depth, batch, SSD_WIDTH, SSD_STATE), F32)],
        scratch_shapes=scratch,
        input_output_aliases=aliases,
        compiler_params=_cparams(2),
        name="ssd_prompt",
    )(*args)
    return y, cso[:, SUBLANES - (SSD_CONV - 1):, :], sso


def _ssd_decode_mixer(proj, conv_state, ssm_all, ssm_out_prev, prm, *, layer, depth, batch, seqlen):
    t0 = DEC_TILE - seqlen
    hist = SSD_CONV - 1
    conv_pad = jnp.pad(conv_state, ((0, 0), (t0 - hist, DEC_TILE - t0), (0, 0)))
    rows = DEC_G * seqlen

    def blk(width, col):
        return pl.BlockSpec((rows, width), lambda s: (s, col // width))

    def const(shape):
        return pl.BlockSpec(shape, lambda s: (0,) * len(shape))

    state_spec = pl.BlockSpec((None, DEC_G, SSD_WIDTH, SSD_STATE), lambda s: (layer, s, 0, 0))
    shapes, cargs, scratch = _ssd_specs_common(prm)
    in_specs = [blk(SSD_WIDTH, COL_Z), blk(SSD_WIDTH, COL_X), blk(GROUP_W, COL_B), blk(GROUP_W, COL_C),
                blk(LANES, COL_DT),
                pl.BlockSpec((DEC_G, DEC_TILE, SSD_CONV_CH), lambda s: (s, 0, 0)), state_spec]
    in_specs += [const(s) for s in shapes]
    args = [proj] * 5 + [conv_pad, ssm_all] + cargs
    aliases = {}
    if ssm_out_prev is not None:
        aliases[len(args)] = 2
        in_specs += [pl.BlockSpec(memory_space=pl.ANY)]
        args += [ssm_out_prev]
    scratch += [pltpu.VMEM((DEC_G * DEC_TILE, SSD_WIDTH), F32), pltpu.VMEM((CHUNK, LANES), F32)]
    y, cso, sso = pl.pallas_call(
        functools.partial(_ssd_decode_kernel, L=seqlen, n_alias=len(aliases)),
        grid=(batch // DEC_G,),
        in_specs=in_specs,
        out_specs=[pl.BlockSpec((rows, SSD_WIDTH), lambda s: (s, 0)),
                   pl.BlockSpec((DEC_G, DEC_TILE, SSD_CONV_CH), lambda s: (s, 0, 0)),
                   state_spec],
        out_shape=[jax.ShapeDtypeStruct((batch * seqlen, D_MIX), F32),
                   jax.ShapeDtypeStruct((batch, DEC_TILE, SSD_CONV_CH), F32),
                   jax.ShapeDtypeStruct((depth, batch, SSD_WIDTH, SSD_STATE), F32)],
        scratch_shapes=scratch,
        input_output_aliases=aliases,
        compiler_params=_cparams(1),
        name="ssd_decode",
    )(*args)
    return y, cso[:, DEC_TILE - hist:, :], sso


def _att_stack_cols(tq):
    return -(-(N_Q_HEADS // N_KV_HEADS) * tq // LANES) * LANES


def _att_kernel(*refs, L, nb, carry, n_alias):
    TQ = CHUNK if L == CHUNK else SUBLANES
    W = WINDOW
    n_in = 12 if carry else 15
    refs = refs[:n_in] + refs[n_in + n_alias:]
    if carry:
        (q_ref, k_ref, v_ref, g_ref, sb_ref, sc_ref, sh_ref, sg_ref, cos_ref, sin_ref, sinks_ref, scw_ref,
         y_ref, ko_ref, vo_ref, sco_ref,
         kd, vt, xpad2) = refs
    else:
        (q_ref, k_ref, v_ref, g_ref, sb_ref, sc_ref, sh_ref, sg_ref, cos_ref, sin_ref, sinks_ref, scw_ref,
         sch_ref, kp_ref, vp_ref,
         y_ref, ko_ref, vo_ref, sco_ref,
         kd, vt, xpad2, pad_w, pad_kv, kc_scr, vc_scr) = refs
    i = pl.program_id(1)
    padded = L != TQ
    parts = TQ // L
    R = _att_stack_cols(TQ)

    def full(ref, scr):
        if not padded:
            return ref[...]
        scr[...] = jnp.zeros(scr.shape, F32)
        for s in range(parts):
            @pl.when(i == s)
            def _():
                scr[0:L, :] = ref[L * s:L * (s + 1), :]
        return scr[...]

    def store_rows(cols, val):
        if not padded:
            y_ref[:, cols] = val.astype(y_ref.dtype)
            return
        for s in range(parts):
            @pl.when(i == s)
            def _():
                y_ref[L * s:L * (s + 1), cols] = val[0:L].astype(y_ref.dtype)

    if padded:
        vals = {name: full(ref, pad_w) for name, ref in
                (("q", q_ref), ("g", g_ref), ("sb", sb_ref), ("sc", sc_ref), ("sh", sh_ref), ("sg", sg_ref))}

        def get(name, cols=slice(None)):
            return vals[name][:, cols]
    else:
        srcs = {"q": q_ref, "g": g_ref, "sb": sb_ref, "sc": sc_ref, "sh": sh_ref, "sg": sg_ref}

        def get(name, cols=slice(None)):
            return srcs[name][:, cols]

    lane = lax.broadcasted_iota(jnp.int32, (TQ, LANES), 1)
    first_half = (lane % HEAD_DIM) < (HEAD_DIM // 2)
    cos = cos_ref[...]
    sin = sin_ref[...]

    def rope(x):
        sw = jnp.where(first_half, pltpu.roll(x, LANES - HEAD_DIM // 2, 1), pltpu.roll(x, HEAD_DIM // 2, 1))
        return x * cos + sw * sin

    k_new = full(k_ref, pad_kv if padded else None)
    k_rot = jnp.concatenate([rope(k_new[:, 0:LANES]), rope(k_new[:, LANES:2 * LANES])], axis=1)
    v_new = full(v_ref, pad_kv if padded else None)
    low_w = lax.broadcasted_iota(jnp.int32, (W, LANES), 1) < HEAD_DIM

    def stage_kv(slot, k_blk, v_blk):
        for kvp in range(N_KV_HEADS // 2):
            kpair = k_blk[:, kvp * LANES:(kvp + 1) * LANES]
            kroll = pltpu.roll(kpair, HEAD_DIM, 1)
            kd[slot, 2 * kvp] = jnp.where(low_w, kpair, kroll).astype(BF16)
            kd[slot, 2 * kvp + 1] = jnp.where(low_w, kroll, kpair).astype(BF16)
            vt[slot, kvp] = v_blk[:, kvp * LANES:(kvp + 1) * LANES].T.astype(BF16)

    def to_channel_major(x):
        return jnp.concatenate([x[:, 0:LANES].T, x[:, LANES:2 * LANES].T], axis=0)

    def to_key_major(xt):
        return jnp.concatenate([xt[0:LANES, :].T, xt[LANES:2 * LANES, :].T], axis=1)

    rowk = lax.broadcasted_iota(jnp.int32, (W, KV_WIDTH), 0)
    if carry:
        cur_slot = lax.rem(i, 2)

        @pl.when(i == 0)
        def _():
            kd[1] = jnp.zeros((N_KV_HEADS, W, LANES), BF16)
            vt[1] = jnp.zeros((N_KV_HEADS // 2, LANES, W), BF16)

        stage_kv(cur_slot, k_rot, v_new)

        @pl.when(i == nb - 1)
        def _():
            ko_ref[...] = to_channel_major(k_rot)
            vo_ref[...] = to_channel_major(v_new)
    else:
        k_prev = to_key_major(kp_ref[...])
        v_prev = to_key_major(vp_ref[...])
        kc_scr[...] = jnp.zeros((W, KV_WIDTH), F32)
        kc_scr[0:TQ, :] = k_rot
        vc_scr[...] = jnp.zeros((W, KV_WIDTH), F32)
        vc_scr[0:TQ, :] = v_new
        k_cur = kc_scr[...]
        v_cur = vc_scr[...]
        stage_kv(0, k_prev, v_prev)
        stage_kv(1, k_cur, v_cur)
        ko_ref[...] = to_channel_major(pltpu.roll(jnp.where(rowk < L, k_cur, k_prev), W - L, 0))
        vo_ref[...] = to_channel_major(pltpu.roll(jnp.where(rowk < L, v_cur, v_prev), W - L, 0))

    s_idx = lax.broadcasted_iota(jnp.int32, (W, R), 0)
    t_idx = lax.broadcasted_iota(jnp.int32, (W, R), 1) & (TQ - 1)
    cur_mask = s_idx <= t_idx
    prev_mask = s_idx > t_idx
    if carry:
        prev_mask = prev_mask & (i > 0)
        slot0_cur = cur_slot == 0
        masks = ((cur_mask & slot0_cur) | (prev_mask & jnp.logical_not(slot0_cur)),
                 (prev_mask & slot0_cur) | (cur_mask & jnp.logical_not(slot0_cur)))
    else:
        masks = (prev_mask, cur_mask)
    low = lane < HEAD_DIM
    heads_per_kv = N_Q_HEADS // N_KV_HEADS

    for kvh in range(N_KV_HEADS):
        kvp, half = divmod(kvh, 2)
        qs = []
        for qq in range(heads_per_kv // 2):
            qp = kvh * (heads_per_kv // 2) + qq
            q_rot = rope(get("q", slice(qp * LANES, (qp + 1) * LANES))) * (HEAD_DIM ** -0.5)
            qs += [jnp.where(low, q_rot, 0.0), jnp.where(low, 0.0, q_rot)]
        if heads_per_kv * TQ < R:
            qs.append(jnp.zeros((R - heads_per_kv * TQ, LANES), F32))
        q_stack = jnp.concatenate(qs, axis=0).astype(BF16)
        s = [jnp.where(masks[slot],
                       lax.dot_general(kd[slot, kvh], q_stack, (((1,), (1,)), ((), ())),
                                       preferred_element_type=F32), -jnp.inf) for slot in range(2)]
        sink = sinks_ref[kvh:kvh + 1, :]
        m = jnp.maximum(jnp.maximum(jnp.max(s[0], axis=0, keepdims=True), jnp.max(s[1], axis=0, keepdims=True)),
                        sink)
        p_un = [jnp.exp(s[slot] - m) for slot in range(2)]
        den = (jnp.sum(p_un[0], axis=0, keepdims=True) + jnp.sum(p_un[1], axis=0, keepdims=True)
               + jnp.exp(sink - m))
        o_t = (jnp.dot(vt[0, kvp], p_un[0].astype(BF16), preferred_element_type=F32)
               + jnp.dot(vt[1, kvp], p_un[1].astype(BF16), preferred_element_type=F32)) * (1.0 / den)
        for qq in range(heads_per_kv // 2):
            qp = kvh * (heads_per_kv // 2) + qq
            qcols = slice(qp * LANES, (qp + 1) * LANES)
            if carry:
                rows = slice(half * HEAD_DIM, (half + 1) * HEAD_DIM)
                o = jnp.concatenate([o_t[rows, (2 * qq) * TQ:(2 * qq + 1) * TQ],
                                     o_t[rows, (2 * qq + 1) * TQ:(2 * qq + 2) * TQ]], axis=0).T
            else:
                if qq == 0:
                    o_all = o_t.T
                oa = o_all[(2 * qq) * TQ:(2 * qq + 1) * TQ, :]
                ob = o_all[(2 * qq + 1) * TQ:(2 * qq + 2) * TQ, :]
                if half == 0:
                    ob = pltpu.roll(ob, HEAD_DIM, 1)
                else:
                    oa = pltpu.roll(oa, HEAD_DIM, 1)
                o = jnp.where(low, oa, ob)
            store_rows(qcols, o * _silu(get("g", qcols)))

    if carry:
        @pl.when(i == 0)
        def _():
            xpad2[0:SUBLANES, :] = jnp.zeros((SUBLANES, SC_WIDTH), F32)
    else:
        xpad2[0:SUBLANES, :] = sch_ref[...]

    u = get("sc") * get("sh")
    xpad2[SUBLANES:SUBLANES + TQ, :] = u
    base = SUBLANES - (SC_CONV - 1)
    conv = xpad2[base:base + TQ, :] * scw_ref[0:1, :]
    for j in range(1, SC_CONV):
        conv = conv + xpad2[base + j:base + j + TQ, :] * scw_ref[j:j + 1, :]
    ysc = (get("sb") * conv) * _silu(get("sg"))
    store_rows(slice(ATT_WIDTH, ATT_WIDTH + SC_WIDTH), ysc)

    tail = (base + L) // SUBLANES * SUBLANES
    tail_tile = xpad2[tail:tail + SUBLANES, :]
    if carry:
        @pl.when(i == nb - 1)
        def _():
            sco_ref[...] = tail_tile

        xpad2[0:SUBLANES, :] = tail_tile
    else:
        sco_ref[...] = tail_tile


def _att_mixer(proj, y_mix, sc_state_pad, k_all_prev, v_all_prev, kv_out_prev, prm, rope_tab, *, layer, depth,
               batch, seqlen):
    carry = sc_state_pad is None
    if seqlen >= CHUNK:
        L, nb = CHUNK, seqlen // CHUNK
        src = proj

        def blk(width, col):
            return pl.BlockSpec((L, width), lambda b, i: (b * nb + i, col // width))

        y_spec = pl.BlockSpec((L, ATT_WIDTH + SC_WIDTH), lambda b, i: (b * nb + i, 1))
        tq = CHUNK
    else:
        L, nb = seqlen, 1
        src = proj
        parts = SUBLANES // L
        assert parts * L == SUBLANES and batch % parts == 0

        def blk(width, col):
            return pl.BlockSpec((SUBLANES, width), lambda b, i: (b, col // width))

        y_spec = pl.BlockSpec((SUBLANES, ATT_WIDTH + SC_WIDTH), lambda b, i: (b, 1))
        tq = SUBLANES

    def const(shape):
        return pl.BlockSpec(shape, lambda b, i: (0,) * len(shape))

    def seq(b, i):
        return b if carry else b * parts + i

    cos_t, sin_t = rope_tab
    n_cols = _att_stack_cols(tq)
    sink_rows = jnp.repeat(prm["sinks"].reshape(N_KV_HEADS, N_Q_HEADS // N_KV_HEADS), tq, axis=1)
    sink_rows = jnp.pad(sink_rows, ((0, 0), (0, n_cols - sink_rows.shape[1])))
    in_specs = [blk(ATT_WIDTH, COL_Q), blk(KV_WIDTH, COL_K), blk(KV_WIDTH, COL_V), blk(ATT_WIDTH, COL_GATT),
                blk(SC_WIDTH, COL_BSC), blk(SC_WIDTH, COL_CSC), blk(SC_WIDTH, COL_HSC), blk(SC_WIDTH, COL_GSC),
                pl.BlockSpec((tq, LANES), lambda b, i: (i if carry else 0, 0)),
                pl.BlockSpec((tq, LANES), lambda b, i: (i if carry else 0, 0)),
                const((N_KV_HEADS, n_cols)), const((SC_CONV, SC_WIDTH))]
    args = [src] * 8 + [cos_t, sin_t, sink_rows, prm["sc_conv_w"]]
    scratch = [pltpu.VMEM((2, N_KV_HEADS, WINDOW, LANES), BF16),
               pltpu.VMEM((2, N_KV_HEADS // 2, LANES, WINDOW), BF16)]
    kv_spec = pl.BlockSpec((None, None, KV_WIDTH, WINDOW), lambda b, i: (layer, seq(b, i), 0, 0))
    sc_spec = pl.BlockSpec((None, SUBLANES, SC_WIDTH), lambda b, i: (seq(b, i), 0, 0))
    if not carry:
        in_specs += [sc_spec, kv_spec, kv_spec]
        args += [sc_state_pad, k_all_prev, v_all_prev]
    scratch += [pltpu.VMEM((SUBLANES + tq, SC_WIDTH), F32)]
    if not carry:
        scratch += [pltpu.VMEM((tq, ATT_WIDTH), F32), pltpu.VMEM((tq, KV_WIDTH), F32),
                    pltpu.VMEM((WINDOW, KV_WIDTH), F32), pltpu.VMEM((WINDOW, KV_WIDTH), F32)]
    aliases = {len(args): 0}
    in_specs += [pl.BlockSpec(memory_space=pl.ANY)]
    args += [y_mix]
    if kv_out_prev is not None:
        for out_idx, buf in zip((1, 2), kv_out_prev):
            aliases[len(args)] = out_idx
            in_specs += [pl.BlockSpec(memory_space=pl.ANY)]
            args += [buf]
    y, ko, vo, sco = pl.pallas_call(
        functools.partial(_att_kernel, L=L, nb=nb, carry=carry, n_alias=len(aliases)),
        grid=(batch, nb) if carry else (batch // parts, parts),
        in_specs=in_specs,
        out_specs=[y_spec, kv_spec, kv_spec, sc_spec],
        out_shape=[jax.ShapeDtypeStruct(y_mix.shape, y_mix.dtype),
                   jax.ShapeDtypeStruct((depth, batch, KV_WIDTH, WINDOW), F32),
                   jax.ShapeDtypeStruct((depth, batch, KV_WIDTH, WINDOW), F32),
                   jax.ShapeDtypeStruct((batch, SUBLANES, SC_WIDTH), F32)],
        scratch_shapes=scratch,
        input_output_aliases=aliases,
        compiler_params=_cparams(2),
        name="att_prompt" if carry else "att_decode",
    )(*args)
    base = SUBLANES - (SC_CONV - 1)
    off = base + L - (base + L) // SUBLANES * SUBLANES
    return y, (ko, vo), sco[:, off:off + SC_CONV - 1, :]


OUT_TM = 1024
OUT_KT = 1024
OUT_NT = 512
N_KSTEPS = D_MIX // OUT_KT
N_NSTEPS = D_MODEL // OUT_NT


def _outproj_kernel(y_ref, wo_ref, x_ref, npost_ref, p_ref, pp_ref, gw_ref, o_ref, acc_ref, xnb_ref):
    k = pl.program_id(1)

    @pl.when(k == 0)
    def _():
        acc_ref[...] = jnp.dot(y_ref[...].astype(BF16), wo_ref[...], preferred_element_type=F32)

    @pl.when((k > 0) & (k < N_KSTEPS))
    def _():
        acc_ref[...] += jnp.dot(y_ref[...].astype(BF16), wo_ref[...], preferred_element_type=F32)

    @pl.when(k == N_KSTEPS - 1)
    def _():
        rows = min(256, acc_ref.shape[0])
        for r in range(acc_ref.shape[0] // rows):
            rs = slice(r * rows, (r + 1) * rows)
            mix = acc_ref[rs, :]
            ms = jnp.mean(mix * mix, axis=-1, keepdims=True)
            xn = x_ref[rs, :] + (mix * lax.rsqrt(ms + EPS)) * npost_ref[...]
            acc_ref[rs, :] = xn
            xnb_ref[rs, :] = xn.astype(BF16)

    for j in range(N_NSTEPS):
        @pl.when(k == N_KSTEPS + j)
        def _():
            gate = _sigmoid(jnp.dot(xnb_ref[...], gw_ref[...], preferred_element_type=F32))
            e = jnp.dot(p_ref[...].astype(BF16), pp_ref[...], preferred_element_type=F32)
            o_ref[...] = acc_ref[:, j * OUT_NT:(j + 1) * OUT_NT] + gate * e


def _out_proj(y_mix, x2d, p_all, prm, *, layer):
    t = x2d.shape[0]
    tm = min(OUT_TM, t)

    def nstep(k):
        return jnp.maximum(k - N_KSTEPS, 0)

    n_row_tiles = t // tm

    def row_tile(i, k, first_unused):
        return jnp.minimum(i + (k >= first_unused).astype(jnp.int32), n_row_tiles - 1)

    return pl.pallas_call(
        _outproj_kernel,
        grid=(n_row_tiles, N_KSTEPS + N_NSTEPS),
        in_specs=[
            pl.BlockSpec((tm, OUT_KT), lambda i, k: (row_tile(i, k, N_KSTEPS + 1),
                                                     jnp.where(k > N_KSTEPS, 0, jnp.minimum(k, N_KSTEPS - 1)))),
            pl.BlockSpec((OUT_KT, D_MODEL),
                         lambda i, k: (jnp.where(k > N_KSTEPS + 1, 0, jnp.minimum(k, N_KSTEPS - 1)), 0)),
            pl.BlockSpec((tm, D_MODEL), lambda i, k: (row_tile(i, k, N_KSTEPS), 0)),
            pl.BlockSpec((1, D_MODEL), lambda i, k: (0, 0)),
            pl.BlockSpec((None, tm, PLE_DIM), lambda i, k: (layer, i, 0)),
            pl.BlockSpec((PLE_DIM, OUT_NT), lambda i, k: (0, nstep(k))),
            pl.BlockSpec((D_MODEL, OUT_NT), lambda i, k: (0, nstep(k))),
        ],
        out_specs=pl.BlockSpec((tm, OUT_NT), lambda i, k: (i, nstep(k))),
        out_shape=jax.ShapeDtypeStruct((t, D_MODEL), F32),
        scratch_shapes=[pltpu.VMEM((tm, D_MODEL), F32),
                        pltpu.VMEM((tm, D_MODEL), BF16)],
        compiler_params=_cparams(2),
        name="out_proj",
    )(y_mix.reshape(t, D_MIX), prm["w_out"], x2d, prm["norm_post"], p_all, prm["ple_proj"], prm["ple_gate"])


_W_IN_SRC = {}
_o = 0
for _name, _size in (("z", SSD_WIDTH), ("xbc", SSD_CONV_CH), ("dt", SSD_HEADS), ("q", ATT_WIDTH), ("k", KV_WIDTH),
                     ("v", KV_WIDTH), ("g_att", ATT_WIDTH), ("b_sc", SC_WIDTH), ("c_sc", SC_WIDTH),
                     ("h_sc", SC_WIDTH), ("g_sc", SC_WIDTH)):
    _W_IN_SRC[_name] = (_o, _size)
    _o += _size
N_IN = _o
_W_IN_DST = {"z": COL_Z, "xbc": COL_X, "q": COL_Q, "g_att": COL_GATT, "b_sc": COL_BSC, "c_sc": COL_CSC,
             "h_sc": COL_HSC, "g_sc": COL_GSC, "k": COL_K, "v": COL_V}
REGROUP_COLS = 256
REGROUP_UNIT = 32


def _regroup_tables(layer):
    starts, valid = [], []
    for j in range(N_PROJ // REGROUP_COLS):
        dst = j * REGROUP_COLS
        src, n = 0, 0
        for name, seg_dst in _W_IN_DST.items():
            seg_src, width = _W_IN_SRC[name]
            if seg_dst <= dst < seg_dst + width:
                src, n = seg_src + dst - seg_dst, REGROUP_COLS
        if dst == COL_DT:
            src, n = _W_IN_SRC["dt"]
        assert src % REGROUP_UNIT == 0 and src + REGROUP_COLS <= N_IN
        starts.append((layer * N_IN + src) // REGROUP_UNIT)
        valid.append(n)
    return jnp.asarray(starts, jnp.int32), jnp.asarray(valid, jnp.int32)


def _regroup_kernel(starts_ref, valid_ref, w_ref, o_ref):
    col = lax.broadcasted_iota(jnp.int32, (D_MODEL, REGROUP_COLS), 1)
    o_ref[...] = jnp.where(col < valid_ref[pl.program_id(0)], w_ref[...].T, 0.0).astype(BF16)


def _regroup_w_in(w_cols, layer):
    starts, valid = _regroup_tables(layer)
    return pl.pallas_call(
        _regroup_kernel,
        grid_spec=pltpu.PrefetchScalarGridSpec(
            num_scalar_prefetch=2,
            grid=(N_PROJ // REGROUP_COLS,),
            in_specs=[pl.BlockSpec((pl.Element(REGROUP_COLS), pl.Element(D_MODEL)),
                                   lambda j, s, v: (s[j] * REGROUP_UNIT, 0))],
            out_specs=pl.BlockSpec((D_MODEL, REGROUP_COLS), lambda j, s, v: (0, j))),
        out_shape=jax.ShapeDtypeStruct((D_MODEL, N_PROJ), BF16),
        compiler_params=_cparams(1),
        name="regroup_w_in",
    )(starts, valid, w_cols)


def _lane_pad(v):
    return jnp.pad(v, (0, LANES - v.shape[0])).reshape(1, LANES)


def _layer_params(i, w_in, w_out, norm_pre, norm_post, ssd_conv_w, ssd_conv_b, ssd_a_log, ssd_dt_bias, ssd_d,
                  ssd_norm, attn_sinks, sc_conv_w, ple_proj, ple_gate, expand):
    return {
        "w_in": _regroup_w_in(w_in, i),
        "w_out": w_out[i].astype(BF16),
        "norm_pre": norm_pre[i].reshape(1, D_MODEL),
        "norm_post": norm_post[i].reshape(1, D_MODEL),
        "ssd_conv_w": ssd_conv_w[i],
        "ssd_conv_b": ssd_conv_b[i].reshape(1, SSD_CONV_CH),
        "a_log": _lane_pad(ssd_a_log[i]),
        "dt_bias": _lane_pad(ssd_dt_bias[i]),
        "d_skip": jnp.repeat(ssd_d[i], SSD_HEAD_DIM).reshape(1, SSD_WIDTH),
        "ssd_norm": ssd_norm[i].reshape(1, SSD_WIDTH),
        "sinks": attn_sinks[i],
        "sc_conv_w": sc_conv_w[i],
        "ple_proj": ple_proj[i].astype(BF16),
        "ple_gate": ple_gate[i].astype(BF16),
        "expand": expand,
    }


def _rope_tables(pos0, n):
    half = HEAD_DIM // 2
    inv_freq = jnp.exp(-math.log(ROPE_THETA) * jnp.arange(half, dtype=F32) * (2.0 / HEAD_DIM))
    pos = pos0 + jnp.arange(n, dtype=jnp.int32)
    ang = pos.astype(F32)[:, None] * inv_freq[None, :]
    cos, sin = jnp.cos(ang), jnp.sin(ang)
    reps = LANES // HEAD_DIM
    return (jnp.tile(jnp.concatenate([cos, cos], axis=1), (1, reps)),
            jnp.tile(jnp.concatenate([-sin, sin], axis=1), (1, reps)))


def _layer(x2d, p_all, prm, rope_tab, carried, *, layer, depth, batch, seqlen, conv_state, ssm_all, k_all, v_all,
           sc_state):
    decode = conv_state is not None
    ssm_out_prev, kv_out_prev = carried
    proj = _in_proj(x2d, prm["norm_pre"], prm["w_in"])
    if decode:
        sc_pad = jnp.pad(sc_state, ((0, 0), (SUBLANES - (SC_CONV - 1), 0), (0, 0)))
        y_mix, conv_new, ssm_out = _ssd_decode_mixer(proj, conv_state, ssm_all, ssm_out_prev, prm, layer=layer,
                                                     depth=depth, batch=batch, seqlen=seqlen)
    else:
        sc_pad = None
        y_mix, conv_new, ssm_out = _ssd_mixer(proj, ssm_out_prev, prm, layer=layer, depth=depth, batch=batch,
                                              seqlen=seqlen)
    y_mix, kv_out, sc_new = _att_mixer(proj, y_mix, sc_pad, k_all, v_all, kv_out_prev, prm, rope_tab, layer=layer,
                                       depth=depth, batch=batch, seqlen=seqlen)
    x_new = _out_proj(y_mix, x2d, p_all, prm, layer=layer)
    return x_new, conv_new, sc_new, (ssm_out, kv_out)


def kernel(x_prompt, x_sample, p_prompt, p_sample, state_ssd_conv, state_ssm, cache_k, cache_v, state_sc_conv,
           w_in, w_out, norm_pre, norm_post, ssd_conv_w, ssd_conv_b, ssd_a_log, ssd_dt_bias, ssd_d, ssd_norm,
           attn_sinks, sc_conv_w, ple_proj, ple_gate):
    bp, lp, _ = x_prompt.shape
    bs, ls, _ = x_sample.shape
    depth = w_in.shape[0]
    assert lp % CHUNK == 0 and ls + SSD_CONV - 1 <= DEC_TILE and bs % DEC_G == 0 and cache_k.shape[2] == WINDOW
    head = jnp.arange(LANES, dtype=jnp.int32)[:, None]
    chan = jnp.arange(SSD_WIDTH, dtype=jnp.int32)[None, :]
    expand = (chan // SSD_HEAD_DIM == head).astype(BF16)
    rope_p = _rope_tables(0, lp)
    rope_s = _rope_tables(PAST_LEN, SUBLANES)
    yp = x_prompt.reshape(bp * lp, D_MODEL)
    ys = x_sample.reshape(bs * ls, D_MODEL)
    pp_all = p_prompt.reshape(depth, bp * lp, PLE_DIM)
    ps_all = p_sample.reshape(depth, bs * ls, PLE_DIM)
    ssm_all = state_ssm.reshape(depth, bs, SSD_WIDTH, SSD_STATE)
    w_cols = jnp.swapaxes(w_in, 1, 2).reshape(depth * N_IN, D_MODEL)
    k_all = cache_k.transpose(0, 1, 3, 4, 2).reshape(depth, bs, KV_WIDTH, WINDOW)
    v_all = cache_v.transpose(0, 1, 3, 4, 2).reshape(depth, bs, KV_WIDTH, WINDOW)
    carried_p = carried_s = (None, None)
    conv_p, sc_p, conv_s, sc_s = [], [], [], []
    for i in range(depth):
        prm = _layer_params(i, w_cols, w_out, norm_pre, norm_post, ssd_conv_w, ssd_conv_b, ssd_a_log, ssd_dt_bias,
                            ssd_d, ssd_norm, attn_sinks, sc_conv_w, ple_proj, ple_gate, expand)
        yp, cv, sc, carried_p = _layer(yp, pp_all, prm, rope_p, carried_p, layer=i, depth=depth, batch=bp, seqlen=lp,
                                       conv_state=None, ssm_all=None, k_all=None, v_all=None, sc_state=None)
        conv_p.append(cv)
        sc_p.append(sc)
        ys, cv, sc, carried_s = _layer(ys, ps_all, prm, rope_s, carried_s, layer=i, depth=depth, batch=bs, seqlen=ls,
                                       conv_state=state_ssd_conv[i], ssm_all=ssm_all, k_all=k_all, v_all=v_all,
                                       sc_state=state_sc_conv[i])
        conv_s.append(cv)
        sc_s.append(sc)

    def states(conv, carried, sc, batch):
        ssm_out, (k_out, v_out) = carried
        return (jnp.stack(conv), ssm_out.reshape(depth, batch, SSD_HEADS, SSD_HEAD_DIM, SSD_STATE),
                k_out.reshape(depth, batch, N_KV_HEADS, HEAD_DIM, WINDOW).transpose(0, 1, 4, 2, 3),
                v_out.reshape(depth, batch, N_KV_HEADS, HEAD_DIM, WINDOW).transpose(0, 1, 4, 2, 3), jnp.stack(sc))

    return (yp.reshape(bp, lp, D_MODEL), ys.reshape(bs, ls, D_MODEL),
            *states(conv_p, carried_p, sc_p, bp), *states(conv_s, carried_s, sc_s, bs))
```

```python
import functools
import math

import jax
import jax.numpy as jnp
from jax import lax
from jax.experimental import pallas as pl
from jax.experimental.pallas import tpu as pltpu

F32 = jnp.float32
BF16 = jnp.bfloat16

D_MODEL = 2048
D_MIX = 2 * D_MODEL
SSD_WIDTH = D_MIX // 2
SSD_HEAD_DIM = 64
SSD_HEADS = SSD_WIDTH // SSD_HEAD_DIM
SSD_GROUPS = 4
SSD_STATE = 128
SSD_CONV = 4
SSD_CONV_CH = SSD_WIDTH + 2 * SSD_GROUPS * SSD_STATE
ATT_WIDTH = D_MIX // 4
HEAD_DIM = 64
N_Q_HEADS = ATT_WIDTH // HEAD_DIM
N_KV_HEADS = 4
KV_WIDTH = N_KV_HEADS * HEAD_DIM
WINDOW = 128
ROPE_THETA = 10000.0
SC_WIDTH = D_MIX // 4
SC_CONV = 3
PLE_DIM = 256
EPS = 1e-6
PAST_LEN = 8192

LANES = 128
SUBLANES = 8
CHUNK = 128
GROUP_W = SSD_WIDTH // SSD_GROUPS
HEADS_PER_GROUP = SSD_HEADS // SSD_GROUPS

COL_Z = 0
COL_X = 2048
COL_B = 4096
COL_C = 4608
COL_Q = 5120
COL_GATT = 6144
COL_BSC = 7168
COL_CSC = 8192
COL_HSC = 9216
COL_GSC = 10240
COL_K = 11264
COL_V = 11520
COL_DT = 11776
N_PROJ = 12288

VMEM_LIMIT = 56 * 1024 * 1024


def _cparams(ndims):
    return pltpu.CompilerParams(dimension_semantics=("arbitrary",) * ndims, vmem_limit_bytes=VMEM_LIMIT)


def _silu(x):
    h = 0.5 * x
    return h + h * jnp.tanh(h)


def _sigmoid(x):
    return 0.5 + 0.5 * jnp.tanh(0.5 * x)


def _inproj_kernel(x_ref, nw_ref, w_ref, o_ref, h_ref, *, tm, row_chunk):
    @pl.when(pl.program_id(1) == 0)
    def _():
        def body(r, carry):
            rows = pl.ds(pl.multiple_of(r * row_chunk, row_chunk), row_chunk)
            x = x_ref[rows, :]
            ms = jnp.mean(x * x, axis=-1, keepdims=True)
            h_ref[rows, :] = ((x * lax.rsqrt(ms + EPS)) * nw_ref[...]).astype(BF16)
            return carry

        lax.fori_loop(0, tm // row_chunk, body, 0)

    o_ref[...] = jnp.dot(h_ref[...], w_ref[...], preferred_element_type=F32)


def _in_proj(x2d, nw, w_bf16):
    t = x2d.shape[0]
    tm = min(1024, t)
    tn = 1024
    return pl.pallas_call(
        functools.partial(_inproj_kernel, tm=tm, row_chunk=128),
        grid=(t // tm, N_PROJ // tn),
        in_specs=[
            pl.BlockSpec((tm, D_MODEL), lambda i, j: (i, 0)),
            pl.BlockSpec((1, D_MODEL), lambda i, j: (0, 0)),
            pl.BlockSpec((D_MODEL, tn), lambda i, j: (0, j)),
        ],
        out_specs=pl.BlockSpec((tm, tn), lambda i, j: (i, j)),
        out_shape=jax.ShapeDtypeStruct((t, N_PROJ), F32),
        scratch_shapes=[pltpu.VMEM((tm, D_MODEL), BF16)],
        compiler_params=_cparams(2),
        name="in_proj",
    )(x2d, nw, w_bf16)


DEC_G = 4
DEC_TILE = SUBLANES


def _softplus(x):
    return jnp.maximum(x, 0.0) + jnp.log1p(jnp.exp(-jnp.abs(x)))


def _ssd_conv_silu(xpad, cw_ref, cbias_ref, xs_scr, bm_scr, cm_scr):
    cwid = 256
    base = SUBLANES - (SSD_CONV - 1)
    for cb in range(SSD_CONV_CH // cwid):
        cols = slice(cb * cwid, (cb + 1) * cwid)
        acc = xpad[base:base + CHUNK, cols] * cw_ref[0:1, cols]
        for j in range(1, SSD_CONV):
            acc = acc + xpad[base + j:base + j + CHUNK, cols] * cw_ref[j:j + 1, cols]
        act = _silu(acc + cbias_ref[:, cols])
        lo = cb * cwid
        if lo < SSD_WIDTH:
            xs_scr[:, lo:lo + cwid] = act
        elif lo < SSD_WIDTH + GROUP_W:
            bm_scr[:, lo - SSD_WIDTH:lo - SSD_WIDTH + cwid] = act.astype(BF16)
        else:
            o = lo - SSD_WIDTH - GROUP_W
            cm_scr[:, o:o + cwid] = act.astype(BF16)


def _ssd_expand(v, e_mat):
    hi = v.astype(BF16)
    lo = (v - hi.astype(F32)).astype(BF16)
    return jnp.dot(hi, e_mat, preferred_element_type=F32) + jnp.dot(lo, e_mat, preferred_element_type=F32)


def _ssd_diag(a_cum, dt, allowed, xs_scr, bm_scr, cm_scr, y_scr, after_group=None):
    Q = CHUNK
    a_cum_t = a_cum.T
    dt_t = dt.T
    lane = lax.broadcasted_iota(jnp.int32, (Q, LANES), 1)
    low_half = lane < SSD_HEAD_DIM
    for g in range(SSD_GROUPS):
        gcols = slice(g * SSD_STATE, (g + 1) * SSD_STATE)
        cbm = lax.dot_general(cm_scr[:, gcols], bm_scr[:, gcols], (((1,), (1,)), ((), ())),
                              preferred_element_type=F32)
        for p in range(HEADS_PER_GROUP // 2):
            pair = g * (HEADS_PER_GROUP // 2) + p
            ms = []
            for h in (2 * pair, 2 * pair + 1):
                seg = jnp.broadcast_to(a_cum[:, h:h + 1], (Q, Q)) - a_cum_t[h:h + 1, :]
                dec = jnp.exp(jnp.where(allowed, seg, -jnp.inf))
                ms.append((cbm * dec * dt_t[h:h + 1, :]).astype(BF16))
            lhs = jnp.concatenate(ms, axis=1)
            xp = xs_scr[:, pair * LANES:(pair + 1) * LANES]
            rhs = jnp.concatenate([jnp.where(low_half, xp, 0.0).astype(BF16),
                                   jnp.where(low_half, 0.0, xp).astype(BF16)], axis=0)
            y_scr[:, pair * LANES:(pair + 1) * LANES] = jnp.dot(lhs, rhs, preferred_element_type=F32)
        if after_group is not None:
            after_group(g)


def _ssd_gate_norm(yg, zg, nw):
    yg = yg * _silu(zg)
    msq = jnp.mean(yg * yg, axis=-1, keepdims=True)
    return (yg * lax.rsqrt(msq + EPS)) * nw


def _ssd_decode_kernel(*refs, L, n_alias):
    Q = CHUNK
    n_in = 14
    refs = refs[:n_in] + refs[n_in + n_alias:]
    (z_ref, x_ref, b_ref, c_ref, dt_ref, cs_ref, ss_ref, cw_ref, cbias_ref, alog_ref, dtb_ref, dsk_ref,
     nw_ref, e_ref, y_ref, cso_ref, sso_ref,
     xpad, xs_scr, bm_scr, cm_scr, y_scr, ea_scr, we_scr, zp_scr, dtp_scr) = refs
    t0 = DEC_TILE - L
    used = DEC_G * DEC_TILE

    xpad[...] = jnp.zeros(xpad.shape, F32)
    for g in range(DEC_G):
        r0 = SUBLANES + DEC_TILE * g
        tok = slice(L * g, L * (g + 1))
        xpad[r0:r0 + DEC_TILE, :] = cs_ref[g]
        xpad[r0 + t0:r0 + DEC_TILE, 0:SSD_WIDTH] = x_ref[tok, :]
        xpad[r0 + t0:r0 + DEC_TILE, SSD_WIDTH:SSD_WIDTH + GROUP_W] = b_ref[tok, :]
        xpad[r0 + t0:r0 + DEC_TILE, SSD_WIDTH + GROUP_W:SSD_CONV_CH] = c_ref[tok, :]
    for g in range(DEC_G):
        r0 = SUBLANES + DEC_TILE * g
        cso_ref[g] = xpad[r0:r0 + DEC_TILE, :]
    _ssd_conv_silu(xpad, cw_ref, cbias_ref, xs_scr, bm_scr, cm_scr)

    row = lax.broadcasted_iota(jnp.int32, (Q, LANES), 0)
    in_tile = row & (DEC_TILE - 1)
    dtp_scr[...] = jnp.zeros((Q, LANES), F32)
    zp_scr[...] = jnp.zeros(zp_scr.shape, F32)
    for g in range(DEC_G):
        dtp_scr[DEC_TILE * g + t0:DEC_TILE * (g + 1), :] = dt_ref[L * g:L * (g + 1), :]
        zp_scr[DEC_TILE * g + t0:DEC_TILE * (g + 1), :] = z_ref[L * g:L * (g + 1), :]
    dt = jnp.where((in_tile >= t0) & (row < used), _softplus(dtp_scr[...] + dtb_ref[...]), 0.0)
    a_cum = dt * (-jnp.exp(alog_ref[...]))
    k = 1
    while k < DEC_TILE:
        a_cum = a_cum + jnp.where(in_tile >= k, pltpu.roll(a_cum, k, 0), 0.0)
        k *= 2
    tot = jnp.where(in_tile == DEC_TILE - 1, a_cum, 0.0)
    k = 1
    while k < DEC_TILE:
        tot = tot + pltpu.roll(tot, Q - k, 0)
        k *= 2
    e_mat = e_ref[...]
    ea_scr[...] = _ssd_expand(jnp.exp(a_cum), e_mat)
    we_scr[...] = _ssd_expand(jnp.exp(tot - a_cum) * dt, e_mat)
    cd_t = jnp.exp(tot).T

    lrow = lax.broadcasted_iota(jnp.int32, (Q, Q), 0)
    scol = lax.broadcasted_iota(jnp.int32, (Q, Q), 1)
    allowed = (lrow >= scol) & ((lrow // DEC_TILE) == (scol // DEC_TILE))
    _ssd_diag(a_cum, dt, allowed, xs_scr, bm_scr, cm_scr, y_scr)

    pair_rows = 2 * DEC_TILE
    first_of_pair = lax.broadcasted_iota(jnp.int32, (pair_rows, GROUP_W), 0) < DEC_TILE
    for g in range(SSD_GROUPS):
        gcols = slice(g * SSD_STATE, (g + 1) * SSD_STATE)
        wcols = slice(g * GROUP_W, (g + 1) * GROUP_W)
        for sp in range(DEC_G // 2):
            rows = slice(sp * pair_rows, (sp + 1) * pair_rows)
            offs = [lax.dot_general(cm_scr[rows, gcols], ss_ref[2 * sp + j, wcols, :].astype(BF16),
                                    (((1,), (1,)), ((), ())), preferred_element_type=F32) for j in range(2)]
            y_off = jnp.where(first_of_pair, offs[0], offs[1])
            y_scr[rows, wcols] = y_scr[rows, wcols] + y_off * ea_scr[rows, wcols]
        yg = y_scr[0:used, wcols] + dsk_ref[:, wcols] * xs_scr[0:used, wcols]
        out = _ssd_gate_norm(yg, zp_scr[:, wcols], nw_ref[:, wcols])
        for s in range(DEC_G):
            y_ref[L * s:L * (s + 1), wcols] = out[DEC_TILE * s + t0:DEC_TILE * (s + 1)].astype(y_ref.dtype)

    row_seq = lax.broadcasted_iota(jnp.int32, (Q, SSD_STATE), 0) // DEC_TILE
    for g in range(SSD_GROUPS):
        wcols = slice(g * GROUP_W, (g + 1) * GROUP_W)
        gcols = slice(g * SSD_STATE, (g + 1) * SSD_STATE)
        xw_t = (xs_scr[:, wcols] * we_scr[:, wcols]).T.astype(BF16)
        bm_g = bm_scr[:, gcols]
        for s in range(DEC_G):
            st = jnp.dot(xw_t, jnp.where(row_seq == s, bm_g, jnp.zeros_like(bm_g)),
                         preferred_element_type=F32)
            last = DEC_TILE * (s + 1) - 1
            cd_rows = jnp.broadcast_to(cd_t[:, last:last + 1], (LANES, LANES))
            for hh in range(HEADS_PER_GROUP):
                h = g * HEADS_PER_GROUP + hh
                r0 = h * SSD_HEAD_DIM
                sso_ref[s, r0:r0 + SSD_HEAD_DIM, :] = (ss_ref[s, r0:r0 + SSD_HEAD_DIM, :] * cd_rows[h:h + 1, :]
                                                       + st[hh * SSD_HEAD_DIM:(hh + 1) * SSD_HEAD_DIM, :])


def _ssd_kernel(*refs, nc, n_alias):
    Q = CHUNK
    n_in = 12
    refs = refs[:n_in] + refs[n_in + n_alias:]
    (z_ref, x_ref, b_ref, c_ref, dt_ref, cw_ref, cbias_ref, alog_ref, dtb_ref, dsk_ref, nw_ref, e_ref,
     y_ref, cso_ref, sso_ref,
     xpad, xs_scr, bm_scr, cm_scr, y_scr, ea_scr, we_scr) = refs
    c = pl.program_id(1)

    @pl.when(c == 0)
    def _init():
        sso_ref[...] = jnp.zeros(sso_ref.shape, F32)
        xpad[0:SUBLANES, :] = jnp.zeros((SUBLANES, SSD_CONV_CH), F32)

    xpad[SUBLANES:SUBLANES + Q, 0:SSD_WIDTH] = x_ref[...]
    xpad[SUBLANES:SUBLANES + Q, SSD_WIDTH:SSD_WIDTH + GROUP_W] = b_ref[...]
    xpad[SUBLANES:SUBLANES + Q, SSD_WIDTH + GROUP_W:SSD_CONV_CH] = c_ref[...]
    _ssd_conv_silu(xpad, cw_ref, cbias_ref, xs_scr, bm_scr, cm_scr)

    tail_tile = xpad[Q:Q + SUBLANES, :]

    @pl.when(c == nc - 1)
    def _():
        cso_ref[...] = tail_tile

    xpad[0:SUBLANES, :] = tail_tile

    row = lax.broadcasted_iota(jnp.int32, (Q, LANES), 0)
    dt = _softplus(dt_ref[...] + dtb_ref[...])
    a_cum = dt * (-jnp.exp(alog_ref[...]))
    k = 1
    while k < Q:
        a_cum = a_cum + jnp.where(row >= k, pltpu.roll(a_cum, k, 0), 0.0)
        k *= 2
    a_last = a_cum[Q - 1:Q, :]
    e_mat = e_ref[...]
    ea_scr[...] = _ssd_expand(jnp.exp(a_cum), e_mat)
    we_scr[...] = _ssd_expand(jnp.exp(a_last - a_cum) * dt, e_mat)
    cd_rows = jnp.broadcast_to(jnp.exp(a_cum.T[:, Q - 1:Q]), (LANES, LANES))

    def carried_state_term(g):
        gcols = slice(g * SSD_STATE, (g + 1) * SSD_STATE)
        wcols = slice(g * GROUP_W, (g + 1) * GROUP_W)
        s_g = sso_ref[wcols, :].astype(BF16)
        y_off = lax.dot_general(cm_scr[:, gcols], s_g, (((1,), (1,)), ((), ())), preferred_element_type=F32)
        y_scr[:, wcols] = y_scr[:, wcols] + y_off * ea_scr[:, wcols] + dsk_ref[:, wcols] * xs_scr[:, wcols]

    causal = lax.broadcasted_iota(jnp.int32, (Q, Q), 0) >= lax.broadcasted_iota(jnp.int32, (Q, Q), 1)
    _ssd_diag(a_cum, dt, causal, xs_scr, bm_scr, cm_scr, y_scr, after_group=carried_state_term)

    for g in range(SSD_GROUPS):
        wcols = slice(g * GROUP_W, (g + 1) * GROUP_W)
        y_ref[:, wcols] = _ssd_gate_norm(y_scr[:, wcols], z_ref[:, wcols], nw_ref[:, wcols]).astype(y_ref.dtype)

    for g in range(SSD_GROUPS):
        wcols = slice(g * GROUP_W, (g + 1) * GROUP_W)
        gcols = slice(g * SSD_STATE, (g + 1) * SSD_STATE)
        xw_t = (xs_scr[:, wcols] * we_scr[:, wcols]).T.astype(BF16)
        st = jnp.dot(xw_t, bm_scr[:, gcols], preferred_element_type=F32)
        for hh in range(HEADS_PER_GROUP):
            h = g * HEADS_PER_GROUP + hh
            r0 = h * SSD_HEAD_DIM
            sso_ref[r0:r0 + SSD_HEAD_DIM, :] = (sso_ref[r0:r0 + SSD_HEAD_DIM, :] * cd_rows[h:h + 1, :]
                                                + st[hh * SSD_HEAD_DIM:(hh + 1) * SSD_HEAD_DIM, :])


def _ssd_specs_common(prm):
    shapes = [(SSD_CONV, SSD_CONV_CH), (1, SSD_CONV_CH), (1, LANES), (1, LANES), (1, SSD_WIDTH), (1, SSD_WIDTH),
              (LANES, SSD_WIDTH)]
    args = [prm["ssd_conv_w"], prm["ssd_conv_b"], prm["a_log"], prm["dt_bias"], prm["d_skip"], prm["ssd_norm"],
            prm["expand"]]
    scratch = [pltpu.VMEM((SUBLANES + CHUNK, SSD_CONV_CH), F32),
               pltpu.VMEM((CHUNK, SSD_WIDTH), F32),
               pltpu.VMEM((CHUNK, GROUP_W), BF16),
               pltpu.VMEM((CHUNK, GROUP_W), BF16),
               pltpu.VMEM((CHUNK, SSD_WIDTH), F32),
               pltpu.VMEM((CHUNK, SSD_WIDTH), F32),
               pltpu.VMEM((CHUNK, SSD_WIDTH), F32)]
    return shapes, args, scratch


def _ssd_mixer(proj, ssm_out_prev, prm, *, layer, depth, batch, seqlen):
    nc = seqlen // CHUNK

    def blk(width, col):
        return pl.BlockSpec((CHUNK, width), lambda b, c: (b * nc + c, col // width))

    def const(shape):
        return pl.BlockSpec(shape, lambda b, c: (0,) * len(shape))

    state_spec = pl.BlockSpec((None, None, SSD_WIDTH, SSD_STATE), lambda b, c: (layer, b, 0, 0))
    shapes, cargs, scratch = _ssd_specs_common(prm)
    in_specs = [blk(SSD_WIDTH, COL_Z), blk(SSD_WIDTH, COL_X), blk(GROUP_W, COL_B), blk(GROUP_W, COL_C),
                blk(LANES, COL_DT)] + [const(s) for s in shapes]
    args = [proj] * 5 + cargs
    aliases = {}
    if ssm_out_prev is not None:
        aliases[len(args)] = 2
        in_specs += [pl.BlockSpec(memory_space=pl.ANY)]
        args += [ssm_out_prev]
    y, cso, sso = pl.pallas_call(
        functools.partial(_ssd_kernel, nc=nc, n_alias=len(aliases)),
        grid=(batch, nc),
        in_specs=in_specs,
        out_specs=[pl.BlockSpec((CHUNK, SSD_WIDTH), lambda b, c: (b * nc + c, 0)),
                   pl.BlockSpec((None, SUBLANES, SSD_CONV_CH), lambda b, c: (b, 0, 0)),
                   state_spec],
        out_shape=[jax.ShapeDtypeStruct((batch * seqlen, D_MIX), BF16),
                   jax.ShapeDtypeStruct((batch, SUBLANES, SSD_CONV_CH), F32),
                   jax.ShapeDtypeStruct((depth, batch, SSD_WIDTH, SSD_STATE), F32)],
        scratch_shapes=scratch,
        input_output_aliases=aliases,
        compiler_params=_cparams(2),
        name="ssd_prompt",
    )(*args)
    return y, cso[:, SUBLANES - (SSD_CONV - 1):, :], sso


def _ssd_decode_mixer(proj, conv_state, ssm_all, ssm_out_prev, prm, *, layer, depth, batch, seqlen):
    t0 = DEC_TILE - seqlen
    hist = SSD_CONV - 1
    conv_pad = jnp.pad(conv_state, ((0, 0), (t0 - hist, DEC_TILE - t0), (0, 0)))
    rows = DEC_G * seqlen

    def blk(width, col):
        return pl.BlockSpec((rows, width), lambda s: (s, col // width))

    def const(shape):
        return pl.BlockSpec(shape, lambda s: (0,) * len(shape))

    state_spec = pl.BlockSpec((None, DEC_G, SSD_WIDTH, SSD_STATE), lambda s: (layer, s, 0, 0))
    shapes, cargs, scratch = _ssd_specs_common(prm)
    in_specs = [blk(SSD_WIDTH, COL_Z), blk(SSD_WIDTH, COL_X), blk(GROUP_W, COL_B), blk(GROUP_W, COL_C),
                blk(LANES, COL_DT),
                pl.BlockSpec((DEC_G, DEC_TILE, SSD_CONV_CH), lambda s: (s, 0, 0)), state_spec]
    in_specs += [const(s) for s in shapes]
    args = [proj] * 5 + [conv_pad, ssm_all] + cargs
    aliases = {}
    if ssm_out_prev is not None:
        aliases[len(args)] = 2
        in_specs += [pl.BlockSpec(memory_space=pl.ANY)]
        args += [ssm_out_prev]
    scratch += [pltpu.VMEM((DEC_G * DEC_TILE, SSD_WIDTH), F32), pltpu.VMEM((CHUNK, LANES), F32)]
    y, cso, sso = pl.pallas_call(
        functools.partial(_ssd_decode_kernel, L=seqlen, n_alias=len(aliases)),
        grid=(batch // DEC_G,),
        in_specs=in_specs,
        out_specs=[pl.BlockSpec((rows, SSD_WIDTH), lambda s: (s, 0)),
                   pl.BlockSpec((DEC_G, DEC_TILE, SSD_CONV_CH), lambda s: (s, 0, 0)),
                   state_spec],
        out_shape=[jax.ShapeDtypeStruct((batch * seqlen, D_MIX), F32),
                   jax.ShapeDtypeStruct((batch, DEC_TILE, SSD_CONV_CH), F32),
                   jax.ShapeDtypeStruct((depth, batch, SSD_WIDTH, SSD_STATE), F32)],
        scratch_shapes=scratch,
        input_output_aliases=aliases,
        compiler_params=_cparams(1),
        name="ssd_decode",
    )(*args)
    return y, cso[:, DEC_TILE - hist:, :], sso


def _att_stack_cols(tq):
    return -(-(N_Q_HEADS // N_KV_HEADS) * tq // LANES) * LANES


def _att_kernel(*refs, L, nb, carry, n_alias, row0=0):
    TQ = CHUNK if L == CHUNK else SUBLANES
    W = WINDOW
    n_in = 12 if carry else 15
    refs = refs[:n_in] + refs[n_in + n_alias:]
    if carry:
        (q_ref, k_ref, v_ref, g_ref, sb_ref, sc_ref, sh_ref, sg_ref, cos_ref, sin_ref, sinks_ref, scw_ref,
         y_ref, ko_ref, vo_ref, sco_ref,
         kd, vt, xpad2) = refs
    else:
        (q_ref, k_ref, v_ref, g_ref, sb_ref, sc_ref, sh_ref, sg_ref, cos_ref, sin_ref, sinks_ref, scw_ref,
         sch_ref, kp_ref, vp_ref,
         y_ref, ko_ref, vo_ref, sco_ref,
         kd, vt, xpad2, pad_w, pad_kv, kc_scr, vc_scr) = refs
    i = pl.program_id(1)
    padded = L != TQ
    R = _att_stack_cols(TQ)

    def full(ref, scr):
        if not padded:
            return ref[...]
        scr[...] = jnp.zeros(scr.shape, F32)
        scr[0:L, :] = ref[row0:row0 + L, :]
        return scr[...]

    def store_rows(cols, val):
        y_ref[row0:row0 + L, cols] = val[0:L].astype(y_ref.dtype)

    if padded:
        vals = {name: full(ref, pad_w) for name, ref in
                (("q", q_ref), ("g", g_ref), ("sb", sb_ref), ("sc", sc_ref), ("sh", sh_ref), ("sg", sg_ref))}

        def get(name, cols=slice(None)):
            return vals[name][:, cols]
    else:
        srcs = {"q": q_ref, "g": g_ref, "sb": sb_ref, "sc": sc_ref, "sh": sh_ref, "sg": sg_ref}

        def get(name, cols=slice(None)):
            return srcs[name][:, cols]

    lane = lax.broadcasted_iota(jnp.int32, (TQ, LANES), 1)
    first_half = (lane % HEAD_DIM) < (HEAD_DIM // 2)
    cos = cos_ref[...]
    sin = sin_ref[...]

    def rope(x):
        sw = jnp.where(first_half, pltpu.roll(x, LANES - HEAD_DIM // 2, 1), pltpu.roll(x, HEAD_DIM // 2, 1))
        return x * cos + sw * sin

    k_new = full(k_ref, pad_kv if padded else None)
    k_rot = jnp.concatenate([rope(k_new[:, 0:LANES]), rope(k_new[:, LANES:2 * LANES])], axis=1)
    v_new = full(v_ref, pad_kv if padded else None)
    low_w = lax.broadcasted_iota(jnp.int32, (W, LANES), 1) < HEAD_DIM

    def stage_kv(slot, k_blk, v_blk):
        for kvp in range(N_KV_HEADS // 2):
            kpair = k_blk[:, kvp * LANES:(kvp + 1) * LANES]
            kroll = pltpu.roll(kpair, HEAD_DIM, 1)
            kd[slot, 2 * kvp] = jnp.where(low_w, kpair, kroll).astype(BF16)
            kd[slot, 2 * kvp + 1] = jnp.where(low_w, kroll, kpair).astype(BF16)
            vt[slot, kvp] = v_blk[:, kvp * LANES:(kvp + 1) * LANES].T.astype(BF16)

    def to_channel_major(x):
        return jnp.concatenate([x[:, 0:LANES].T, x[:, LANES:2 * LANES].T], axis=0)

    def to_key_major(xt):
        return jnp.concatenate([xt[0:LANES, :].T, xt[LANES:2 * LANES, :].T], axis=1)

    rowk = lax.broadcasted_iota(jnp.int32, (W, KV_WIDTH), 0)
    if carry:
        cur_slot = lax.rem(i, 2)

        @pl.when(i == 0)
        def _():
            kd[1] = jnp.zeros((N_KV_HEADS, W, LANES), BF16)
            vt[1] = jnp.zeros((N_KV_HEADS // 2, LANES, W), BF16)

        stage_kv(cur_slot, k_rot, v_new)

        @pl.when(i == nb - 1)
        def _():
            ko_ref[...] = to_channel_major(k_rot)
            vo_ref[...] = to_channel_major(v_new)
    else:
        k_prev = to_key_major(kp_ref[...])
        v_prev = to_key_major(vp_ref[...])
        kc_scr[...] = jnp.zeros((W, KV_WIDTH), F32)
        kc_scr[0:TQ, :] = k_rot
        vc_scr[...] = jnp.zeros((W, KV_WIDTH), F32)
        vc_scr[0:TQ, :] = v_new
        k_cur = kc_scr[...]
        v_cur = vc_scr[...]
        stage_kv(0, k_prev, v_prev)
        stage_kv(1, k_cur, v_cur)
        ko_ref[...] = to_channel_major(pltpu.roll(jnp.where(rowk < L, k_cur, k_prev), W - L, 0))
        vo_ref[...] = to_channel_major(pltpu.roll(jnp.where(rowk < L, v_cur, v_prev), W - L, 0))

    s_idx = lax.broadcasted_iota(jnp.int32, (W, R), 0)
    t_idx = lax.broadcasted_iota(jnp.int32, (W, R), 1) & (TQ - 1)
    cur_mask = s_idx <= t_idx
    prev_mask = s_idx > t_idx
    if carry:
        prev_mask = prev_mask & (i > 0)
        slot0_cur = cur_slot == 0
        masks = ((cur_mask & slot0_cur) | (prev_mask & jnp.logical_not(slot0_cur)),
                 (prev_mask & slot0_cur) | (cur_mask & jnp.logical_not(slot0_cur)))
    else:
        masks = (prev_mask, cur_mask)
    low = lane < HEAD_DIM
    heads_per_kv = N_Q_HEADS // N_KV_HEADS

    for kvh in range(N_KV_HEADS):
        kvp, half = divmod(kvh, 2)
        qs = []
        for qq in range(heads_per_kv // 2):
            qp = kvh * (heads_per_kv // 2) + qq
            q_rot = rope(get("q", slice(qp * LANES, (qp + 1) * LANES))) * (HEAD_DIM ** -0.5)
            qs += [jnp.where(low, q_rot, 0.0), jnp.where(low, 0.0, q_rot)]
        if heads_per_kv * TQ < R:
            qs.append(jnp.zeros((R - heads_per_kv * TQ, LANES), F32))
        q_stack = jnp.concatenate(qs, axis=0).astype(BF16)
        s = [jnp.where(masks[slot],
                       lax.dot_general(kd[slot, kvh], q_stack, (((1,), (1,)), ((), ())),
                                       preferred_element_type=F32), -jnp.inf) for slot in range(2)]
        sink = sinks_ref[kvh:kvh + 1, :]
        m = jnp.maximum(jnp.maximum(jnp.max(s[0], axis=0, keepdims=True), jnp.max(s[1], axis=0, keepdims=True)),
                        sink)
        p_un = [jnp.exp(s[slot] - m) for slot in range(2)]
        den = (jnp.sum(p_un[0], axis=0, keepdims=True) + jnp.sum(p_un[1], axis=0, keepdims=True)
               + jnp.exp(sink - m))
        o_t = (jnp.dot(vt[0, kvp], p_un[0].astype(BF16), preferred_element_type=F32)
               + jnp.dot(vt[1, kvp], p_un[1].astype(BF16), preferred_element_type=F32)) * (1.0 / den)
        for qq in range(heads_per_kv // 2):
            qp = kvh * (heads_per_kv // 2) + qq
            qcols = slice(qp * LANES, (qp + 1) * LANES)
            if carry:
                rows = slice(half * HEAD_DIM, (half + 1) * HEAD_DIM)
                o = jnp.concatenate([o_t[rows, (2 * qq) * TQ:(2 * qq + 1) * TQ],
                                     o_t[rows, (2 * qq + 1) * TQ:(2 * qq + 2) * TQ]], axis=0).T
            else:
                if qq == 0:
                    o_all = o_t.T
                oa = o_all[(2 * qq) * TQ:(2 * qq + 1) * TQ, :]
                ob = o_all[(2 * qq + 1) * TQ:(2 * qq + 2) * TQ, :]
                if half == 0:
                    ob = pltpu.roll(ob, HEAD_DIM, 1)
                else:
                    oa = pltpu.roll(oa, HEAD_DIM, 1)
                o = jnp.where(low, oa, ob)
            store_rows(qcols, o * _silu(get("g", qcols)))

    if carry:
        @pl.when(i == 0)
        def _():
            xpad2[0:SUBLANES, :] = jnp.zeros((SUBLANES, SC_WIDTH), F32)
    else:
        xpad2[0:SUBLANES, :] = sch_ref[...]

    u = get("sc") * get("sh")
    xpad2[SUBLANES:SUBLANES + TQ, :] = u
    base = SUBLANES - (SC_CONV - 1)
    conv = xpad2[base:base + TQ, :] * scw_ref[0:1, :]
    for j in range(1, SC_CONV):
        conv = conv + xpad2[base + j:base + j + TQ, :] * scw_ref[j:j + 1, :]
    ysc = (get("sb") * conv) * _silu(get("sg"))
    store_rows(slice(ATT_WIDTH, ATT_WIDTH + SC_WIDTH), ysc)

    tail = (base + L) // SUBLANES * SUBLANES
    tail_tile = xpad2[tail:tail + SUBLANES, :]
    if carry:
        @pl.when(i == nb - 1)
        def _():
            sco_ref[...] = tail_tile

        xpad2[0:SUBLANES, :] = tail_tile
    else:
        sco_ref[...] = tail_tile


def _att_decode_tile_kernel(*refs, L, parts, n_alias):
    n_in, n_out = 15, 4
    ins, outs, scratch = refs[:n_in], refs[n_in + n_alias:n_in + n_alias + n_out], refs[n_in + n_alias + n_out:]
    for s in range(parts):
        seq_ins = list(ins[:12]) + [r.at[s] for r in ins[12:]]
        seq_outs = [outs[0]] + [r.at[s] for r in outs[1:]]
        _att_kernel(*seq_ins, *seq_outs, *scratch, L=L, nb=1, carry=False, n_alias=0, row0=L * s)


def _att_mixer(proj, y_mix, sc_state_pad, k_all_prev, v_all_prev, kv_out_prev, prm, rope_tab, *, layer, depth,
               batch, seqlen):
    carry = sc_state_pad is None
    if seqlen >= CHUNK:
        L, nb = CHUNK, seqlen // CHUNK
        src = proj

        def blk(width, col):
            return pl.BlockSpec((L, width), lambda b, i: (b * nb + i, col // width))

        y_spec = pl.BlockSpec((L, ATT_WIDTH + SC_WIDTH), lambda b, i: (b * nb + i, 1))
        tq = CHUNK
    else:
        L, nb = seqlen, 1
        src = proj
        parts = SUBLANES // L
        assert parts * L == SUBLANES and batch % parts == 0

        def blk(width, col):
            return pl.BlockSpec((SUBLANES, width), lambda b, i: (b, col // width))

        y_spec = pl.BlockSpec((SUBLANES, ATT_WIDTH + SC_WIDTH), lambda b, i: (b, 1))
        tq = SUBLANES

    def const(shape):
        return pl.BlockSpec(shape, lambda b, i: (0,) * len(shape))

    cos_t, sin_t = rope_tab
    n_cols = _att_stack_cols(tq)
    sink_rows = jnp.repeat(prm["sinks"].reshape(N_KV_HEADS, N_Q_HEADS // N_KV_HEADS), tq, axis=1)
    sink_rows = jnp.pad(sink_rows, ((0, 0), (0, n_cols - sink_rows.shape[1])))
    in_specs = [blk(ATT_WIDTH, COL_Q), blk(KV_WIDTH, COL_K), blk(KV_WIDTH, COL_V), blk(ATT_WIDTH, COL_GATT),
                blk(SC_WIDTH, COL_BSC), blk(SC_WIDTH, COL_CSC), blk(SC_WIDTH, COL_HSC), blk(SC_WIDTH, COL_GSC),
                pl.BlockSpec((tq, LANES), lambda b, i: (i if carry else 0, 0)),
                pl.BlockSpec((tq, LANES), lambda b, i: (i if carry else 0, 0)),
                const((N_KV_HEADS, n_cols)), const((SC_CONV, SC_WIDTH))]
    args = [src] * 8 + [cos_t, sin_t, sink_rows, prm["sc_conv_w"]]
    scratch = [pltpu.VMEM((2, N_KV_HEADS, WINDOW, LANES), BF16),
               pltpu.VMEM((2, N_KV_HEADS // 2, LANES, WINDOW), BF16)]
    per_step = None if carry else parts
    kv_spec = pl.BlockSpec((None, per_step, KV_WIDTH, WINDOW), lambda b, i: (layer, b, 0, 0))
    sc_spec = pl.BlockSpec((per_step, SUBLANES, SC_WIDTH), lambda b, i: (b, 0, 0))
    if not carry:
        in_specs += [sc_spec, kv_spec, kv_spec]
        args += [sc_state_pad, k_all_prev, v_all_prev]
    scratch += [pltpu.VMEM((SUBLANES + tq, SC_WIDTH), F32)]
    if not carry:
        scratch += [pltpu.VMEM((tq, ATT_WIDTH), F32), pltpu.VMEM((tq, KV_WIDTH), F32),
                    pltpu.VMEM((WINDOW, KV_WIDTH), F32), pltpu.VMEM((WINDOW, KV_WIDTH), F32)]
    aliases = {len(args): 0}
    in_specs += [pl.BlockSpec(memory_space=pl.ANY)]
    args += [y_mix]
    if kv_out_prev is not None:
        for out_idx, buf in zip((1, 2), kv_out_prev):
            aliases[len(args)] = out_idx
            in_specs += [pl.BlockSpec(memory_space=pl.ANY)]
            args += [buf]
    if carry:
        body = functools.partial(_att_kernel, L=L, nb=nb, carry=True, n_alias=len(aliases))
    else:
        body = functools.partial(_att_decode_tile_kernel, L=L, parts=parts, n_alias=len(aliases))
    y, ko, vo, sco = pl.pallas_call(
        body,
        grid=(batch, nb) if carry else (batch // parts, 1),
        in_specs=in_specs,
        out_specs=[y_spec, kv_spec, kv_spec, sc_spec],
        out_shape=[jax.ShapeDtypeStruct(y_mix.shape, y_mix.dtype),
                   jax.ShapeDtypeStruct((depth, batch, KV_WIDTH, WINDOW), F32),
                   jax.ShapeDtypeStruct((depth, batch, KV_WIDTH, WINDOW), F32),
                   jax.ShapeDtypeStruct((batch, SUBLANES, SC_WIDTH), F32)],
        scratch_shapes=scratch,
        input_output_aliases=aliases,
        compiler_params=_cparams(2),
        name="att_prompt" if carry else "att_decode",
    )(*args)
    base = SUBLANES - (SC_CONV - 1)
    off = base + L - (base + L) // SUBLANES * SUBLANES
    return y, (ko, vo), sco[:, off:off + SC_CONV - 1, :]


OUT_TM = 1024
OUT_KT = 1024
OUT_NT = 512
N_KSTEPS = D_MIX // OUT_KT
N_NSTEPS = D_MODEL // OUT_NT


def _outproj_kernel(y_ref, wo_ref, x_ref, npost_ref, p_ref, pp_ref, gw_ref, o_ref, acc_ref, xnb_ref):
    k = pl.program_id(1)

    @pl.when(k == 0)
    def _():
        acc_ref[...] = jnp.dot(y_ref[...].astype(BF16), wo_ref[...], preferred_element_type=F32)

    @pl.when((k > 0) & (k < N_KSTEPS))
    def _():
        acc_ref[...] += jnp.dot(y_ref[...].astype(BF16), wo_ref[...], preferred_element_type=F32)

    @pl.when(k == N_KSTEPS - 1)
    def _():
        rows = min(256, acc_ref.shape[0])
        for r in range(acc_ref.shape[0] // rows):
            rs = slice(r * rows, (r + 1) * rows)
            mix = acc_ref[rs, :]
            ms = jnp.mean(mix * mix, axis=-1, keepdims=True)
            xn = x_ref[rs, :] + (mix * lax.rsqrt(ms + EPS)) * npost_ref[...]
            acc_ref[rs, :] = xn
            xnb_ref[rs, :] = xn.astype(BF16)

    for j in range(N_NSTEPS):
        @pl.when(k == N_KSTEPS + j)
        def _():
            gate = _sigmoid(jnp.dot(xnb_ref[...], gw_ref[...], preferred_element_type=F32))
            e = jnp.dot(p_ref[...].astype(BF16), pp_ref[...], preferred_element_type=F32)
            o_ref[...] = acc_ref[:, j * OUT_NT:(j + 1) * OUT_NT] + gate * e


def _out_proj(y_mix, x2d, p_all, prm, *, layer):
    t = x2d.shape[0]
    tm = min(OUT_TM, t)

    def nstep(k):
        return jnp.maximum(k - N_KSTEPS, 0)

    n_row_tiles = t // tm

    def row_tile(i, k, first_unused):
        return jnp.minimum(i + (k >= first_unused).astype(jnp.int32), n_row_tiles - 1)

    return pl.pallas_call(
        _outproj_kernel,
        grid=(n_row_tiles, N_KSTEPS + N_NSTEPS),
        in_specs=[
            pl.BlockSpec((tm, OUT_KT), lambda i, k: (row_tile(i, k, N_KSTEPS + 1),
                                                     jnp.where(k > N_KSTEPS, 0, jnp.minimum(k, N_KSTEPS - 1)))),
            pl.BlockSpec((OUT_KT, D_MODEL),
                         lambda i, k: (jnp.where(k > N_KSTEPS + 1, 0, jnp.minimum(k, N_KSTEPS - 1)), 0)),
            pl.BlockSpec((tm, D_MODEL), lambda i, k: (row_tile(i, k, N_KSTEPS), 0)),
            pl.BlockSpec((1, D_MODEL), lambda i, k: (0, 0)),
            pl.BlockSpec((None, tm, PLE_DIM), lambda i, k: (layer, i, 0)),
            pl.BlockSpec((PLE_DIM, OUT_NT), lambda i, k: (0, nstep(k))),
            pl.BlockSpec((D_MODEL, OUT_NT), lambda i, k: (0, nstep(k))),
        ],
        out_specs=pl.BlockSpec((tm, OUT_NT), lambda i, k: (i, nstep(k))),
        out_shape=jax.ShapeDtypeStruct((t, D_MODEL), F32),
        scratch_shapes=[pltpu.VMEM((tm, D_MODEL), F32),
                        pltpu.VMEM((tm, D_MODEL), BF16)],
        compiler_params=_cparams(2),
        name="out_proj",
    )(y_mix.reshape(t, D_MIX), prm["w_out"], x2d, prm["norm_post"], p_all, prm["ple_proj"], prm["ple_gate"])


_W_IN_SRC = {}
_o = 0
for _name, _size in (("z", SSD_WIDTH), ("xbc", SSD_CONV_CH), ("dt", SSD_HEADS), ("q", ATT_WIDTH), ("k", KV_WIDTH),
                     ("v", KV_WIDTH), ("g_att", ATT_WIDTH), ("b_sc", SC_WIDTH), ("c_sc", SC_WIDTH),
                     ("h_sc", SC_WIDTH), ("g_sc", SC_WIDTH)):
    _W_IN_SRC[_name] = (_o, _size)
    _o += _size
N_IN = _o
_W_IN_DST = {"z": COL_Z, "xbc": COL_X, "q": COL_Q, "g_att": COL_GATT, "b_sc": COL_BSC, "c_sc": COL_CSC,
             "h_sc": COL_HSC, "g_sc": COL_GSC, "k": COL_K, "v": COL_V}
REGROUP_COLS = 256
REGROUP_UNIT = 32


def _regroup_tables(layer):
    starts, valid = [], []
    for j in range(N_PROJ // REGROUP_COLS):
        dst = j * REGROUP_COLS
        src, n = 0, 0
        for name, seg_dst in _W_IN_DST.items():
            seg_src, width = _W_IN_SRC[name]
            if seg_dst <= dst < seg_dst + width:
                src, n = seg_src + dst - seg_dst, REGROUP_COLS
        if dst == COL_DT:
            src, n = _W_IN_SRC["dt"]
        assert src % REGROUP_UNIT == 0 and src + REGROUP_COLS <= N_IN
        starts.append((layer * N_IN + src) // REGROUP_UNIT)
        valid.append(n)
    return jnp.asarray(starts, jnp.int32), jnp.asarray(valid, jnp.int32)


def _regroup_kernel(starts_ref, valid_ref, w_ref, o_ref):
    col = lax.broadcasted_iota(jnp.int32, (D_MODEL, REGROUP_COLS), 1)
    o_ref[...] = jnp.where(col < valid_ref[pl.program_id(0)], w_ref[...].T, 0.0).astype(BF16)


def _regroup_w_in(w_cols, layer):
    starts, valid = _regroup_tables(layer)
    return pl.pallas_call(
        _regroup_kernel,
        grid_spec=pltpu.PrefetchScalarGridSpec(
            num_scalar_prefetch=2,
            grid=(N_PROJ // REGROUP_COLS,),
            in_specs=[pl.BlockSpec((pl.Element(REGROUP_COLS), pl.Element(D_MODEL)),
                                   lambda j, s, v: (s[j] * REGROUP_UNIT, 0))],
            out_specs=pl.BlockSpec((D_MODEL, REGROUP_COLS), lambda j, s, v: (0, j))),
        out_shape=jax.ShapeDtypeStruct((D_MODEL, N_PROJ), BF16),
        compiler_params=_cparams(1),
        name="regroup_w_in",
    )(starts, valid, w_cols)


def _lane_pad(v):
    return jnp.pad(v, (0, LANES - v.shape[0])).reshape(1, LANES)


def _layer_params(i, w_in, w_out, norm_pre, norm_post, ssd_conv_w, ssd_conv_b, ssd_a_log, ssd_dt_bias, ssd_d,
                  ssd_norm, attn_sinks, sc_conv_w, ple_proj, ple_gate, expand):
    return {
        "w_in": _regroup_w_in(w_in, i),
        "w_out": w_out[i].astype(BF16),
        "norm_pre": norm_pre[i].reshape(1, D_MODEL),
        "norm_post": norm_post[i].reshape(1, D_MODEL),
        "ssd_conv_w": ssd_conv_w[i],
        "ssd_conv_b": ssd_conv_b[i].reshape(1, SSD_CONV_CH),
        "a_log": _lane_pad(ssd_a_log[i]),
        "dt_bias": _lane_pad(ssd_dt_bias[i]),
        "d_skip": jnp.repeat(ssd_d[i], SSD_HEAD_DIM).reshape(1, SSD_WIDTH),
        "ssd_norm": ssd_norm[i].reshape(1, SSD_WIDTH),
        "sinks": attn_sinks[i],
        "sc_conv_w": sc_conv_w[i],
        "ple_proj": ple_proj[i].astype(BF16),
        "ple_gate": ple_gate[i].astype(BF16),
        "expand": expand,
    }


def _rope_tables(pos0, n):
    half = HEAD_DIM // 2
    inv_freq = jnp.exp(-math.log(ROPE_THETA) * jnp.arange(half, dtype=F32) * (2.0 / HEAD_DIM))
    pos = pos0 + jnp.arange(n, dtype=jnp.int32)
    ang = pos.astype(F32)[:, None] * inv_freq[None, :]
    cos, sin = jnp.cos(ang), jnp.sin(ang)
    reps = LANES // HEAD_DIM
    return (jnp.tile(jnp.concatenate([cos, cos], axis=1), (1, reps)),
            jnp.tile(jnp.concatenate([-sin, sin], axis=1), (1, reps)))


def _layer(x2d, p_all, prm, rope_tab, carried, *, layer, depth, batch, seqlen, conv_state, ssm_all, k_all, v_all,
           sc_state):
    decode = conv_state is not None
    ssm_out_prev, kv_out_prev = carried
    proj = _in_proj(x2d, prm["norm_pre"], prm["w_in"])
    if decode:
        sc_pad = jnp.pad(sc_state, ((0, 0), (SUBLANES - (SC_CONV - 1), 0), (0, 0)))
        y_mix, conv_new, ssm_out = _ssd_decode_mixer(proj, conv_state, ssm_all, ssm_out_prev, prm, layer=layer,
                                                     depth=depth, batch=batch, seqlen=seqlen)
    else:
        sc_pad = None
        y_mix, conv_new, ssm_out = _ssd_mixer(proj, ssm_out_prev, prm, layer=layer, depth=depth, batch=batch,
                                              seqlen=seqlen)
    y_mix, kv_out, sc_new = _att_mixer(proj, y_mix, sc_pad, k_all, v_all, kv_out_prev, prm, rope_tab, layer=layer,
                                       depth=depth, batch=batch, seqlen=seqlen)
    x_new = _out_proj(y_mix, x2d, p_all, prm, layer=layer)
    return x_new, conv_new, sc_new, (ssm_out, kv_out)


def kernel(x_prompt, x_sample, p_prompt, p_sample, state_ssd_conv, state_ssm, cache_k, cache_v, state_sc_conv,
           w_in, w_out, norm_pre, norm_post, ssd_conv_w, ssd_conv_b, ssd_a_log, ssd_dt_bias, ssd_d, ssd_norm,
           attn_sinks, sc_conv_w, ple_proj, ple_gate):
    bp, lp, _ = x_prompt.shape
    bs, ls, _ = x_sample.shape
    depth = w_in.shape[0]
    assert lp % CHUNK == 0 and ls + SSD_CONV - 1 <= DEC_TILE and bs % DEC_G == 0 and cache_k.shape[2] == WINDOW
    head = jnp.arange(LANES, dtype=jnp.int32)[:, None]
    chan = jnp.arange(SSD_WIDTH, dtype=jnp.int32)[None, :]
    expand = (chan // SSD_HEAD_DIM == head).astype(BF16)
    rope_p = _rope_tables(0, lp)
    rope_s = _rope_tables(PAST_LEN, SUBLANES)
    yp = x_prompt.reshape(bp * lp, D_MODEL)
    ys = x_sample.reshape(bs * ls, D_MODEL)
    pp_all = p_prompt.reshape(depth, bp * lp, PLE_DIM)
    ps_all = p_sample.reshape(depth, bs * ls, PLE_DIM)
    ssm_all = state_ssm.reshape(depth, bs, SSD_WIDTH, SSD_STATE)
    w_cols = jnp.swapaxes(w_in, 1, 2).reshape(depth * N_IN, D_MODEL)
    k_all = cache_k.transpose(0, 1, 3, 4, 2).reshape(depth, bs, KV_WIDTH, WINDOW)
    v_all = cache_v.transpose(0, 1, 3, 4, 2).reshape(depth, bs, KV_WIDTH, WINDOW)
    carried_p = carried_s = (None, None)
    conv_p, sc_p, conv_s, sc_s = [], [], [], []
    for i in range(depth):
        prm = _layer_params(i, w_cols, w_out, norm_pre, norm_post, ssd_conv_w, ssd_conv_b, ssd_a_log, ssd_dt_bias,
                            ssd_d, ssd_norm, attn_sinks, sc_conv_w, ple_proj, ple_gate, expand)
        yp, cv, sc, carried_p = _layer(yp, pp_all, prm, rope_p, carried_p, layer=i, depth=depth, batch=bp, seqlen=lp,
                                       conv_state=None, ssm_all=None, k_all=None, v_all=None, sc_state=None)
        conv_p.append(cv)
        sc_p.append(sc)
        ys, cv, sc, carried_s = _layer(ys, ps_all, prm, rope_s, carried_s, layer=i, depth=depth, batch=bs, seqlen=ls,
                                       conv_state=state_ssd_conv[i], ssm_all=ssm_all, k_all=k_all, v_all=v_all,
                                       sc_state=state_sc_conv[i])
        conv_s.append(cv)
        sc_s.append(sc)

    def states(conv, carried, sc, batch):
        ssm_out, (k_out, v_out) = carried
        return (jnp.stack(conv), ssm_out.reshape(depth, batch, SSD_HEADS, SSD_HEAD_DIM, SSD_STATE),
                k_out.reshape(depth, batch, N_KV_HEADS, HEAD_DIM, WINDOW).transpose(0, 1, 4, 2, 3),
                v_out.reshape(depth, batch, N_KV_HEADS, HEAD_DIM, WINDOW).transpose(0, 1, 4, 2, 3), jnp.stack(sc))

    return (yp.reshape(bp, lp, D_MODEL), ys.reshape(bs, ls, D_MODEL),
            *states(conv_p, carried_p, sc_p, bp), *states(conv_s, carried_s, sc_s, bs))
```

```python
import functools
import math

import jax
import jax.numpy as jnp
from jax import lax
from jax.experimental import pallas as pl
from jax.experimental.pallas import tpu as pltpu

F32 = jnp.float32
BF16 = jnp.bfloat16

D_MODEL = 2048
D_MIX = 2 * D_MODEL
SSD_WIDTH = D_MIX // 2
SSD_HEAD_DIM = 64
SSD_HEADS = SSD_WIDTH // SSD_HEAD_DIM
SSD_GROUPS = 4
SSD_STATE = 128
SSD_CONV = 4
SSD_CONV_CH = SSD_WIDTH + 2 * SSD_GROUPS * SSD_STATE
ATT_WIDTH = D_MIX // 4
HEAD_DIM = 64
N_Q_HEADS = ATT_WIDTH // HEAD_DIM
N_KV_HEADS = 4
KV_WIDTH = N_KV_HEADS * HEAD_DIM
WINDOW = 128
ROPE_THETA = 10000.0
SC_WIDTH = D_MIX // 4
SC_CONV = 3
PLE_DIM = 256
EPS = 1e-6
PAST_LEN = 8192

LANES = 128
SUBLANES = 8
CHUNK = 128
GROUP_W = SSD_WIDTH // SSD_GROUPS
HEADS_PER_GROUP = SSD_HEADS // SSD_GROUPS

COL_Z = 0
COL_X = 2048
COL_B = 4096
COL_C = 4608
COL_Q = 5120
COL_GATT = 6144
COL_BSC = 7168
COL_CSC = 8192
COL_HSC = 9216
COL_GSC = 10240
COL_K = 11264
COL_V = 11520
COL_DT = 11776
N_PROJ = 12288

VMEM_LIMIT = 56 * 1024 * 1024


def _cparams(ndims):
    return pltpu.CompilerParams(dimension_semantics=("arbitrary",) * ndims, vmem_limit_bytes=VMEM_LIMIT)


def _silu(x):
    h = 0.5 * x
    return h + h * jnp.tanh(h)


def _sigmoid(x):
    return 0.5 + 0.5 * jnp.tanh(0.5 * x)


def _inproj_kernel(x_ref, nw_ref, w_ref, o_ref, h_ref, *, tm, row_chunk, switch):
    i = pl.program_id(0)
    j = pl.program_id(1)
    n_chunks = tm // row_chunk

    def norm_chunk(r, slot):
        rows = pl.ds(pl.multiple_of(r * row_chunk, row_chunk), row_chunk)
        x = x_ref[rows, :]
        ms = jnp.mean(x * x, axis=-1, keepdims=True)
        h_ref[slot, rows, :] = ((x * lax.rsqrt(ms + EPS)) * nw_ref[...]).astype(BF16)

    @pl.when((i == 0) & (j == 0))
    def _():
        def body(r, carry):
            norm_chunk(r, 0)
            return carry

        lax.fori_loop(0, n_chunks, body, 0)

    cur = lax.rem(i, 2)
    o_ref[...] = jnp.dot(h_ref[cur], w_ref[...], preferred_element_type=F32)
    norm_chunk(lax.rem(j + n_chunks - lax.rem(switch, n_chunks), n_chunks), 1 - cur)


def _in_proj(x2d, nw, w_bf16):
    t = x2d.shape[0]
    tm = min(1024, t)
    tn = 1024
    row_chunk = 128
    n_tiles, n_steps = t // tm, N_PROJ // tn
    switch = n_steps - tm // row_chunk
    assert switch >= 1
    return pl.pallas_call(
        functools.partial(_inproj_kernel, tm=tm, row_chunk=row_chunk, switch=switch),
        grid=(n_tiles, n_steps),
        in_specs=[
            pl.BlockSpec((tm, D_MODEL),
                         lambda i, j: (jnp.minimum(i + (j >= switch).astype(jnp.int32), n_tiles - 1), 0)),
            pl.BlockSpec((1, D_MODEL), lambda i, j: (0, 0)),
            pl.BlockSpec((D_MODEL, tn), lambda i, j: (0, j)),
        ],
        out_specs=pl.BlockSpec((tm, tn), lambda i, j: (i, j)),
        out_shape=jax.ShapeDtypeStruct((t, N_PROJ), F32),
        scratch_shapes=[pltpu.VMEM((2, tm, D_MODEL), BF16)],
        compiler_params=_cparams(2),
        name="in_proj",
    )(x2d, nw, w_bf16)


DEC_G = 4
DEC_TILE = SUBLANES


def _softplus(x):
    return jnp.maximum(x, 0.0) + jnp.log1p(jnp.exp(-jnp.abs(x)))


def _ssd_conv_silu(xpad, cw_ref, cbias_ref, xs_scr, bm_scr, cm_scr):
    cwid = 256
    base = SUBLANES - (SSD_CONV - 1)
    for cb in range(SSD_CONV_CH // cwid):
        cols = slice(cb * cwid, (cb + 1) * cwid)
        acc = xpad[base:base + CHUNK, cols] * cw_ref[0:1, cols]
        for j in range(1, SSD_CONV):
            acc = acc + xpad[base + j:base + j + CHUNK, cols] * cw_ref[j:j + 1, cols]
        act = _silu(acc + cbias_ref[:, cols])
        lo = cb * cwid
        if lo < SSD_WIDTH:
            xs_scr[:, lo:lo + cwid] = act
        elif lo < SSD_WIDTH + GROUP_W:
            bm_scr[:, lo - SSD_WIDTH:lo - SSD_WIDTH + cwid] = act.astype(BF16)
        else:
            o = lo - SSD_WIDTH - GROUP_W
            cm_scr[:, o:o + cwid] = act.astype(BF16)


def _ssd_expand(v, e_mat):
    hi = v.astype(BF16)
    lo = (v - hi.astype(F32)).astype(BF16)
    return jnp.dot(hi, e_mat, preferred_element_type=F32) + jnp.dot(lo, e_mat, preferred_element_type=F32)


def _ssd_diag(a_cum, dt, allowed, xs_scr, bm_scr, cm_scr, y_scr, after_group=None):
    Q = CHUNK
    a_cum_t = a_cum.T
    dt_t = dt.T
    lane = lax.broadcasted_iota(jnp.int32, (Q, LANES), 1)
    low_half = lane < SSD_HEAD_DIM
    for g in range(SSD_GROUPS):
        gcols = slice(g * SSD_STATE, (g + 1) * SSD_STATE)
        cbm = lax.dot_general(cm_scr[:, gcols], bm_scr[:, gcols], (((1,), (1,)), ((), ())),
                              preferred_element_type=F32)
        for p in range(HEADS_PER_GROUP // 2):
            pair = g * (HEADS_PER_GROUP // 2) + p
            ms = []
            for h in (2 * pair, 2 * pair + 1):
                seg = jnp.broadcast_to(a_cum[:, h:h + 1], (Q, Q)) - a_cum_t[h:h + 1, :]
                dec = jnp.exp(jnp.where(allowed, seg, -jnp.inf))
                ms.append((cbm * dec * dt_t[h:h + 1, :]).astype(BF16))
            lhs = jnp.concatenate(ms, axis=1)
            xp = xs_scr[:, pair * LANES:(pair + 1) * LANES]
            rhs = jnp.concatenate([jnp.where(low_half, xp, 0.0).astype(BF16),
                                   jnp.where(low_half, 0.0, xp).astype(BF16)], axis=0)
            y_scr[:, pair * LANES:(pair + 1) * LANES] = jnp.dot(lhs, rhs, preferred_element_type=F32)
        if after_group is not None:
            after_group(g)


def _ssd_gate_norm(yg, zg, nw):
    yg = yg * _silu(zg)
    msq = jnp.mean(yg * yg, axis=-1, keepdims=True)
    return (yg * lax.rsqrt(msq + EPS)) * nw


def _ssd_decode_kernel(*refs, L, n_alias):
    Q = CHUNK
    n_in = 14
    refs = refs[:n_in] + refs[n_in + n_alias:]
    (z_ref, x_ref, b_ref, c_ref, dt_ref, cs_ref, ss_ref, cw_ref, cbias_ref, alog_ref, dtb_ref, dsk_ref,
     nw_ref, e_ref, y_ref, cso_ref, sso_ref,
     xpad, xs_scr, bm_scr, cm_scr, y_scr, ea_scr, we_scr, zp_scr, dtp_scr) = refs
    t0 = DEC_TILE - L
    used = DEC_G * DEC_TILE

    xpad[...] = jnp.zeros(xpad.shape, F32)
    for g in range(DEC_G):
        r0 = SUBLANES + DEC_TILE * g
        tok = slice(L * g, L * (g + 1))
        xpad[r0:r0 + DEC_TILE, :] = cs_ref[g]
        xpad[r0 + t0:r0 + DEC_TILE, 0:SSD_WIDTH] = x_ref[tok, :]
        xpad[r0 + t0:r0 + DEC_TILE, SSD_WIDTH:SSD_WIDTH + GROUP_W] = b_ref[tok, :]
        xpad[r0 + t0:r0 + DEC_TILE, SSD_WIDTH + GROUP_W:SSD_CONV_CH] = c_ref[tok, :]
    for g in range(DEC_G):
        r0 = SUBLANES + DEC_TILE * g
        cso_ref[g] = xpad[r0:r0 + DEC_TILE, :]
    _ssd_conv_silu(xpad, cw_ref, cbias_ref, xs_scr, bm_scr, cm_scr)

    row = lax.broadcasted_iota(jnp.int32, (Q, LANES), 0)
    in_tile = row & (DEC_TILE - 1)
    dtp_scr[...] = jnp.zeros((Q, LANES), F32)
    zp_scr[...] = jnp.zeros(zp_scr.shape, F32)
    for g in range(DEC_G):
        dtp_scr[DEC_TILE * g + t0:DEC_TILE * (g + 1), :] = dt_ref[L * g:L * (g + 1), :]
        zp_scr[DEC_TILE * g + t0:DEC_TILE * (g + 1), :] = z_ref[L * g:L * (g + 1), :]
    dt = jnp.where((in_tile >= t0) & (row < used), _softplus(dtp_scr[...] + dtb_ref[...]), 0.0)
    a_cum = dt * (-jnp.exp(alog_ref[...]))
    k = 1
    while k < DEC_TILE:
        a_cum = a_cum + jnp.where(in_tile >= k, pltpu.roll(a_cum, k, 0), 0.0)
        k *= 2
    tot = jnp.where(in_tile == DEC_TILE - 1, a_cum, 0.0)
    k = 1
    while k < DEC_TILE:
        tot = tot + pltpu.roll(tot, Q - k, 0)
        k *= 2
    e_mat = e_ref[...]
    ea_scr[...] = _ssd_expand(jnp.exp(a_cum), e_mat)
    we_scr[...] = _ssd_expand(jnp.exp(tot - a_cum) * dt, e_mat)
    cd_t = jnp.exp(tot).T

    lrow = lax.broadcasted_iota(jnp.int32, (Q, Q), 0)
    scol = lax.broadcasted_iota(jnp.int32, (Q, Q), 1)
    allowed = (lrow >= scol) & ((lrow // DEC_TILE) == (scol // DEC_TILE))
    _ssd_diag(a_cum, dt, allowed, xs_scr, bm_scr, cm_scr, y_scr)

    pair_rows = 2 * DEC_TILE
    first_of_pair = lax.broadcasted_iota(jnp.int32, (pair_rows, GROUP_W), 0) < DEC_TILE
    for g in range(SSD_GROUPS):
        gcols = slice(g * SSD_STATE, (g + 1) * SSD_STATE)
        wcols = slice(g * GROUP_W, (g + 1) * GROUP_W)
        for sp in range(DEC_G // 2):
            rows = slice(sp * pair_rows, (sp + 1) * pair_rows)
            offs = [lax.dot_general(cm_scr[rows, gcols], ss_ref[2 * sp + j, wcols, :].astype(BF16),
                                    (((1,), (1,)), ((), ())), preferred_element_type=F32) for j in range(2)]
            y_off = jnp.where(first_of_pair, offs[0], offs[1])
            y_scr[rows, wcols] = y_scr[rows, wcols] + y_off * ea_scr[rows, wcols]
        yg = y_scr[0:used, wcols] + dsk_ref[:, wcols] * xs_scr[0:used, wcols]
        out = _ssd_gate_norm(yg, zp_scr[:, wcols], nw_ref[:, wcols])
        for s in range(DEC_G):
            y_ref[L * s:L * (s + 1), wcols] = out[DEC_TILE * s + t0:DEC_TILE * (s + 1)].astype(y_ref.dtype)

    row_seq = lax.broadcasted_iota(jnp.int32, (Q, SSD_STATE), 0) // DEC_TILE
    for g in range(SSD_GROUPS):
        wcols = slice(g * GROUP_W, (g + 1) * GROUP_W)
        gcols = slice(g * SSD_STATE, (g + 1) * SSD_STATE)
        xw_t = (xs_scr[:, wcols] * we_scr[:, wcols]).T.astype(BF16)
        bm_g = bm_scr[:, gcols]
        for s in range(DEC_G):
            st = jnp.dot(xw_t, jnp.where(row_seq == s, bm_g, jnp.zeros_like(bm_g)),
                         preferred_element_type=F32)
            last = DEC_TILE * (s + 1) - 1
            cd_rows = jnp.broadcast_to(cd_t[:, last:last + 1], (LANES, LANES))
            for hh in range(HEADS_PER_GROUP):
                h = g * HEADS_PER_GROUP + hh
                r0 = h * SSD_HEAD_DIM
                sso_ref[s, r0:r0 + SSD_HEAD_DIM, :] = (ss_ref[s, r0:r0 + SSD_HEAD_DIM, :] * cd_rows[h:h + 1, :]
                                                       + st[hh * SSD_HEAD_DIM:(hh + 1) * SSD_HEAD_DIM, :])


def _ssd_kernel(*refs, nc, n_alias):
    Q = CHUNK
    n_in = 12
    refs = refs[:n_in] + refs[n_in + n_alias:]
    (z_ref, x_ref, b_ref, c_ref, dt_ref, cw_ref, cbias_ref, alog_ref, dtb_ref, dsk_ref, nw_ref, e_ref,
     y_ref, cso_ref, sso_ref,
     xpad, xs_scr, bm_scr, cm_scr, y_scr, ea_scr, we_scr) = refs
    c = pl.program_id(1)

    @pl.when(c == 0)
    def _init():
        sso_ref[...] = jnp.zeros(sso_ref.shape, F32)
        xpad[0:SUBLANES, :] = jnp.zeros((SUBLANES, SSD_CONV_CH), F32)

    xpad[SUBLANES:SUBLANES + Q, 0:SSD_WIDTH] = x_ref[...]
    xpad[SUBLANES:SUBLANES + Q, SSD_WIDTH:SSD_WIDTH + GROUP_W] = b_ref[...]
    xpad[SUBLANES:SUBLANES + Q, SSD_WIDTH + GROUP_W:SSD_CONV_CH] = c_ref[...]
    _ssd_conv_silu(xpad, cw_ref, cbias_ref, xs_scr, bm_scr, cm_scr)

    tail_tile = xpad[Q:Q + SUBLANES, :]

    @pl.when(c == nc - 1)
    def _():
        cso_ref[...] = tail_tile

    xpad[0:SUBLANES, :] = tail_tile

    row = lax.broadcasted_iota(jnp.int32, (Q, LANES), 0)
    dt = _softplus(dt_ref[...] + dtb_ref[...])
    a_cum = dt * (-jnp.exp(alog_ref[...]))
    k = 1
    while k < Q:
        a_cum = a_cum + jnp.where(row >= k, pltpu.roll(a_cum, k, 0), 0.0)
        k *= 2
    a_last = a_cum[Q - 1:Q, :]
    e_mat = e_ref[...]
    ea_scr[...] = _ssd_expand(jnp.exp(a_cum), e_mat)
    we_scr[...] = _ssd_expand(jnp.exp(a_last - a_cum) * dt, e_mat)
    cd_rows = jnp.broadcast_to(jnp.exp(a_cum.T[:, Q - 1:Q]), (LANES, LANES))

    def carried_state_term(g):
        gcols = slice(g * SSD_STATE, (g + 1) * SSD_STATE)
        wcols = slice(g * GROUP_W, (g + 1) * GROUP_W)
        s_g = sso_ref[wcols, :].astype(BF16)
        y_off = lax.dot_general(cm_scr[:, gcols], s_g, (((1,), (1,)), ((), ())), preferred_element_type=F32)
        y_scr[:, wcols] = y_scr[:, wcols] + y_off * ea_scr[:, wcols] + dsk_ref[:, wcols] * xs_scr[:, wcols]

    causal = lax.broadcasted_iota(jnp.int32, (Q, Q), 0) >= lax.broadcasted_iota(jnp.int32, (Q, Q), 1)
    _ssd_diag(a_cum, dt, causal, xs_scr, bm_scr, cm_scr, y_scr, after_group=carried_state_term)

    for g in range(SSD_GROUPS):
        wcols = slice(g * GROUP_W, (g + 1) * GROUP_W)
        y_ref[:, wcols] = _ssd_gate_norm(y_scr[:, wcols], z_ref[:, wcols], nw_ref[:, wcols]).astype(y_ref.dtype)

    for g in range(SSD_GROUPS):
        wcols = slice(g * GROUP_W, (g + 1) * GROUP_W)
        gcols = slice(g * SSD_STATE, (g + 1) * SSD_STATE)
        xw_t = (xs_scr[:, wcols] * we_scr[:, wcols]).T.astype(BF16)
        st = jnp.dot(xw_t, bm_scr[:, gcols], preferred_element_type=F32)
        for hh in range(HEADS_PER_GROUP):
            h = g * HEADS_PER_GROUP + hh
            r0 = h * SSD_HEAD_DIM
            sso_ref[r0:r0 + SSD_HEAD_DIM, :] = (sso_ref[r0:r0 + SSD_HEAD_DIM, :] * cd_rows[h:h + 1, :]
                                                + st[hh * SSD_HEAD_DIM:(hh + 1) * SSD_HEAD_DIM, :])


def _ssd_specs_common(prm):
    shapes = [(SSD_CONV, SSD_CONV_CH), (1, SSD_CONV_CH), (1, LANES), (1, LANES), (1, SSD_WIDTH), (1, SSD_WIDTH),
              (LANES, SSD_WIDTH)]
    args = [prm["ssd_conv_w"], prm["ssd_conv_b"], prm["a_log"], prm["dt_bias"], prm["d_skip"], prm["ssd_norm"],
            prm["expand"]]
    scratch = [pltpu.VMEM((SUBLANES + CHUNK, SSD_CONV_CH), F32),
               pltpu.VMEM((CHUNK, SSD_WIDTH), F32),
               pltpu.VMEM((CHUNK, GROUP_W), BF16),
               pltpu.VMEM((CHUNK, GROUP_W), BF16),
               pltpu.VMEM((CHUNK, SSD_WIDTH), F32),
               pltpu.VMEM((CHUNK, SSD_WIDTH), F32),
               pltpu.VMEM((CHUNK, SSD_WIDTH), F32)]
    return shapes, args, scratch


def _ssd_mixer(proj, ssm_out_prev, prm, *, layer, depth, batch, seqlen):
    nc = seqlen // CHUNK

    def blk(width, col):
        return pl.BlockSpec((CHUNK, width), lambda b, c: (b * nc + c, col // width))

    def const(shape):
        return pl.BlockSpec(shape, lambda b, c: (0,) * len(shape))

    state_spec = pl.BlockSpec((None, None, SSD_WIDTH, SSD_STATE), lambda b, c: (layer, b, 0, 0))
    shapes, cargs, scratch = _ssd_specs_common(prm)
    in_specs = [blk(SSD_WIDTH, COL_Z), blk(SSD_WIDTH, COL_X), blk(GROUP_W, COL_B), blk(GROUP_W, COL_C),
                blk(LANES, COL_DT)] + [const(s) for s in shapes]
    args = [proj] * 5 + cargs
    aliases = {}
    if ssm_out_prev is not None:
        aliases[len(args)] = 2
        in_specs += [pl.BlockSpec(memory_space=pl.ANY)]
        args += [ssm_out_prev]
    y, cso, sso = pl.pallas_call(
        functools.partial(_ssd_kernel, nc=nc, n_alias=len(aliases)),
        grid=(batch, nc),
        in_specs=in_specs,
        out_specs=[pl.BlockSpec((CHUNK, SSD_WIDTH), lambda b, c: (b * nc + c, 0)),
                   pl.BlockSpec((None, SUBLANES, SSD_CONV_CH), lambda b, c: (b, 0, 0)),
                   state_spec],
        out_shape=[jax.ShapeDtypeStruct((batch * seqlen, D_MIX), BF16),
                   jax.ShapeDtypeStruct((batch, SUBLANES, SSD_CONV_CH), F32),
                   jax.ShapeDtypeStruct((depth, batch, SSD_WIDTH, SSD_STATE), F32)],
        scratch_shapes=scratch,
        input_output_aliases=aliases,
        compiler_params=_cparams(2),
        name="ssd_prompt",
    )(*args)
    return y, cso[:, SUBLANES - (SSD_CONV - 1):, :], sso


def _ssd_decode_mixer(proj, conv_state, ssm_all, ssm_out_prev, prm, *, layer, depth, batch, seqlen):
    t0 = DEC_TILE - seqlen
    hist = SSD_CONV - 1
    conv_pad = jnp.pad(conv_state, ((0, 0), (t0 - hist, DEC_TILE - t0), (0, 0)))
    rows = DEC_G * seqlen

    def blk(width, col):
        return pl.BlockSpec((rows, width), lambda s: (s, col // width))

    def const(shape):
        return pl.BlockSpec(shape, lambda s: (0,) * len(shape))

    state_spec = pl.BlockSpec((None, DEC_G, SSD_WIDTH, SSD_STATE), lambda s: (layer, s, 0, 0))
    shapes, cargs, scratch = _ssd_specs_common(prm)
    in_specs = [blk(SSD_WIDTH, COL_Z), blk(SSD_WIDTH, COL_X), blk(GROUP_W, COL_B), blk(GROUP_W, COL_C),
                blk(LANES, COL_DT),
                pl.BlockSpec((DEC_G, DEC_TILE, SSD_CONV_CH), lambda s: (s, 0, 0)), state_spec]
    in_specs += [const(s) for s in shapes]
    args = [proj] * 5 + [conv_pad, ssm_all] + cargs
    aliases = {}
    if ssm_out_prev is not None:
        aliases[len(args)] = 2
        in_specs += [pl.BlockSpec(memory_space=pl.ANY)]
        args += [ssm_out_prev]
    scratch += [pltpu.VMEM((DEC_G * DEC_TILE, SSD_WIDTH), F32), pltpu.VMEM((CHUNK, LANES), F32)]
    y, cso, sso = pl.pallas_call(
        functools.partial(_ssd_decode_kernel, L=seqlen, n_alias=len(aliases)),
        grid=(batch // DEC_G,),
        in_specs=in_specs,
        out_specs=[pl.BlockSpec((rows, SSD_WIDTH), lambda s: (s, 0)),
                   pl.BlockSpec((DEC_G, DEC_TILE, SSD_CONV_CH), lambda s: (s, 0, 0)),
                   state_spec],
        out_shape=[jax.ShapeDtypeStruct((batch * seqlen, D_MIX), F32),
                   jax.ShapeDtypeStruct((batch, DEC_TILE, SSD_CONV_CH), F32),
                   jax.ShapeDtypeStruct((depth, batch, SSD_WIDTH, SSD_STATE), F32)],
        scratch_shapes=scratch,
        input_output_aliases=aliases,
        compiler_params=_cparams(1),
        name="ssd_decode",
    )(*args)
    return y, cso[:, DEC_TILE - hist:, :], sso


def _att_stack_cols(tq):
    return -(-(N_Q_HEADS // N_KV_HEADS) * tq // LANES) * LANES


def _att_kernel(*refs, L, nb, carry, n_alias, row0=0):
    TQ = CHUNK if L == CHUNK else SUBLANES
    W = WINDOW
    n_in = 12 if carry else 15
    refs = refs[:n_in] + refs[n_in + n_alias:]
    if carry:
        (q_ref, k_ref, v_ref, g_ref, sb_ref, sc_ref, sh_ref, sg_ref, cos_ref, sin_ref, sinks_ref, scw_ref,
         y_ref, ko_ref, vo_ref, sco_ref,
         kd, vt, xpad2) = refs
    else:
        (q_ref, k_ref, v_ref, g_ref, sb_ref, sc_ref, sh_ref, sg_ref, cos_ref, sin_ref, sinks_ref, scw_ref,
         sch_ref, kp_ref, vp_ref,
         y_ref, ko_ref, vo_ref, sco_ref,
         kd, vt, xpad2, pad_w, pad_kv, kc_scr, vc_scr) = refs
    i = pl.program_id(1)
    padded = L != TQ
    R = _att_stack_cols(TQ)

    def full(ref, scr):
        if not padded:
            return ref[...]
        scr[...] = jnp.zeros(scr.shape, F32)
        scr[0:L, :] = ref[row0:row0 + L, :]
        return scr[...]

    def store_rows(cols, val):
        y_ref[row0:row0 + L, cols] = val[0:L].astype(y_ref.dtype)

    if padded:
        vals = {name: full(ref, pad_w) for name, ref in
                (("q", q_ref), ("g", g_ref), ("sb", sb_ref), ("sc", sc_ref), ("sh", sh_ref), ("sg", sg_ref))}

        def get(name, cols=slice(None)):
            return vals[name][:, cols]
    else:
        srcs = {"q": q_ref, "g": g_ref, "sb": sb_ref, "sc": sc_ref, "sh": sh_ref, "sg": sg_ref}

        def get(name, cols=slice(None)):
            return srcs[name][:, cols]

    lane = lax.broadcasted_iota(jnp.int32, (TQ, LANES), 1)
    first_half = (lane % HEAD_DIM) < (HEAD_DIM // 2)
    cos = cos_ref[...]
    sin = sin_ref[...]

    def rope(x):
        sw = jnp.where(first_half, pltpu.roll(x, LANES - HEAD_DIM // 2, 1), pltpu.roll(x, HEAD_DIM // 2, 1))
        return x * cos + sw * sin

    k_new = full(k_ref, pad_kv if padded else None)
    k_rot = jnp.concatenate([rope(k_new[:, 0:LANES]), rope(k_new[:, LANES:2 * LANES])], axis=1)
    v_new = full(v_ref, pad_kv if padded else None)
    low_w = lax.broadcasted_iota(jnp.int32, (W, LANES), 1) < HEAD_DIM

    def stage_kv(slot, k_blk, v_blk):
        for kvp in range(N_KV_HEADS // 2):
            kpair = k_blk[:, kvp * LANES:(kvp + 1) * LANES]
            kroll = pltpu.roll(kpair, HEAD_DIM, 1)
            kd[slot, 2 * kvp] = jnp.where(low_w, kpair, kroll).astype(BF16)
            kd[slot, 2 * kvp + 1] = jnp.where(low_w, kroll, kpair).astype(BF16)
            vt[slot, kvp] = v_blk[:, kvp * LANES:(kvp + 1) * LANES].T.astype(BF16)

    def to_channel_major(x):
        return jnp.concatenate([x[:, 0:LANES].T, x[:, LANES:2 * LANES].T], axis=0)

    def to_key_major(xt):
        return jnp.concatenate([xt[0:LANES, :].T, xt[LANES:2 * LANES, :].T], axis=1)

    rowk = lax.broadcasted_iota(jnp.int32, (W, KV_WIDTH), 0)
    if carry:
        cur_slot = lax.rem(i, 2)

        @pl.when(i == 0)
        def _():
            kd[1] = jnp.zeros((N_KV_HEADS, W, LANES), BF16)
            vt[1] = jnp.zeros((N_KV_HEADS // 2, LANES, W), BF16)

        stage_kv(cur_slot, k_rot, v_new)

        @pl.when(i == nb - 1)
        def _():
            ko_ref[...] = to_channel_major(k_rot)
            vo_ref[...] = to_channel_major(v_new)
    else:
        k_prev = to_key_major(kp_ref[...])
        v_prev = to_key_major(vp_ref[...])
        kc_scr[...] = jnp.zeros((W, KV_WIDTH), F32)
        kc_scr[0:TQ, :] = k_rot
        vc_scr[...] = jnp.zeros((W, KV_WIDTH), F32)
        vc_scr[0:TQ, :] = v_new
        k_cur = kc_scr[...]
        v_cur = vc_scr[...]
        stage_kv(0, k_prev, v_prev)
        stage_kv(1, k_cur, v_cur)
        ko_ref[...] = to_channel_major(pltpu.roll(jnp.where(rowk < L, k_cur, k_prev), W - L, 0))
        vo_ref[...] = to_channel_major(pltpu.roll(jnp.where(rowk < L, v_cur, v_prev), W - L, 0))

    s_idx = lax.broadcasted_iota(jnp.int32, (W, R), 0)
    t_idx = lax.broadcasted_iota(jnp.int32, (W, R), 1) & (TQ - 1)
    cur_mask = s_idx <= t_idx
    prev_mask = s_idx > t_idx
    if carry:
        prev_mask = prev_mask & (i > 0)
        slot0_cur = cur_slot == 0
        masks = ((cur_mask & slot0_cur) | (prev_mask & jnp.logical_not(slot0_cur)),
                 (prev_mask & slot0_cur) | (cur_mask & jnp.logical_not(slot0_cur)))
    else:
        masks = (prev_mask, cur_mask)
    low = lane < HEAD_DIM
    heads_per_kv = N_Q_HEADS // N_KV_HEADS

    for kvh in range(N_KV_HEADS):
        kvp, half = divmod(kvh, 2)
        qs = []
        for qq in range(heads_per_kv // 2):
            qp = kvh * (heads_per_kv // 2) + qq
            q_rot = rope(get("q", slice(qp * LANES, (qp + 1) * LANES))) * (HEAD_DIM ** -0.5)
            qs += [jnp.where(low, q_rot, 0.0), jnp.where(low, 0.0, q_rot)]
        if heads_per_kv * TQ < R:
            qs.append(jnp.zeros((R - heads_per_kv * TQ, LANES), F32))
        q_stack = jnp.concatenate(qs, axis=0).astype(BF16)
        s = [jnp.where(masks[slot],
                       lax.dot_general(kd[slot, kvh], q_stack, (((1,), (1,)), ((), ())),
                                       preferred_element_type=F32), -jnp.inf) for slot in range(2)]
        sink = sinks_ref[kvh:kvh + 1, :]
        m = jnp.maximum(jnp.maximum(jnp.max(s[0], axis=0, keepdims=True), jnp.max(s[1], axis=0, keepdims=True)),
                        sink)
        p_un = [jnp.exp(s[slot] - m) for slot in range(2)]
        den = (jnp.sum(p_un[0], axis=0, keepdims=True) + jnp.sum(p_un[1], axis=0, keepdims=True)
               + jnp.exp(sink - m))
        o_t = (jnp.dot(vt[0, kvp], p_un[0].astype(BF16), preferred_element_type=F32)
               + jnp.dot(vt[1, kvp], p_un[1].astype(BF16), preferred_element_type=F32)) * (1.0 / den)
        for qq in range(heads_per_kv // 2):
            qp = kvh * (heads_per_kv // 2) + qq
            qcols = slice(qp * LANES, (qp + 1) * LANES)
            if carry:
                rows = slice(half * HEAD_DIM, (half + 1) * HEAD_DIM)
                o = jnp.concatenate([o_t[rows, (2 * qq) * TQ:(2 * qq + 1) * TQ],
                                     o_t[rows, (2 * qq + 1) * TQ:(2 * qq + 2) * TQ]], axis=0).T
            else:
                if qq == 0:
                    o_all = o_t.T
                oa = o_all[(2 * qq) * TQ:(2 * qq + 1) * TQ, :]
                ob = o_all[(2 * qq + 1) * TQ:(2 * qq + 2) * TQ, :]
                if half == 0:
                    ob = pltpu.roll(ob, HEAD_DIM, 1)
                else:
                    oa = pltpu.roll(oa, HEAD_DIM, 1)
                o = jnp.where(low, oa, ob)
            store_rows(qcols, o * _silu(get("g", qcols)))

    if carry:
        @pl.when(i == 0)
        def _():
            xpad2[0:SUBLANES, :] = jnp.zeros((SUBLANES, SC_WIDTH), F32)
    else:
        xpad2[0:SUBLANES, :] = sch_ref[...]

    u = get("sc") * get("sh")
    xpad2[SUBLANES:SUBLANES + TQ, :] = u
    base = SUBLANES - (SC_CONV - 1)
    conv = xpad2[base:base + TQ, :] * scw_ref[0:1, :]
    for j in range(1, SC_CONV):
        conv = conv + xpad2[base + j:base + j + TQ, :] * scw_ref[j:j + 1, :]
    ysc = (get("sb") * conv) * _silu(get("sg"))
    store_rows(slice(ATT_WIDTH, ATT_WIDTH + SC_WIDTH), ysc)

    tail = (base + L) // SUBLANES * SUBLANES
    tail_tile = xpad2[tail:tail + SUBLANES, :]
    if carry:
        @pl.when(i == nb - 1)
        def _():
            sco_ref[...] = tail_tile

        xpad2[0:SUBLANES, :] = tail_tile
    else:
        sco_ref[...] = tail_tile


def _att_decode_tile_kernel(*refs, L, parts, n_alias):
    n_in, n_out = 15, 4
    ins, outs, scratch = refs[:n_in], refs[n_in + n_alias:n_in + n_alias + n_out], refs[n_in + n_alias + n_out:]
    for s in range(parts):
        seq_ins = list(ins[:12]) + [r.at[s] for r in ins[12:]]
        seq_outs = [outs[0]] + [r.at[s] for r in outs[1:]]
        _att_kernel(*seq_ins, *seq_outs, *scratch, L=L, nb=1, carry=False, n_alias=0, row0=L * s)


def _att_mixer(proj, y_mix, sc_state_pad, k_all_prev, v_all_prev, kv_out_prev, prm, rope_tab, *, layer, depth,
               batch, seqlen):
    carry = sc_state_pad is None
    if seqlen >= CHUNK:
        L, nb = CHUNK, seqlen // CHUNK
        src = proj

        def blk(width, col):
            return pl.BlockSpec((L, width), lambda b, i: (b * nb + i, col // width))

        y_spec = pl.BlockSpec((L, ATT_WIDTH + SC_WIDTH), lambda b, i: (b * nb + i, 1))
        tq = CHUNK
    else:
        L, nb = seqlen, 1
        src = proj
        parts = SUBLANES // L
        assert parts * L == SUBLANES and batch % parts == 0

        def blk(width, col):
            return pl.BlockSpec((SUBLANES, width), lambda b, i: (b, col // width))

        y_spec = pl.BlockSpec((SUBLANES, ATT_WIDTH + SC_WIDTH), lambda b, i: (b, 1))
        tq = SUBLANES

    def const(shape):
        return pl.BlockSpec(shape, lambda b, i: (0,) * len(shape))

    cos_t, sin_t = rope_tab
    n_cols = _att_stack_cols(tq)
    sink_rows = jnp.repeat(prm["sinks"].reshape(N_KV_HEADS, N_Q_HEADS // N_KV_HEADS), tq, axis=1)
    sink_rows = jnp.pad(sink_rows, ((0, 0), (0, n_cols - sink_rows.shape[1])))
    in_specs = [blk(ATT_WIDTH, COL_Q), blk(KV_WIDTH, COL_K), blk(KV_WIDTH, COL_V), blk(ATT_WIDTH, COL_GATT),
                blk(SC_WIDTH, COL_BSC), blk(SC_WIDTH, COL_CSC), blk(SC_WIDTH, COL_HSC), blk(SC_WIDTH, COL_GSC),
                pl.BlockSpec((tq, LANES), lambda b, i: (i if carry else 0, 0)),
                pl.BlockSpec((tq, LANES), lambda b, i: (i if carry else 0, 0)),
                const((N_KV_HEADS, n_cols)), const((SC_CONV, SC_WIDTH))]
    args = [src] * 8 + [cos_t, sin_t, sink_rows, prm["sc_conv_w"]]
    scratch = [pltpu.VMEM((2, N_KV_HEADS, WINDOW, LANES), BF16),
               pltpu.VMEM((2, N_KV_HEADS // 2, LANES, WINDOW), BF16)]
    per_step = None if carry else parts
    kv_spec = pl.BlockSpec((None, per_step, KV_WIDTH, WINDOW), lambda b, i: (layer, b, 0, 0))
    sc_spec = pl.BlockSpec((per_step, SUBLANES, SC_WIDTH), lambda b, i: (b, 0, 0))
    if not carry:
        in_specs += [sc_spec, kv_spec, kv_spec]
        args += [sc_state_pad, k_all_prev, v_all_prev]
    scratch += [pltpu.VMEM((SUBLANES + tq, SC_WIDTH), F32)]
    if not carry:
        scratch += [pltpu.VMEM((tq, ATT_WIDTH), F32), pltpu.VMEM((tq, KV_WIDTH), F32),
                    pltpu.VMEM((WINDOW, KV_WIDTH), F32), pltpu.VMEM((WINDOW, KV_WIDTH), F32)]
    aliases = {len(args): 0}
    in_specs += [pl.BlockSpec(memory_space=pl.ANY)]
    args += [y_mix]
    if kv_out_prev is not None:
        for out_idx, buf in zip((1, 2), kv_out_prev):
            aliases[len(args)] = out_idx
            in_specs += [pl.BlockSpec(memory_space=pl.ANY)]
            args += [buf]
    if carry:
        body = functools.partial(_att_kernel, L=L, nb=nb, carry=True, n_alias=len(aliases))
    else:
        body = functools.partial(_att_decode_tile_kernel, L=L, parts=parts, n_alias=len(aliases))
    y, ko, vo, sco = pl.pallas_call(
        body,
        grid=(batch, nb) if carry else (batch // parts, 1),
        in_specs=in_specs,
        out_specs=[y_spec, kv_spec, kv_spec, sc_spec],
        out_shape=[jax.ShapeDtypeStruct(y_mix.shape, y_mix.dtype),
                   jax.ShapeDtypeStruct((depth, batch, KV_WIDTH, WINDOW), F32),
                   jax.ShapeDtypeStruct((depth, batch, KV_WIDTH, WINDOW), F32),
                   jax.ShapeDtypeStruct((batch, SUBLANES, SC_WIDTH), F32)],
        scratch_shapes=scratch,
        input_output_aliases=aliases,
        compiler_params=_cparams(2),
        name="att_prompt" if carry else "att_decode",
    )(*args)
    base = SUBLANES - (SC_CONV - 1)
    off = base + L - (base + L) // SUBLANES * SUBLANES
    return y, (ko, vo), sco[:, off:off + SC_CONV - 1, :]


OUT_TM = 1024
OUT_KT = 1024
OUT_NT = 512
N_KSTEPS = D_MIX // OUT_KT
N_NSTEPS = D_MODEL // OUT_NT


def _outproj_kernel(y_ref, wo_ref, x_ref, npost_ref, p_ref, pp_ref, gw_ref, o_ref, acc_ref, xnb_ref):
    k = pl.program_id(1)

    @pl.when(k == 0)
    def _():
        acc_ref[...] = jnp.dot(y_ref[...].astype(BF16), wo_ref[...], preferred_element_type=F32)

    @pl.when((k > 0) & (k < N_KSTEPS))
    def _():
        acc_ref[...] += jnp.dot(y_ref[...].astype(BF16), wo_ref[...], preferred_element_type=F32)

    @pl.when(k == N_KSTEPS - 1)
    def _():
        rows = min(256, acc_ref.shape[0])
        for r in range(acc_ref.shape[0] // rows):
            rs = slice(r * rows, (r + 1) * rows)
            mix = acc_ref[rs, :]
            ms = jnp.mean(mix * mix, axis=-1, keepdims=True)
            xn = x_ref[rs, :] + (mix * lax.rsqrt(ms + EPS)) * npost_ref[...]
            acc_ref[rs, :] = xn
            xnb_ref[rs, :] = xn.astype(BF16)

    for j in range(N_NSTEPS):
        @pl.when(k == N_KSTEPS + j)
        def _():
            gate = _sigmoid(jnp.dot(xnb_ref[...], gw_ref[...], preferred_element_type=F32))
            e = jnp.dot(p_ref[...].astype(BF16), pp_ref[...], preferred_element_type=F32)
            o_ref[...] = acc_ref[:, j * OUT_NT:(j + 1) * OUT_NT] + gate * e


def _out_proj(y_mix, x2d, p_all, prm, *, layer):
    t = x2d.shape[0]
    tm = min(OUT_TM, t)

    def nstep(k):
        return jnp.maximum(k - N_KSTEPS, 0)

    n_row_tiles = t // tm

    def row_tile(i, k, first_unused):
        return jnp.minimum(i + (k >= first_unused).astype(jnp.int32), n_row_tiles - 1)

    return pl.pallas_call(
        _outproj_kernel,
        grid=(n_row_tiles, N_KSTEPS + N_NSTEPS),
        in_specs=[
            pl.BlockSpec((tm, OUT_KT), lambda i, k: (row_tile(i, k, N_KSTEPS + 1),
                                                     jnp.where(k > N_KSTEPS, 0, jnp.minimum(k, N_KSTEPS - 1)))),
            pl.BlockSpec((OUT_KT, D_MODEL),
                         lambda i, k: (jnp.where(k > N_KSTEPS + 1, 0, jnp.minimum(k, N_KSTEPS - 1)), 0)),
            pl.BlockSpec((tm, D_MODEL), lambda i, k: (row_tile(i, k, N_KSTEPS), 0)),
            pl.BlockSpec((1, D_MODEL), lambda i, k: (0, 0)),
            pl.BlockSpec((None, tm, PLE_DIM), lambda i, k: (layer, i, 0)),
            pl.BlockSpec((PLE_DIM, OUT_NT), lambda i, k: (0, nstep(k))),
            pl.BlockSpec((D_MODEL, OUT_NT), lambda i, k: (0, nstep(k))),
        ],
        out_specs=pl.BlockSpec((tm, OUT_NT), lambda i, k: (i, nstep(k))),
        out_shape=jax.ShapeDtypeStruct((t, D_MODEL), F32),
        scratch_shapes=[pltpu.VMEM((tm, D_MODEL), F32),
                        pltpu.VMEM((tm, D_MODEL), BF16)],
        compiler_params=_cparams(2),
        name="out_proj",
    )(y_mix.reshape(t, D_MIX), prm["w_out"], x2d, prm["norm_post"], p_all, prm["ple_proj"], prm["ple_gate"])


_W_IN_SRC = {}
_o = 0
for _name, _size in (("z", SSD_WIDTH), ("xbc", SSD_CONV_CH), ("dt", SSD_HEADS), ("q", ATT_WIDTH), ("k", KV_WIDTH),
                     ("v", KV_WIDTH), ("g_att", ATT_WIDTH), ("b_sc", SC_WIDTH), ("c_sc", SC_WIDTH),
                     ("h_sc", SC_WIDTH), ("g_sc", SC_WIDTH)):
    _W_IN_SRC[_name] = (_o, _size)
    _o += _size
N_IN = _o
_W_IN_DST = {"z": COL_Z, "xbc": COL_X, "q": COL_Q, "g_att": COL_GATT, "b_sc": COL_BSC, "c_sc": COL_CSC,
             "h_sc": COL_HSC, "g_sc": COL_GSC, "k": COL_K, "v": COL_V}
REGROUP_COLS = 256
REGROUP_UNIT = 32


def _regroup_tables(layer):
    starts, valid = [], []
    for j in range(N_PROJ // REGROUP_COLS):
        dst = j * REGROUP_COLS
        src, n = 0, 0
        for name, seg_dst in _W_IN_DST.items():
            seg_src, width = _W_IN_SRC[name]
            if seg_dst <= dst < seg_dst + width:
                src, n = seg_src + dst - seg_dst, REGROUP_COLS
        if dst == COL_DT:
            src, n = _W_IN_SRC["dt"]
        assert src % REGROUP_UNIT == 0 and src + REGROUP_COLS <= N_IN
        starts.append((layer * N_IN + src) // REGROUP_UNIT)
        valid.append(n)
    return jnp.asarray(starts, jnp.int32), jnp.asarray(valid, jnp.int32)


def _regroup_kernel(starts_ref, valid_ref, w_ref, o_ref):
    col = lax.broadcasted_iota(jnp.int32, (D_MODEL, REGROUP_COLS), 1)
    o_ref[...] = jnp.where(col < valid_ref[pl.program_id(0)], w_ref[...].T, 0.0).astype(BF16)


def _regroup_w_in(w_cols, layer):
    starts, valid = _regroup_tables(layer)
    return pl.pallas_call(
        _regroup_kernel,
        grid_spec=pltpu.PrefetchScalarGridSpec(
            num_scalar_prefetch=2,
            grid=(N_PROJ // REGROUP_COLS,),
            in_specs=[pl.BlockSpec((pl.Element(REGROUP_COLS), pl.Element(D_MODEL)),
                                   lambda j, s, v: (s[j] * REGROUP_UNIT, 0))],
            out_specs=pl.BlockSpec((D_MODEL, REGROUP_COLS), lambda j, s, v: (0, j))),
        out_shape=jax.ShapeDtypeStruct((D_MODEL, N_PROJ), BF16),
        compiler_params=_cparams(1),
        name="regroup_w_in",
    )(starts, valid, w_cols)


def _lane_pad(v):
    return jnp.pad(v, (0, LANES - v.shape[0])).reshape(1, LANES)


def _layer_params(i, w_in, w_out, norm_pre, norm_post, ssd_conv_w, ssd_conv_b, ssd_a_log, ssd_dt_bias, ssd_d,
                  ssd_norm, attn_sinks, sc_conv_w, ple_proj, ple_gate, expand):
    return {
        "w_in": _regroup_w_in(w_in, i),
        "w_out": w_out[i].astype(BF16),
        "norm_pre": norm_pre[i].reshape(1, D_MODEL),
        "norm_post": norm_post[i].reshape(1, D_MODEL),
        "ssd_conv_w": ssd_conv_w[i],
        "ssd_conv_b": ssd_conv_b[i].reshape(1, SSD_CONV_CH),
        "a_log": _lane_pad(ssd_a_log[i]),
        "dt_bias": _lane_pad(ssd_dt_bias[i]),
        "d_skip": jnp.repeat(ssd_d[i], SSD_HEAD_DIM).reshape(1, SSD_WIDTH),
        "ssd_norm": ssd_norm[i].reshape(1, SSD_WIDTH),
        "sinks": attn_sinks[i],
        "sc_conv_w": sc_conv_w[i],
        "ple_proj": ple_proj[i].astype(BF16),
        "ple_gate": ple_gate[i].astype(BF16),
        "expand": expand,
    }


def _rope_tables(pos0, n):
    half = HEAD_DIM // 2
    inv_freq = jnp.exp(-math.log(ROPE_THETA) * jnp.arange(half, dtype=F32) * (2.0 / HEAD_DIM))
    pos = pos0 + jnp.arange(n, dtype=jnp.int32)
    ang = pos.astype(F32)[:, None] * inv_freq[None, :]
    cos, sin = jnp.cos(ang), jnp.sin(ang)
    reps = LANES // HEAD_DIM
    return (jnp.tile(jnp.concatenate([cos, cos], axis=1), (1, reps)),
            jnp.tile(jnp.concatenate([-sin, sin], axis=1), (1, reps)))


def _layer(x2d, p_all, prm, rope_tab, carried, *, layer, depth, batch, seqlen, conv_state, ssm_all, k_all, v_all,
           sc_state):
    decode = conv_state is not None
    ssm_out_prev, kv_out_prev = carried
    proj = _in_proj(x2d, prm["norm_pre"], prm["w_in"])
    if decode:
        sc_pad = jnp.pad(sc_state, ((0, 0), (SUBLANES - (SC_CONV - 1), 0), (0, 0)))
        y_mix, conv_new, ssm_out = _ssd_decode_mixer(proj, conv_state, ssm_all, ssm_out_prev, prm, layer=layer,
                                                     depth=depth, batch=batch, seqlen=seqlen)
    else:
        sc_pad = None
        y_mix, conv_new, ssm_out = _ssd_mixer(proj, ssm_out_prev, prm, layer=layer, depth=depth, batch=batch,
                                              seqlen=seqlen)
    y_mix, kv_out, sc_new = _att_mixer(proj, y_mix, sc_pad, k_all, v_all, kv_out_prev, prm, rope_tab, layer=layer,
                                       depth=depth, batch=batch, seqlen=seqlen)
    x_new = _out_proj(y_mix, x2d, p_all, prm, layer=layer)
    return x_new, conv_new, sc_new, (ssm_out, kv_out)


def kernel(x_prompt, x_sample, p_prompt, p_sample, state_ssd_conv, state_ssm, cache_k, cache_v, state_sc_conv,
           w_in, w_out, norm_pre, norm_post, ssd_conv_w, ssd_conv_b, ssd_a_log, ssd_dt_bias, ssd_d, ssd_norm,
           attn_sinks, sc_conv_w, ple_proj, ple_gate):
    bp, lp, _ = x_prompt.shape
    bs, ls, _ = x_sample.shape
    depth = w_in.shape[0]
    assert lp % CHUNK == 0 and ls + SSD_CONV - 1 <= DEC_TILE and bs % DEC_G == 0 and cache_k.shape[2] == WINDOW
    head = jnp.arange(LANES, dtype=jnp.int32)[:, None]
    chan = jnp.arange(SSD_WIDTH, dtype=jnp.int32)[None, :]
    expand = (chan // SSD_HEAD_DIM == head).astype(BF16)
    rope_p = _rope_tables(0, lp)
    rope_s = _rope_tables(PAST_LEN, SUBLANES)
    yp = x_prompt.reshape(bp * lp, D_MODEL)
    ys = x_sample.reshape(bs * ls, D_MODEL)
    pp_all = p_prompt.reshape(depth, bp * lp, PLE_DIM)
    ps_all = p_sample.reshape(depth, bs * ls, PLE_DIM)
    ssm_all = state_ssm.reshape(depth, bs, SSD_WIDTH, SSD_STATE)
    w_cols = jnp.swapaxes(w_in, 1, 2).reshape(depth * N_IN, D_MODEL)
    k_all = cache_k.transpose(0, 1, 3, 4, 2).reshape(depth, bs, KV_WIDTH, WINDOW)
    v_all = cache_v.transpose(0, 1, 3, 4, 2).reshape(depth, bs, KV_WIDTH, WINDOW)
    carried_p = carried_s = (None, None)
    conv_p, sc_p, conv_s, sc_s = [], [], [], []
    for i in range(depth):
        prm = _layer_params(i, w_cols, w_out, norm_pre, norm_post, ssd_conv_w, ssd_conv_b, ssd_a_log, ssd_dt_bias,
                            ssd_d, ssd_norm, attn_sinks, sc_conv_w, ple_proj, ple_gate, expand)
        yp, cv, sc, carried_p = _layer(yp, pp_all, prm, rope_p, carried_p, layer=i, depth=depth, batch=bp, seqlen=lp,
                                       conv_state=None, ssm_all=None, k_all=None, v_all=None, sc_state=None)
        conv_p.append(cv)
        sc_p.append(sc)
        ys, cv, sc, carried_s = _layer(ys, ps_all, prm, rope_s, carried_s, layer=i, depth=depth, batch=bs, seqlen=ls,
                                       conv_state=state_ssd_conv[i], ssm_all=ssm_all, k_all=k_all, v_all=v_all,
                                       sc_state=state_sc_conv[i])
        conv_s.append(cv)
        sc_s.append(sc)

    def states(conv, carried, sc, batch):
        ssm_out, (k_out, v_out) = carried
        return (jnp.stack(conv), ssm_out.reshape(depth, batch, SSD_HEADS, SSD_HEAD_DIM, SSD_STATE),
                k_out.reshape(depth, batch, N_KV_HEADS, HEAD_DIM, WINDOW).transpose(0, 1, 4, 2, 3),
                v_out.reshape(depth, batch, N_KV_HEADS, HEAD_DIM, WINDOW).transpose(0, 1, 4, 2, 3), jnp.stack(sc))

    return (yp.reshape(bp, lp, D_MODEL), ys.reshape(bs, ls, D_MODEL),
            *states(conv_p, carried_p, sc_p, bp), *states(conv_s, carried_s, sc_s, bs))
```
